```python
import math
import jax
import jax.numpy as jnp
from jax import lax
import numpy as np

D_MODEL = 1024
BATCH = 2
SEQ = 16384
DEPTH = 2

GRID_W = 64
CTX_LEN = 256
BRANCH_W = D_MODEL // 2
N_BRANCHES = 4
SHORT_CONV = 3
EPS = 1e-6
MASK_NEG = -1e30
LB_FLOOR = 1e-30

SSD_INNER = BRANCH_W
SSD_HEAD_DIM = 64
SSD_HEADS = SSD_INNER // SSD_HEAD_DIM
SSD_GROUPS = 2
SSD_STATE = 64
SSD_CHUNK = 128
SSD_CONV_CH = SSD_INNER + 2 * SSD_GROUPS * SSD_STATE
DT_MIN = 1e-3
DT_MAX = 1e-1

HY_WIDTH = BRANCH_W
HY_ORDER = 2
HY_HIDDEN = 64
HY_BANDS = 8
HY_POS_DIM = 1 + 2 * HY_BANDS
HY_DECAY_TARGET = 1e-2
HY_SHORT_PCT = 0.3
HY_LONG_PCT = 1.5

ML_HEADS = 4
ML_HEAD_DIM = BRANCH_W // ML_HEADS
ML_WIDTH = BRANCH_W
ML_CHUNK = 128

HG_HEADS = 4
HG_EXPAND = BRANCH_W // HG_HEADS
HG_HEAD_DIM = BRANCH_W // HG_HEADS
HG_WIDTH = BRANCH_W
HG_CHUNK = 64
HG_SUB = 8

N_EXPERTS = 16
EC_CAPACITY = 2
D_EXPERT = 2 * D_MODEL

SSD_COLS = SSD_INNER + SSD_CONV_CH + 2 * SSD_HEADS
HY_COLS = (HY_ORDER + 1) * HY_WIDTH
ML_COLS = 4 * ML_WIDTH + 4 * ML_HEADS
HG_COLS = 5 * HG_WIDTH
GATE_COLS = N_BRANCHES * D_MODEL
PROJ_CUTS = (SSD_COLS, SSD_COLS + HY_COLS, SSD_COLS + HY_COLS + ML_COLS,
             SSD_COLS + HY_COLS + ML_COLS + HG_COLS)
PROJ_COLS = PROJ_CUTS[-1] + GATE_COLS

kernel_name = 'hybrid_bidir_ssd_hyena_mlstm_hgrn2_ecmoe'


def rmsnorm(x, g):
    xf = x.astype(jnp.float32)
    y = xf * lax.rsqrt(jnp.mean(xf * xf, axis=-1, keepdims=True) + EPS)
    return (y * g.astype(jnp.float32)).astype(x.dtype)


def head_rmsnorm(y, g):
    yf = y.astype(jnp.float32)
    yf = yf * lax.rsqrt(jnp.mean(yf * yf, axis=-1, keepdims=True) + EPS)
    return yf.reshape(y.shape[:2] + (-1,)) * g.astype(jnp.float32)


def modulate(h, shift, scale):
    return h * (1.0 + scale) + shift


def masked_exp(mask, logd):
    return jnp.where(mask, jnp.exp(jnp.where(mask, logd, 0.0)), 0.0)


def short_conv(u, w, bias):
    y = lax.conv_general_dilated(u, w[:, None, :].astype(u.dtype), window_strides=(1,), padding='SAME',
                                 dimension_numbers=('NWC', 'WIO', 'NWC'), feature_group_count=u.shape[-1])
    return y + bias.astype(u.dtype)


def to_col_major(t):
    b, n = t.shape[:2]
    rows = n // GRID_W
    rest = t.shape[2:]
    return t.reshape((b, rows, GRID_W) + rest).swapaxes(1, 2).reshape((b, n) + rest)


def to_row_major(t):
    b, n = t.shape[:2]
    rows = n // GRID_W
    rest = t.shape[2:]
    return t.reshape((b, GRID_W, rows) + rest).swapaxes(1, 2).reshape((b, n) + rest)


def bidirectional(scan_fn, ctx_fwd, ctx_bwd, lat_fwd, lat_bwd, init):
    flip = lambda ts: tuple(jnp.flip(t, axis=1) for t in ts)
    yc_f, sc_f = scan_fn(*ctx_fwd, init)
    yl_f, _ = scan_fn(*lat_fwd, sc_f)
    yc_b, sc_b = scan_fn(*flip(ctx_bwd), init)
    yl_b, _ = scan_fn(*flip(lat_bwd), sc_b)
    return yc_f + jnp.flip(yc_b, axis=1), yl_f + jnp.flip(yl_b, axis=1)


def ssd_scan(xs, dt, da, bm, cm, h0):
    b, n, H, P = xs.shape
    G = bm.shape[2]
    nc = n // SSD_CHUNK
    rep = H // G
    chunk = lambda t: t.reshape((b, nc, SSD_CHUNK) + t.shape[2:])
    xs, dt, da = chunk(xs), chunk(dt), chunk(da)
    bh = chunk(jnp.repeat(bm, rep, axis=2))
    ch = chunk(jnp.repeat(cm, rep, axis=2))
    acs = jnp.cumsum(da, axis=2)
    tri = jnp.tril(jnp.ones((SSD_CHUNK, SSD_CHUNK), bool))
    decay = masked_exp(tri[:, :, None], acs[:, :, :, None, :] - acs[:, :, None, :, :])
    scores = jnp.einsum('bcthn,bcshn->bctsh', ch, bh) * decay * dt[:, :, None, :, :]
    y_diag = jnp.einsum('bctsh,bcshp->bcthp', scores, xs)
    to_end = jnp.exp(acs[:, :, -1:, :] - acs) * dt
    states = jnp.einsum('bcsh,bcshn,bcshp->bchpn', to_end, bh, xs)
    chunk_decay = jnp.exp(acs[:, :, -1, :])

    def step(h, inp):
        dec, st = inp
        return dec[:, :, None, None] * h + st, h

    final, h_prev = lax.scan(step, h0, (chunk_decay.swapaxes(0, 1), states.swapaxes(0, 1)))
    h_prev = h_prev.swapaxes(0, 1)
    y_off = jnp.einsum('bcthn,bchpn->bcthp', ch, h_prev) * jnp.exp(acs)[..., None]
    return (y_diag + y_off).reshape(b, n, H, P), final


def ssd_inputs(p, conv_w, conv_b, dt_bias, a_log):
    b, n = p.shape[:2]
    z, xbc, dt = jnp.split(p, [SSD_INNER, SSD_INNER + SSD_CONV_CH], axis=-1)
    xbc = jax.nn.silu(short_conv(xbc, conv_w, conv_b))
    xs, bm, cm = jnp.split(xbc, [SSD_INNER, SSD_INNER + SSD_GROUPS * SSD_STATE], axis=-1)
    xs = xs.reshape(b, n, SSD_HEADS, SSD_HEAD_DIM)
    bm = bm.reshape(b, n, SSD_GROUPS, SSD_STATE)
    cm = cm.reshape(b, n, SSD_GROUPS, SSD_STATE)
    dt = jax.nn.softplus(dt.astype(jnp.float32).reshape(b, n, 2, SSD_HEADS) + dt_bias.astype(jnp.float32))
    da = dt * -jnp.exp(a_log.astype(jnp.float32))
    fwd = (xs, dt[:, :, 0], da[:, :, 0], bm, cm)
    bwd = (xs, dt[:, :, 1], da[:, :, 1], bm, cm)
    return z, xs, fwd, bwd


def ssd_branch(p_lat, p_ctx, conv_w, conv_b, dt_bias, a_log, d_skip, norm_g):
    lat = ssd_inputs(p_lat, conv_w, conv_b, dt_bias, a_log)
    ctx = ssd_inputs(p_ctx, conv_w, conv_b, dt_bias, a_log)
    h0 = jnp.zeros((p_lat.shape[0], SSD_HEADS, SSD_HEAD_DIM, SSD_STATE), jnp.float32)
    y_ctx, y_lat = bidirectional(ssd_scan, ctx[2], ctx[3], lat[2], lat[3], h0)

    def finish(y, inp):
        z, xs = inp[0], inp[1]
        y = (y + d_skip.astype(jnp.float32)[:, None] * xs).reshape(z.shape)
        return rmsnorm(y * jax.nn.silu(z.astype(jnp.float32)), norm_g)

    return finish(y_lat, lat), finish(y_ctx, ctx)


def hyena_filter_spectra(n, w1, b1, w2, b2, w3, freq):
    f32 = jnp.float32
    pos = jnp.arange(n, dtype=f32)
    t = pos / max(n - 1, 1)
    bands = jnp.arange(1, HY_BANDS + 1, dtype=f32)
    ang = (2.0 * math.pi / n) * pos[:, None] * bands[None, :]
    feats = jnp.concatenate([t[:, None], jnp.cos(ang), jnp.sin(ang)], axis=-1)
    hid = jnp.sin(freq[0] * (feats @ w1 + b1))
    hid = jnp.sin(freq[1] * (hid @ w2 + b2))
    h = (hid @ w3).reshape(n, HY_ORDER, 2, HY_WIDTH)
    deltas = jnp.abs(jnp.linspace(math.log(HY_DECAY_TARGET) / HY_LONG_PCT,
                                  math.log(HY_DECAY_TARGET) / HY_SHORT_PCT, HY_WIDTH, dtype=f32))
    h = h * jnp.exp(-t[:, None] * deltas[None, :])[:, None, None, :]
    two_sided = jnp.concatenate([h[:, :, 0], jnp.zeros((1, HY_ORDER, HY_WIDTH), f32),
                                 jnp.flip(h[1:, :, 1], axis=0)], axis=0)
    two_sided = two_sided / (jnp.sum(jnp.abs(two_sided), axis=0, keepdims=True) + EPS)
    return jnp.fft.rfft(two_sided, axis=0)


def fft_longconv(u, spec, skip):
    n = u.shape[1]
    uf = jnp.fft.rfft(u, n=2 * n, axis=1)
    y = jnp.fft.irfft(uf * spec, n=2 * n, axis=1)[:, :n]
    return y + u * skip


def hyena_branch(p, conv_w, conv_b, w1, b1, w2, b2, w3, freq, skip):
    f32 = jnp.float32
    u = short_conv(p, conv_w, conv_b).astype(f32)
    parts = jnp.split(u, HY_ORDER + 1, axis=-1)
    spec = hyena_filter_spectra(p.shape[1], w1.astype(f32), b1.astype(f32), w2.astype(f32),
                                b2.astype(f32), w3.astype(f32), freq.astype(f32))
    z = parts[0]
    for o in range(HY_ORDER):
        z = parts[o + 1] * fft_longconv(z, spec[:, o], skip[o].astype(f32))
    return z


def mlstm_scan(q, k, v, logi, logf, state):
    b, n, H, _ = q.shape
    nc = n // ML_CHUNK
    chunk = lambda t: t.reshape((b, nc, ML_CHUNK) + t.shape[2:])
    q, k, v, logi, logf = chunk(q), chunk(k), chunk(v), chunk(logi), chunk(logf)
    bcum = jnp.cumsum(logf, axis=2)
    btot = bcum[:, :, -1]
    a = btot[:, :, None] - bcum + logi
    m_loc = jnp.max(a, axis=2)
    w_loc = jnp.exp(a - m_loc[:, :, None])
    c_loc = jnp.einsum('bcsh,bcshv,bcshk->bchvk', w_loc, v, k)
    n_loc = jnp.einsum('bcsh,bcshk->bchk', w_loc, k)

    def step(carry, inp):
        c_st, n_st, m_st = carry
        g, cl, nl, ml = inp
        m_new = jnp.maximum(g + m_st, ml)
        a_old, a_new = jnp.exp(g + m_st - m_new), jnp.exp(ml - m_new)
        c_new = a_old[..., None, None] * c_st + a_new[..., None, None] * cl
        n_new = a_old[..., None] * n_st + a_new[..., None] * nl
        return (c_new, n_new, m_new), carry

    sw = lambda t: t.swapaxes(0, 1)
    final, (c_prev, n_prev, m_prev) = lax.scan(step, state, (sw(btot), sw(c_loc), sw(n_loc), sw(m_loc)))
    c_prev, n_prev, m_prev = sw(c_prev), sw(n_prev), sw(m_prev)
    tri = jnp.tril(jnp.ones((ML_CHUNK, ML_CHUNK), bool))[:, :, None]
    dmat = jnp.where(tri, bcum[:, :, :, None, :] - bcum[:, :, None, :, :] + logi[:, :, None, :, :],
                     MASK_NEG)
    inter = bcum + m_prev[:, :, None, :]
    m_t = jnp.maximum(inter, jnp.max(dmat, axis=3))
    w = masked_exp(tri, dmat - m_t[:, :, :, None, :]) * jnp.einsum('bcthk,bcshk->bctsh', q, k)
    w_int = jnp.exp(inter - m_t)
    num = jnp.einsum('bctsh,bcshv->bcthv', w, v) + w_int[..., None] * jnp.einsum('bcthk,bchvk->bcthv', q, c_prev)
    den = jnp.sum(w, axis=3) + w_int * jnp.einsum('bcthk,bchk->bcth', q, n_prev)
    h = num / jnp.maximum(jnp.abs(den), jnp.exp(-m_t))[..., None]
    return h.reshape(b, n, H, -1), final


def mlstm_inputs(p, conv_w, conv_b, i_bias, f_bias):
    b, n = p.shape[:2]
    qk, v, o, gates = jnp.split(p, [2 * ML_WIDTH, 3 * ML_WIDTH, 4 * ML_WIDTH], axis=-1)
    q, k = jnp.split(jax.nn.silu(short_conv(qk, conv_w, conv_b)), 2, axis=-1)
    heads = lambda t: t.reshape(b, n, ML_HEADS, ML_HEAD_DIM)
    q, k, v = heads(q), heads(k) * (ML_HEAD_DIM ** -0.5), heads(v)
    gates = gates.astype(jnp.float32).reshape(b, n, 4, ML_HEADS)
    logi = gates[:, :, :2] + i_bias.astype(jnp.float32)
    logf = jax.nn.log_sigmoid(gates[:, :, 2:] + f_bias.astype(jnp.float32))
    fwd = (q, k, v, logi[:, :, 0], logf[:, :, 0])
    bwd = (q, k, v, logi[:, :, 1], logf[:, :, 1])
    return o, fwd, bwd


def mlstm_branch(p_lat, p_ctx, conv_w, conv_b, i_bias, f_bias, norm_g):
    o_l, lat_f, lat_b = mlstm_inputs(p_lat, conv_w, conv_b, i_bias, f_bias)
    o_c, ctx_f, ctx_b = mlstm_inputs(p_ctx, conv_w, conv_b, i_bias, f_bias)
    b = p_lat.shape[0]
    s0 = (jnp.zeros((b, ML_HEADS, ML_HEAD_DIM, ML_HEAD_DIM), jnp.float32),
          jnp.zeros((b, ML_HEADS, ML_HEAD_DIM), jnp.float32),
          jnp.zeros((b, ML_HEADS), jnp.float32))
    h_c, h_l = bidirectional(mlstm_scan, ctx_f, ctx_b, lat_f, lat_b, s0)
    fin = lambda hh, o: jax.nn.sigmoid(o.astype(jnp.float32)) * head_rmsnorm(hh, norm_g)
    return fin(h_l, o_l), fin(h_c, o_c)


def hgrn2_scan(q, k, v, logf, s0):
    b, n, H, K = q.shape
    V = v.shape[-1]
    nc, ns = n // HG_CHUNK, HG_CHUNK // HG_SUB
    sub = lambda t: t.reshape((b, nc, ns, HG_SUB) + t.shape[2:])
    bq = jnp.cumsum(logf.reshape(b, nc, HG_CHUNK, H, K), axis=2)
    bs = bq.reshape(b, nc, ns, HG_SUB, H, K)
    qs, ks, vs = sub(q), sub(k), sub(v)
    end = bs[:, :, :, -1]
    start = jnp.concatenate([jnp.zeros_like(end[:, :, :1]), end[:, :, :-1]], axis=2)
    tri = jnp.tril(jnp.ones((HG_SUB, HG_SUB), bool))
    pair = masked_exp(tri[:, :, None, None], bs[:, :, :, :, None] - bs[:, :, :, None])
    att = jnp.einsum('bcjthk,bcjshk,bcjtshk->bcjtsh', qs, ks, pair)
    o = jnp.einsum('bcjtsh,bcjshv->bcjthv', att, vs)
    earlier = jnp.tril(jnp.ones((ns, ns), bool), -1)
    qx = qs * jnp.exp(bs - start[:, :, :, None])
    kx = ks[:, :, None] * masked_exp(earlier[:, :, None, None, None],
                                     start[:, :, :, None, None] - bs[:, :, None])
    att_x = jnp.einsum('bcithk,bcijshk->bcitjsh', qx, kx)
    o = (o + jnp.einsum('bcitjsh,bcjshv->bcithv', att_x, vs)).reshape(b, nc, HG_CHUNK, H, V)
    qc = q.reshape(b, nc, HG_CHUNK, H, K)
    kc = k.reshape(b, nc, HG_CHUNK, H, K)
    vc = v.reshape(b, nc, HG_CHUNK, H, V)
    btot = bq[:, :, -1]
    s_loc = jnp.einsum('bcshk,bcshv->bchkv', kc * jnp.exp(btot[:, :, None] - bq), vc)

    def step(s, inp):
        dec, sl = inp
        return dec[..., None] * s + sl, s

    final, s_prev = lax.scan(step, s0, (jnp.exp(btot).swapaxes(0, 1), s_loc.swapaxes(0, 1)))
    s_prev = s_prev.swapaxes(0, 1)
    o = o + jnp.einsum('bcthk,bchkv->bcthv', qc * jnp.exp(bq), s_prev)
    return o.reshape(b, n, H, V), final


def hgrn2_inputs(p, lb):
    b, n = p.shape[:2]
    q, i, f_fwd, f_bwd, g = jnp.split(p, 5, axis=-1)
    q = jax.nn.silu(q).reshape(b, n, HG_HEADS, HG_EXPAND)
    i = i.reshape(b, n, HG_HEADS, HG_HEAD_DIM)
    lb = lb.reshape(HG_HEADS, HG_EXPAND)
    log_lb, log_ub = jnp.log(jnp.maximum(lb, LB_FLOOR)), jnp.log1p(-lb)

    def direction(pre):
        pre = pre.astype(jnp.float32).reshape(b, n, HG_HEADS, HG_EXPAND)
        logf = jnp.logaddexp(log_lb, log_ub + jax.nn.log_sigmoid(pre))
        key = (1.0 - lb) * jax.nn.sigmoid(-pre)
        return (q, key, i, logf)

    return g, direction(f_fwd), direction(f_bwd)


def hgrn2_branch(p_lat, p_ctx, lb, norm_g):
    g_l, lat_f, lat_b = hgrn2_inputs(p_lat, lb)
    g_c, ctx_f, ctx_b = hgrn2_inputs(p_ctx, lb)
    lat_f = tuple(to_col_major(t) for t in lat_f)
    lat_b = tuple(to_col_major(t) for t in lat_b)
    s0 = jnp.zeros((p_lat.shape[0], HG_HEADS, HG_EXPAND, HG_HEAD_DIM), jnp.float32)
    o_c, o_l = bidirectional(hgrn2_scan, ctx_f, ctx_b, lat_f, lat_b, s0)
    o_l = to_row_major(o_l)
    fin = lambda o, g: head_rmsnorm(o, norm_g) * jax.nn.sigmoid(g.astype(jnp.float32))
    return fin(o_l, g_l), fin(o_c, g_c)


def merge_branches(ys, gate_pre, w_branch, w_out):
    b, n = gate_pre.shape[:2]
    gates = jax.nn.sigmoid(gate_pre.astype(jnp.float32)).reshape(b, n, N_BRANCHES, D_MODEL)
    merged = gates[:, :, 0] * (ys[0].astype(w_branch.dtype) @ w_branch[0])
    for i in range(1, N_BRANCHES):
        merged = merged + gates[:, :, i] * (ys[i].astype(w_branch.dtype) @ w_branch[i])
    return merged.astype(w_out.dtype) @ w_out


def token_mixer(h, hc, ctx_out, w_in, ssd_conv_w, ssd_conv_b, ssd_dt_bias, ssd_a_log, ssd_d, ssd_norm_g,
                hy_conv_w, hy_conv_b, hy_w1, hy_b1, hy_w2, hy_b2, hy_w3, hy_freq, hy_skip,
                ml_conv_w, ml_conv_b, ml_i_bias, ml_f_bias, ml_norm_g, hg_lb, hg_norm_g, w_branch, w_out):
    pa, pb, pc, pd, pg = jnp.split(h @ w_in, PROJ_CUTS, axis=-1)
    qa, qb, qc, qd, qg = jnp.split(hc @ w_in, PROJ_CUTS, axis=-1)
    ya, ya_c = ssd_branch(pa, qa, ssd_conv_w, ssd_conv_b, ssd_dt_bias, ssd_a_log, ssd_d, ssd_norm_g)
    yc, yc_c = mlstm_branch(pc, qc, ml_conv_w, ml_conv_b, ml_i_bias, ml_f_bias, ml_norm_g)
    yd, yd_c = hgrn2_branch(pd, qd, hg_lb, hg_norm_g)
    hyena = lambda p: hyena_branch(p, hy_conv_w, hy_conv_b, hy_w1, hy_b1, hy_w2, hy_b2, hy_w3, hy_freq, hy_skip)
    y = merge_branches((ya, hyena(pb), yc, yd), pg, w_branch, w_out)
    if not ctx_out:
        return y, None
    y_c = merge_branches((ya_c, hyena(qb), yc_c, yd_c), qg, w_branch, w_out)
    return y, y_c


def expert_choice_ffn(h, w_router, w_gate, w_up, w_down):
    b, n, d = h.shape
    cap = EC_CAPACITY * n // N_EXPERTS
    aff = jax.nn.softmax((h @ w_router).astype(jnp.float32), axis=-1)
    score, idx = lax.top_k(jnp.swapaxes(aff, 1, 2), cap)
    xg = jax.vmap(lambda hh, ii: hh[ii])(h, idx)
    hid = jax.nn.silu(jnp.einsum('becd,edf->becf', xg, w_gate)) * jnp.einsum('becd,edf->becf', xg, w_up)
    ye = jnp.einsum('becf,efd->becd', hid, w_down) * score[..., None].astype(h.dtype)
    return jax.vmap(lambda ii, yy: jnp.zeros((n, d), yy.dtype).at[ii].add(yy))(
        idx.reshape(b, -1), ye.reshape(b, -1, d))


def setup_inputs(seed: int = 0) -> dict:
    key = jax.random.key(seed)
    ks = iter(jax.random.split(key, 48))
    f32 = jnp.float32

    def nrm(shape, scale):
        return jax.random.normal(next(ks), shape, f32) * scale

    def gain(shape):
        return 1.0 + nrm(shape, 0.02)

    dt = jnp.exp(jax.random.uniform(next(ks), (DEPTH, 2, SSD_HEADS), f32, math.log(DT_MIN), math.log(DT_MAX)))
    a_init = jax.random.uniform(next(ks), (DEPTH, 2, SSD_HEADS), f32, 1.0, 16.0)
    return {
        'x': nrm((BATCH, SEQ, D_MODEL), 1.0),
        'c': nrm((BATCH, D_MODEL), 1.0),
        'ctx': nrm((BATCH, CTX_LEN, D_MODEL), 1.0),
        'c_ctx': nrm((D_MODEL,), 1.0),
        'w_ada': nrm((DEPTH, D_MODEL, 6 * D_MODEL), 0.5 * D_MODEL ** -0.5),
        'b_ada': nrm((DEPTH, 6 * D_MODEL), 0.01),
        'norm1_g': gain((DEPTH, D_MODEL)),
        'norm2_g': gain((DEPTH, D_MODEL)),
        'w_in': nrm((DEPTH, D_MODEL, PROJ_COLS), D_MODEL ** -0.5),
        'ssd_conv_w': nrm((DEPTH, SHORT_CONV, SSD_CONV_CH), SHORT_CONV ** -0.5),
        'ssd_conv_b': nrm((DEPTH, SSD_CONV_CH), 0.01),
        'ssd_dt_bias': dt + jnp.log(-jnp.expm1(-dt)),
        'ssd_a_log': jnp.log(a_init),
        'ssd_d': 1.0 + nrm((DEPTH, SSD_HEADS), 0.1),
        'ssd_norm_g': gain((DEPTH, SSD_INNER)),
        'hy_conv_w': nrm((DEPTH, SHORT_CONV, HY_COLS), SHORT_CONV ** -0.5),
        'hy_conv_b': nrm((DEPTH, HY_COLS), 0.01),
        'hy_w1': nrm((DEPTH, HY_POS_DIM, HY_HIDDEN), HY_POS_DIM ** -0.5),
        'hy_b1': nrm((DEPTH, HY_HIDDEN), 0.01),
        'hy_w2': nrm((DEPTH, HY_HIDDEN, HY_HIDDEN), HY_HIDDEN ** -0.5),
        'hy_b2': nrm((DEPTH, HY_HIDDEN), 0.01),
        'hy_w3': nrm((DEPTH, HY_HIDDEN, HY_ORDER * 2 * HY_WIDTH), HY_HIDDEN ** -0.5),
        'hy_freq': 1.0 + nrm((DEPTH, 2, HY_HIDDEN), 0.1),
        'hy_skip': nrm((DEPTH, HY_ORDER, HY_WIDTH), 0.5),
        'ml_conv_w': nrm((DEPTH, SHORT_CONV, 2 * ML_WIDTH), SHORT_CONV ** -0.5),
        'ml_conv_b': nrm((DEPTH, 2 * ML_WIDTH), 0.01),
        'ml_i_bias': nrm((DEPTH, 2, ML_HEADS), 0.1),
        'ml_f_bias': jnp.linspace(3.0, 6.0, ML_HEADS, dtype=f32) + nrm((DEPTH, 2, ML_HEADS), 0.1),
        'ml_norm_g': gain((DEPTH, ML_WIDTH)),
        'hg_lb_logits': nrm((DEPTH, HG_HEADS * HG_EXPAND), 0.1),
        'hg_norm_g': gain((DEPTH, HG_WIDTH)),
        'w_branch': nrm((DEPTH, N_BRANCHES, BRANCH_W, D_MODEL), BRANCH_W ** -0.5),
        'w_out': nrm((DEPTH, D_MODEL, D_MODEL), D_MODEL ** -0.5),
        'w_router': nrm((DEPTH, D_MODEL, N_EXPERTS), D_MODEL ** -0.5),
        'w_gate': nrm((DEPTH, N_EXPERTS, D_MODEL, D_EXPERT), D_MODEL ** -0.5),
        'w_up': nrm((DEPTH, N_EXPERTS, D_MODEL, D_EXPERT), D_MODEL ** -0.5),
        'w_down': nrm((DEPTH, N_EXPERTS, D_EXPERT, D_MODEL), D_EXPERT ** -0.5),
        'final_g': gain((D_MODEL,)),
    }


def reference(x, c, ctx, c_ctx, w_ada, b_ada, norm1_g, norm2_g, w_in,
              ssd_conv_w, ssd_conv_b, ssd_dt_bias, ssd_a_log, ssd_d, ssd_norm_g,
              hy_conv_w, hy_conv_b, hy_w1, hy_b1, hy_w2, hy_b2, hy_w3, hy_freq, hy_skip,
              ml_conv_w, ml_conv_b, ml_i_bias, ml_f_bias, ml_norm_g,
              hg_lb_logits, hg_norm_g, w_branch, w_out,
              w_router, w_gate, w_up, w_down, final_g):
    p_lb = jax.nn.softmax(hg_lb_logits.astype(jnp.float32), axis=0)
    lower_bounds = jnp.maximum(jnp.cumsum(p_lb, axis=0) - p_lb[0], 0.0)
    for l in range(DEPTH):
        ctx_out = l < DEPTH - 1
        mod = jax.nn.silu(c) @ w_ada[l] + b_ada[l]
        mod_c = jax.nn.silu(c_ctx) @ w_ada[l] + b_ada[l]
        sh1, sc1, g1, sh2, sc2, g2 = jnp.split(mod[:, None, :], 6, axis=-1)
        csh1, csc1, cg1, csh2, csc2, cg2 = jnp.split(mod_c, 6)
        h = modulate(rmsnorm(x, norm1_g[l]), sh1, sc1)
        hc = modulate(rmsnorm(ctx, norm1_g[l]), csh1, csc1)
        y, y_c = token_mixer(h, hc, ctx_out, w_in[l],
                             ssd_conv_w[l], ssd_conv_b[l], ssd_dt_bias[l], ssd_a_log[l], ssd_d[l], ssd_norm_g[l],
                             hy_conv_w[l], hy_conv_b[l], hy_w1[l], hy_b1[l], hy_w2[l], hy_b2[l], hy_w3[l],
                             hy_freq[l], hy_skip[l],
                             ml_conv_w[l], ml_conv_b[l], ml_i_bias[l], ml_f_bias[l], ml_norm_g[l],
                             lower_bounds[l], hg_norm_g[l], w_branch[l], w_out[l])
        x = x + g1 * y
        x = x + g2 * expert_choice_ffn(modulate(rmsnorm(x, norm2_g[l]), sh2, sc2),
                                       w_router[l], w_gate[l], w_up[l], w_down[l])
        if ctx_out:
            ctx = ctx + cg1 * y_c
            ctx = ctx + cg2 * expert_choice_ffn(modulate(rmsnorm(ctx, norm2_g[l]), csh2, csc2),
                                                w_router[l], w_gate[l], w_up[l], w_down[l])
    return rmsnorm(x, final_g)
```

```python
import math
import jax
import jax.numpy as jnp
from jax import lax
import numpy as np
from jax.experimental import pallas as pl
from jax.experimental.pallas import tpu as pltpu


D_MODEL = 1024
BATCH = 2
SEQ = 16384
DEPTH = 2

GRID_W = 64
CTX_LEN = 256
BRANCH_W = D_MODEL // 2
N_BRANCHES = 4
SHORT_CONV = 3
EPS = 1e-6
MASK_NEG = -1e30
LB_FLOOR = 1e-30

SSD_INNER = BRANCH_W
SSD_HEAD_DIM = 64
SSD_HEADS = SSD_INNER // SSD_HEAD_DIM
SSD_GROUPS = 2
SSD_STATE = 64
SSD_CHUNK = 128
SSD_CONV_CH = SSD_INNER + 2 * SSD_GROUPS * SSD_STATE
DT_MIN = 1e-3
DT_MAX = 1e-1

HY_WIDTH = BRANCH_W
HY_ORDER = 2
HY_HIDDEN = 64
HY_BANDS = 8
HY_POS_DIM = 1 + 2 * HY_BANDS
HY_DECAY_TARGET = 1e-2
HY_SHORT_PCT = 0.3
HY_LONG_PCT = 1.5

ML_HEADS = 4
ML_HEAD_DIM = BRANCH_W // ML_HEADS
ML_WIDTH = BRANCH_W
ML_CHUNK = 128

HG_HEADS = 4
HG_EXPAND = BRANCH_W // HG_HEADS
HG_HEAD_DIM = BRANCH_W // HG_HEADS
HG_WIDTH = BRANCH_W
HG_CHUNK = 64
HG_SUB = 8

N_EXPERTS = 16
EC_CAPACITY = 2
D_EXPERT = 2 * D_MODEL

SSD_COLS = SSD_INNER + SSD_CONV_CH + 2 * SSD_HEADS
HY_COLS = (HY_ORDER + 1) * HY_WIDTH
ML_COLS = 4 * ML_WIDTH + 4 * ML_HEADS
HG_COLS = 5 * HG_WIDTH
GATE_COLS = N_BRANCHES * D_MODEL
PROJ_CUTS = (SSD_COLS, SSD_COLS + HY_COLS, SSD_COLS + HY_COLS + ML_COLS,
             SSD_COLS + HY_COLS + ML_COLS + HG_COLS)
PROJ_COLS = PROJ_CUTS[-1] + GATE_COLS


def rmsnorm(x, g):
    xf = x.astype(jnp.float32)
    y = xf * lax.rsqrt(jnp.mean(xf * xf, axis=-1, keepdims=True) + EPS)
    return (y * g.astype(jnp.float32)).astype(x.dtype)


def head_rmsnorm(y, g):
    yf = y.astype(jnp.float32)
    yf = yf * lax.rsqrt(jnp.mean(yf * yf, axis=-1, keepdims=True) + EPS)
    return yf.reshape(y.shape[:2] + (-1,)) * g.astype(jnp.float32)


def modulate(h, shift, scale):
    return h * (1.0 + scale) + shift


def masked_exp(mask, logd):
    return jnp.where(mask, jnp.exp(jnp.where(mask, logd, 0.0)), 0.0)


def short_conv(u, w, bias):
    y = lax.conv_general_dilated(u, w[:, None, :].astype(u.dtype), window_strides=(1,), padding='SAME',
                                 dimension_numbers=('NWC', 'WIO', 'NWC'), feature_group_count=u.shape[-1])
    return y + bias.astype(u.dtype)


def to_col_major(t):
    b, n = t.shape[:2]
    rows = n // GRID_W
    rest = t.shape[2:]
    return t.reshape((b, rows, GRID_W) + rest).swapaxes(1, 2).reshape((b, n) + rest)


def to_row_major(t):
    b, n = t.shape[:2]
    rows = n // GRID_W
    rest = t.shape[2:]
    return t.reshape((b, GRID_W, rows) + rest).swapaxes(1, 2).reshape((b, n) + rest)


def bidirectional(scan_fn, ctx_fwd, ctx_bwd, lat_fwd, lat_bwd, init):
    flip = lambda ts: tuple(jnp.flip(t, axis=1) for t in ts)
    yc_f, sc_f = scan_fn(*ctx_fwd, init)
    yl_f, _ = scan_fn(*lat_fwd, sc_f)
    yc_b, sc_b = scan_fn(*flip(ctx_bwd), init)
    yl_b, _ = scan_fn(*flip(lat_bwd), sc_b)
    return yc_f + jnp.flip(yc_b, axis=1), yl_f + jnp.flip(yl_b, axis=1)


def ssd_scan(xs, dt, da, bm, cm, h0):
    b, n, H, P = xs.shape
    G = bm.shape[2]
    nc = n // SSD_CHUNK
    rep = H // G
    chunk = lambda t: t.reshape((b, nc, SSD_CHUNK) + t.shape[2:])
    xs, dt, da = chunk(xs), chunk(dt), chunk(da)
    bh = chunk(jnp.repeat(bm, rep, axis=2))
    ch = chunk(jnp.repeat(cm, rep, axis=2))
    acs = jnp.cumsum(da, axis=2)
    tri = jnp.tril(jnp.ones((SSD_CHUNK, SSD_CHUNK), bool))
    decay = masked_exp(tri[:, :, None], acs[:, :, :, None, :] - acs[:, :, None, :, :])
    scores = jnp.einsum('bcthn,bcshn->bctsh', ch, bh) * decay * dt[:, :, None, :, :]
    y_diag = jnp.einsum('bctsh,bcshp->bcthp', scores, xs)
    to_end = jnp.exp(acs[:, :, -1:, :] - acs) * dt
    states = jnp.einsum('bcsh,bcshn,bcshp->bchpn', to_end, bh, xs)
    chunk_decay = jnp.exp(acs[:, :, -1, :])

    def step(h, inp):
        dec, st = inp
        return dec[:, :, None, None] * h + st, h

    final, h_prev = lax.scan(step, h0, (chunk_decay.swapaxes(0, 1), states.swapaxes(0, 1)))
    h_prev = h_prev.swapaxes(0, 1)
    y_off = jnp.einsum('bcthn,bchpn->bcthp', ch, h_prev) * jnp.exp(acs)[..., None]
    return (y_diag + y_off).reshape(b, n, H, P), final


def ssd_inputs(p, conv_w, conv_b, dt_bias, a_log):
    b, n = p.shape[:2]
    z, xbc, dt = jnp.split(p, [SSD_INNER, SSD_INNER + SSD_CONV_CH], axis=-1)
    xbc = jax.nn.silu(short_conv(xbc, conv_w, conv_b))
    xs, bm, cm = jnp.split(xbc, [SSD_INNER, SSD_INNER + SSD_GROUPS * SSD_STATE], axis=-1)
    xs = xs.reshape(b, n, SSD_HEADS, SSD_HEAD_DIM)
    bm = bm.reshape(b, n, SSD_GROUPS, SSD_STATE)
    cm = cm.reshape(b, n, SSD_GROUPS, SSD_STATE)
    dt = jax.nn.softplus(dt.astype(jnp.float32).reshape(b, n, 2, SSD_HEADS) + dt_bias.astype(jnp.float32))
    da = dt * -jnp.exp(a_log.astype(jnp.float32))
    fwd = (xs, dt[:, :, 0], da[:, :, 0], bm, cm)
    bwd = (xs, dt[:, :, 1], da[:, :, 1], bm, cm)
    return z, xs, fwd, bwd


def ssd_branch(p_lat, p_ctx, conv_w, conv_b, dt_bias, a_log, d_skip, norm_g):
    lat = ssd_inputs(p_lat, conv_w, conv_b, dt_bias, a_log)
    ctx = ssd_inputs(p_ctx, conv_w, conv_b, dt_bias, a_log)
    h0 = jnp.zeros((p_lat.shape[0], SSD_HEADS, SSD_HEAD_DIM, SSD_STATE), jnp.float32)
    y_ctx, y_lat = bidirectional(ssd_scan, ctx[2], ctx[3], lat[2], lat[3], h0)

    def finish(y, inp):
        z, xs = inp[0], inp[1]
        y = (y + d_skip.astype(jnp.float32)[:, None] * xs).reshape(z.shape)
        return rmsnorm(y * jax.nn.silu(z.astype(jnp.float32)), norm_g)

    return finish(y_lat, lat), finish(y_ctx, ctx)


def hyena_filter_spectra(n, w1, b1, w2, b2, w3, freq):
    f32 = jnp.float32
    pos = jnp.arange(n, dtype=f32)
    t = pos / max(n - 1, 1)
    bands = jnp.arange(1, HY_BANDS + 1, dtype=f32)
    ang = (2.0 * math.pi / n) * pos[:, None] * bands[None, :]
    feats = jnp.concatenate([t[:, None], jnp.cos(ang), jnp.sin(ang)], axis=-1)
    hid = jnp.sin(freq[0] * (feats @ w1 + b1))
    hid = jnp.sin(freq[1] * (hid @ w2 + b2))
    h = (hid @ w3).reshape(n, HY_ORDER, 2, HY_WIDTH)
    deltas = jnp.abs(jnp.linspace(math.log(HY_DECAY_TARGET) / HY_LONG_PCT,
                                  math.log(HY_DECAY_TARGET) / HY_SHORT_PCT, HY_WIDTH, dtype=f32))
    h = h * jnp.exp(-t[:, None] * deltas[None, :])[:, None, None, :]
    two_sided = jnp.concatenate([h[:, :, 0], jnp.zeros((1, HY_ORDER, HY_WIDTH), f32),
                                 jnp.flip(h[1:, :, 1], axis=0)], axis=0)
    two_sided = two_sided / (jnp.sum(jnp.abs(two_sided), axis=0, keepdims=True) + EPS)
    return jnp.fft.rfft(two_sided, axis=0)


def fft_longconv(u, spec, skip):
    n = u.shape[1]
    uf = jnp.fft.rfft(u, n=2 * n, axis=1)
    y = jnp.fft.irfft(uf * spec, n=2 * n, axis=1)[:, :n]
    return y + u * skip


def hyena_branch(p, conv_w, conv_b, w1, b1, w2, b2, w3, freq, skip):
    f32 = jnp.float32
    u = short_conv(p, conv_w, conv_b).astype(f32)
    parts = jnp.split(u, HY_ORDER + 1, axis=-1)
    spec = hyena_filter_spectra(p.shape[1], w1.astype(f32), b1.astype(f32), w2.astype(f32),
                                b2.astype(f32), w3.astype(f32), freq.astype(f32))
    z = parts[0]
    for o in range(HY_ORDER):
        z = parts[o + 1] * fft_longconv(z, spec[:, o], skip[o].astype(f32))
    return z


def mlstm_scan(q, k, v, logi, logf, state):
    b, n, H, _ = q.shape
    nc = n // ML_CHUNK
    chunk = lambda t: t.reshape((b, nc, ML_CHUNK) + t.shape[2:])
    q, k, v, logi, logf = chunk(q), chunk(k), chunk(v), chunk(logi), chunk(logf)
    bcum = jnp.cumsum(logf, axis=2)
    btot = bcum[:, :, -1]
    a = btot[:, :, None] - bcum + logi
    m_loc = jnp.max(a, axis=2)
    w_loc = jnp.exp(a - m_loc[:, :, None])
    c_loc = jnp.einsum('bcsh,bcshv,bcshk->bchvk', w_loc, v, k)
    n_loc = jnp.einsum('bcsh,bcshk->bchk', w_loc, k)

    def step(carry, inp):
        c_st, n_st, m_st = carry
        g, cl, nl, ml = inp
        m_new = jnp.maximum(g + m_st, ml)
        a_old, a_new = jnp.exp(g + m_st - m_new), jnp.exp(ml - m_new)
        c_new = a_old[..., None, None] * c_st + a_new[..., None, None] * cl
        n_new = a_old[..., None] * n_st + a_new[..., None] * nl
        return (c_new, n_new, m_new), carry

    sw = lambda t: t.swapaxes(0, 1)
    final, (c_prev, n_prev, m_prev) = lax.scan(step, state, (sw(btot), sw(c_loc), sw(n_loc), sw(m_loc)))
    c_prev, n_prev, m_prev = sw(c_prev), sw(n_prev), sw(m_prev)
    tri = jnp.tril(jnp.ones((ML_CHUNK, ML_CHUNK), bool))[:, :, None]
    dmat = jnp.where(tri, bcum[:, :, :, None, :] - bcum[:, :, None, :, :] + logi[:, :, None, :, :],
                     MASK_NEG)
    inter = bcum + m_prev[:, :, None, :]
    m_t = jnp.maximum(inter, jnp.max(dmat, axis=3))
    w = masked_exp(tri, dmat - m_t[:, :, :, None, :]) * jnp.einsum('bcthk,bcshk->bctsh', q, k)
    w_int = jnp.exp(inter - m_t)
    num = jnp.einsum('bctsh,bcshv->bcthv', w, v) + w_int[..., None] * jnp.einsum('bcthk,bchvk->bcthv', q, c_prev)
    den = jnp.sum(w, axis=3) + w_int * jnp.einsum('bcthk,bchk->bcth', q, n_prev)
    h = num / jnp.maximum(jnp.abs(den), jnp.exp(-m_t))[..., None]
    return h.reshape(b, n, H, -1), final


def mlstm_inputs(p, conv_w, conv_b, i_bias, f_bias):
    b, n = p.shape[:2]
    qk, v, o, gates = jnp.split(p, [2 * ML_WIDTH, 3 * ML_WIDTH, 4 * ML_WIDTH], axis=-1)
    q, k = jnp.split(jax.nn.silu(short_conv(qk, conv_w, conv_b)), 2, axis=-1)
    heads = lambda t: t.reshape(b, n, ML_HEADS, ML_HEAD_DIM)
    q, k, v = heads(q), heads(k) * (ML_HEAD_DIM ** -0.5), heads(v)
    gates = gates.astype(jnp.float32).reshape(b, n, 4, ML_HEADS)
    logi = gates[:, :, :2] + i_bias.astype(jnp.float32)
    logf = jax.nn.log_sigmoid(gates[:, :, 2:] + f_bias.astype(jnp.float32))
    fwd = (q, k, v, logi[:, :, 0], logf[:, :, 0])
    bwd = (q, k, v, logi[:, :, 1], logf[:, :, 1])
    return o, fwd, bwd


def mlstm_branch(p_lat, p_ctx, conv_w, conv_b, i_bias, f_bias, norm_g):
    o_l, lat_f, lat_b = mlstm_inputs(p_lat, conv_w, conv_b, i_bias, f_bias)
    o_c, ctx_f, ctx_b = mlstm_inputs(p_ctx, conv_w, conv_b, i_bias, f_bias)
    b = p_lat.shape[0]
    s0 = (jnp.zeros((b, ML_HEADS, ML_HEAD_DIM, ML_HEAD_DIM), jnp.float32),
          jnp.zeros((b, ML_HEADS, ML_HEAD_DIM), jnp.float32),
          jnp.zeros((b, ML_HEADS), jnp.float32))
    h_c, h_l = bidirectional(mlstm_scan, ctx_f, ctx_b, lat_f, lat_b, s0)
    fin = lambda hh, o: jax.nn.sigmoid(o.astype(jnp.float32)) * head_rmsnorm(hh, norm_g)
    return fin(h_l, o_l), fin(h_c, o_c)


def hgrn2_scan(q, k, v, logf, s0):
    b, n, H, K = q.shape
    V = v.shape[-1]
    nc, ns = n // HG_CHUNK, HG_CHUNK // HG_SUB
    sub = lambda t: t.reshape((b, nc, ns, HG_SUB) + t.shape[2:])
    bq = jnp.cumsum(logf.reshape(b, nc, HG_CHUNK, H, K), axis=2)
    bs = bq.reshape(b, nc, ns, HG_SUB, H, K)
    qs, ks, vs = sub(q), sub(k), sub(v)
    end = bs[:, :, :, -1]
    start = jnp.concatenate([jnp.zeros_like(end[:, :, :1]), end[:, :, :-1]], axis=2)
    tri = jnp.tril(jnp.ones((HG_SUB, HG_SUB), bool))
    pair = masked_exp(tri[:, :, None, None], bs[:, :, :, :, None] - bs[:, :, :, None])
    att = jnp.einsum('bcjthk,bcjshk,bcjtshk->bcjtsh', qs, ks, pair)
    o = jnp.einsum('bcjtsh,bcjshv->bcjthv', att, vs)
    earlier = jnp.tril(jnp.ones((ns, ns), bool), -1)
    qx = qs * jnp.exp(bs - start[:, :, :, None])
    kx = ks[:, :, None] * masked_exp(earlier[:, :, None, None, None],
                                     start[:, :, :, None, None] - bs[:, :, None])
    att_x = jnp.einsum('bcithk,bcijshk->bcitjsh', qx, kx)
    o = (o + jnp.einsum('bcitjsh,bcjshv->bcithv', att_x, vs)).reshape(b, nc, HG_CHUNK, H, V)
    qc = q.reshape(b, nc, HG_CHUNK, H, K)
    kc = k.reshape(b, nc, HG_CHUNK, H, K)
    vc = v.reshape(b, nc, HG_CHUNK, H, V)
    btot = bq[:, :, -1]
    s_loc = jnp.einsum('bcshk,bcshv->bchkv', kc * jnp.exp(btot[:, :, None] - bq), vc)

    def step(s, inp):
        dec, sl = inp
        return dec[..., None] * s + sl, s

    final, s_prev = lax.scan(step, s0, (jnp.exp(btot).swapaxes(0, 1), s_loc.swapaxes(0, 1)))
    s_prev = s_prev.swapaxes(0, 1)
    o = o + jnp.einsum('bcthk,bchkv->bcthv', qc * jnp.exp(bq), s_prev)
    return o.reshape(b, n, H, V), final


def hgrn2_inputs(p, lb):
    b, n = p.shape[:2]
    q, i, f_fwd, f_bwd, g = jnp.split(p, 5, axis=-1)
    q = jax.nn.silu(q).reshape(b, n, HG_HEADS, HG_EXPAND)
    i = i.reshape(b, n, HG_HEADS, HG_HEAD_DIM)
    lb = lb.reshape(HG_HEADS, HG_EXPAND)
    log_lb, log_ub = jnp.log(jnp.maximum(lb, LB_FLOOR)), jnp.log1p(-lb)

    def direction(pre):
        pre = pre.astype(jnp.float32).reshape(b, n, HG_HEADS, HG_EXPAND)
        logf = jnp.logaddexp(log_lb, log_ub + jax.nn.log_sigmoid(pre))
        key = (1.0 - lb) * jax.nn.sigmoid(-pre)
        return (q, key, i, logf)

    return g, direction(f_fwd), direction(f_bwd)


def hgrn2_branch(p_lat, p_ctx, lb, norm_g):
    g_l, lat_f, lat_b = hgrn2_inputs(p_lat, lb)
    g_c, ctx_f, ctx_b = hgrn2_inputs(p_ctx, lb)
    lat_f = tuple(to_col_major(t) for t in lat_f)
    lat_b = tuple(to_col_major(t) for t in lat_b)
    s0 = jnp.zeros((p_lat.shape[0], HG_HEADS, HG_EXPAND, HG_HEAD_DIM), jnp.float32)
    o_c, o_l = bidirectional(hgrn2_scan, ctx_f, ctx_b, lat_f, lat_b, s0)
    o_l = to_row_major(o_l)
    fin = lambda o, g: head_rmsnorm(o, norm_g) * jax.nn.sigmoid(g.astype(jnp.float32))
    return fin(o_l, g_l), fin(o_c, g_c)


def merge_branches(ys, gate_pre, w_branch, w_out):
    b, n = gate_pre.shape[:2]
    gates = jax.nn.sigmoid(gate_pre.astype(jnp.float32)).reshape(b, n, N_BRANCHES, D_MODEL)
    merged = gates[:, :, 0] * (ys[0].astype(w_branch.dtype) @ w_branch[0])
    for i in range(1, N_BRANCHES):
        merged = merged + gates[:, :, i] * (ys[i].astype(w_branch.dtype) @ w_branch[i])
    return merged.astype(w_out.dtype) @ w_out


def token_mixer(h, hc, ctx_out, w_in, ssd_conv_w, ssd_conv_b, ssd_dt_bias, ssd_a_log, ssd_d, ssd_norm_g,
                hy_conv_w, hy_conv_b, hy_w1, hy_b1, hy_w2, hy_b2, hy_w3, hy_freq, hy_skip,
                ml_conv_w, ml_conv_b, ml_i_bias, ml_f_bias, ml_norm_g, hg_lb, hg_norm_g, w_branch, w_out):
    pa, pb, pc, pd, pg = jnp.split(h @ w_in, PROJ_CUTS, axis=-1)
    qa, qb, qc, qd, qg = jnp.split(hc @ w_in, PROJ_CUTS, axis=-1)
    ya, ya_c = ssd_branch(pa, qa, ssd_conv_w, ssd_conv_b, ssd_dt_bias, ssd_a_log, ssd_d, ssd_norm_g)
    yc, yc_c = mlstm_branch(pc, qc, ml_conv_w, ml_conv_b, ml_i_bias, ml_f_bias, ml_norm_g)
    yd, yd_c = hgrn2_branch(pd, qd, hg_lb, hg_norm_g)
    hyena = lambda p: hyena_branch(p, hy_conv_w, hy_conv_b, hy_w1, hy_b1, hy_w2, hy_b2, hy_w3, hy_freq, hy_skip)
    y = merge_branches((ya, hyena(pb), yc, yd), pg, w_branch, w_out)
    if not ctx_out:
        return y, None
    y_c = merge_branches((ya_c, hyena(qb), yc_c, yd_c), qg, w_branch, w_out)
    return y, y_c


def expert_choice_ffn(h, w_router, w_gate, w_up, w_down):
    b, n, d = h.shape
    cap = EC_CAPACITY * n // N_EXPERTS
    aff = jax.nn.softmax((h @ w_router).astype(jnp.float32), axis=-1)
    score, idx = lax.top_k(jnp.swapaxes(aff, 1, 2), cap)
    xg = jax.vmap(lambda hh, ii: hh[ii])(h, idx)
    hid = jax.nn.silu(jnp.einsum('becd,edf->becf', xg, w_gate)) * jnp.einsum('becd,edf->becf', xg, w_up)
    ye = jnp.einsum('becf,efd->becd', hid, w_down) * score[..., None].astype(h.dtype)
    return jax.vmap(lambda ii, yy: jnp.zeros((n, d), yy.dtype).at[ii].add(yy))(
        idx.reshape(b, -1), ye.reshape(b, -1, d))


def _final_rmsnorm_kernel(x_ref, g_ref, o_ref):
    xf = x_ref[...]
    y = xf * lax.rsqrt(jnp.mean(xf * xf, axis=-1, keepdims=True) + EPS)
    o_ref[...] = y * g_ref[...]


def _final_rmsnorm(x, g):
    b, n, d = x.shape
    rows = 1024
    x2 = x.reshape(b * n, d)
    out = pl.pallas_call(
        _final_rmsnorm_kernel,
        grid=(b * n // rows,),
        in_specs=[pl.BlockSpec((rows, d), lambda i: (i, 0)), pl.BlockSpec((1, d), lambda i: (0, 0))],
        out_specs=pl.BlockSpec((rows, d), lambda i: (i, 0)),
        out_shape=jax.ShapeDtypeStruct((b * n, d), x.dtype),
        name="final_rmsnorm",
    )(x2, g.reshape(1, d))
    return out.reshape(b, n, d)


def kernel(x, c, ctx, c_ctx, w_ada, b_ada, norm1_g, norm2_g, w_in, ssd_conv_w, ssd_conv_b, ssd_dt_bias, ssd_a_log, ssd_d, ssd_norm_g, hy_conv_w, hy_conv_b, hy_w1, hy_b1, hy_w2, hy_b2, hy_w3, hy_freq, hy_skip, ml_conv_w, ml_conv_b, ml_i_bias, ml_f_bias, ml_norm_g, hg_lb_logits, hg_norm_g, w_branch, w_out, w_router, w_gate, w_up, w_down, final_g):
    p_lb = jax.nn.softmax(hg_lb_logits.astype(jnp.float32), axis=0)
    lower_bounds = jnp.maximum(jnp.cumsum(p_lb, axis=0) - p_lb[0], 0.0)
    for l in range(DEPTH):
        ctx_out = l < DEPTH - 1
        mod = jax.nn.silu(c) @ w_ada[l] + b_ada[l]
        mod_c = jax.nn.silu(c_ctx) @ w_ada[l] + b_ada[l]
        sh1, sc1, g1, sh2, sc2, g2 = jnp.split(mod[:, None, :], 6, axis=-1)
        csh1, csc1, cg1, csh2, csc2, cg2 = jnp.split(mod_c, 6)
        h = modulate(rmsnorm(x, norm1_g[l]), sh1, sc1)
        hc = modulate(rmsnorm(ctx, norm1_g[l]), csh1, csc1)
        y, y_c = token_mixer(h, hc, ctx_out, w_in[l],
                             ssd_conv_w[l], ssd_conv_b[l], ssd_dt_bias[l], ssd_a_log[l], ssd_d[l], ssd_norm_g[l],
                             hy_conv_w[l], hy_conv_b[l], hy_w1[l], hy_b1[l], hy_w2[l], hy_b2[l], hy_w3[l],
                             hy_freq[l], hy_skip[l],
                             ml_conv_w[l], ml_conv_b[l], ml_i_bias[l], ml_f_bias[l], ml_norm_g[l],
                             lower_bounds[l], hg_norm_g[l], w_branch[l], w_out[l])
        x = x + g1 * y
        x = x + g2 * expert_choice_ffn(modulate(rmsnorm(x, norm2_g[l]), sh2, sc2),
                                       w_router[l], w_gate[l], w_up[l], w_down[l])
        if ctx_out:
            ctx = ctx + cg1 * y_c
            ctx = ctx + cg2 * expert_choice_ffn(modulate(rmsnorm(ctx, norm2_g[l]), csh2, csc2),
                                                w_router[l], w_gate[l], w_up[l], w_down[l])
    return _final_rmsnorm(x, final_g)
```

```python
import functools
import math
import jax
import jax.numpy as jnp
from jax import lax
import numpy as np
from jax.experimental import pallas as pl
from jax.experimental.pallas import tpu as pltpu


D_MODEL = 1024
DEPTH = 2
GRID_W = 64
BRANCH_W = D_MODEL // 2
N_BRANCHES = 4
EPS = 1e-6
MASK_NEG = -1e30
LB_FLOOR = 1e-30

SSD_HEAD_DIM = 64
SSD_HEADS = BRANCH_W // SSD_HEAD_DIM
SSD_GROUPS = 2
SSD_STATE = 64
SSD_CONV_CH = BRANCH_W + 2 * SSD_GROUPS * SSD_STATE
SSD_COLS = BRANCH_W + SSD_CONV_CH + 2 * SSD_HEADS

HY_WIDTH = BRANCH_W
HY_ORDER = 2
HY_BANDS = 8
HY_DECAY_TARGET = 1e-2
HY_SHORT_PCT = 0.3
HY_LONG_PCT = 1.5
HY_COLS = (HY_ORDER + 1) * HY_WIDTH

ML_HEADS = 4
ML_HEAD_DIM = BRANCH_W // ML_HEADS
ML_COLS = 4 * BRANCH_W + 4 * ML_HEADS

HG_HEADS = 4
HG_HEAD_DIM = BRANCH_W // HG_HEADS
HG_COLS = 5 * BRANCH_W
HG_LEAF = 8

N_EXPERTS = 16
EC_CAPACITY = 2

LANES = 128
SUBLANES = 8
ROW_TILE = 256
SCAN_CHUNK = 128
VMEM_LIMIT = 56 * 1024 * 1024

F32 = jnp.float32
BF16 = jnp.bfloat16
HIGHEST = lax.Precision.HIGHEST


def _dot(a, b):
    return jnp.dot(a, b, preferred_element_type=F32)


def _dot_nt(a, b):
    return lax.dot_general(a, b, (((1,), (1,)), ((), ())), preferred_element_type=F32)


def _dot_tn(a, b):
    return lax.dot_general(a, b, (((0,), (0,)), ((), ())), preferred_element_type=F32)


def _dot_exact(a, b):
    return jnp.dot(a, b, preferred_element_type=F32, precision=HIGHEST)


def _sigmoid(x):
    return 1.0 / (1.0 + jnp.exp(-x))


def _silu(x):
    return x * _sigmoid(x)


def _softplus(x):
    return jnp.maximum(x, 0.0) + jnp.log1p(jnp.exp(-jnp.abs(x)))


def _log_sigmoid(x):
    return jnp.minimum(x, 0.0) - jnp.log1p(jnp.exp(-jnp.abs(x)))


def _rms(x):
    return x * lax.rsqrt(jnp.mean(x * x, axis=-1, keepdims=True) + EPS)


def _proj_kernel(x_ref, sh_ref, sc_ref, g_ref, *refs, n_out):
    h = (_rms(x_ref[...]) * g_ref[...] * (1.0 + sc_ref[...]) + sh_ref[...]).astype(BF16)
    for w_ref, o_ref in zip(refs[:n_out], refs[n_out:]):
        o_ref[...] = _dot(h, w_ref[...])


def _proj(xc, shift, scale, gain, weights, n_ctx_tiles):
    B, T, D = xc.shape
    mod_spec = pl.BlockSpec((None, None, 1, D), lambda b, i: (b, jnp.where(i < n_ctx_tiles, 1, 0), 0, 0))
    return pl.pallas_call(
        functools.partial(_proj_kernel, n_out=len(weights)),
        grid=(B, T // ROW_TILE),
        in_specs=[pl.BlockSpec((None, ROW_TILE, D), lambda b, i: (b, i, 0)), mod_spec, mod_spec,
                  pl.BlockSpec((1, D), lambda b, i: (0, 0))]
                 + [pl.BlockSpec(w.shape, lambda b, i: (0, 0)) for w in weights],
        out_specs=[pl.BlockSpec((None, ROW_TILE, w.shape[1]), lambda b, i: (b, i, 0)) for w in weights],
        out_shape=[jax.ShapeDtypeStruct((B, T, w.shape[1]), F32) for w in weights],
        compiler_params=pltpu.CompilerParams(dimension_semantics=("parallel", "parallel"),
                                             vmem_limit_bytes=VMEM_LIMIT),
        name="norm_mod_proj",
    )(xc, shift, scale, gain.reshape(1, D), *weights)


def _scan_chunk(step, reverse, n_ctx_chunks, n_chunks):
    if not reverse:
        return step
    return jnp.where(step < n_ctx_chunks, n_ctx_chunks - 1 - step, n_chunks - 1 - (step - n_ctx_chunks))


def _scan_specs(reverse, n_ctx_chunks, n_chunks, chunk):
    cidx = lambda s: _scan_chunk(s, reverse, n_ctx_chunks, n_chunks)
    per_tile = chunk // SUBLANES
    n_tiles = n_chunks * per_tile

    def main(width):
        return pl.BlockSpec((None, chunk, width), lambda b, s: (b, cidx(s), 0))

    def prev(width):
        return pl.BlockSpec((None, SUBLANES, width), lambda b, s: (b, jnp.maximum(cidx(s) * per_tile - 1, 0), 0))

    def nxt(width):
        return pl.BlockSpec((None, SUBLANES, width),
                            lambda b, s: (b, jnp.minimum((cidx(s) + 1) * per_tile, n_tiles - 1), 0))

    def const(shape):
        return pl.BlockSpec(shape, lambda b, s: (0,) * len(shape))

    return main, prev, nxt, const


def _segment_edges(chunk_idx, n_ctx_chunks, n_chunks):
    first = jnp.logical_or(chunk_idx == 0, chunk_idx == n_ctx_chunks)
    last = jnp.logical_or(chunk_idx == n_ctx_chunks - 1, chunk_idx == n_chunks - 1)
    return first, last


def _conv3(u, u_prev_tile, u_next_tile, w, bias, first, last):
    q = u.shape[0]
    rows = lax.broadcasted_iota(jnp.int32, u.shape, 0)
    before = jnp.where(first, 0.0, u_prev_tile[SUBLANES - 1:SUBLANES, :])
    after = jnp.where(last, 0.0, u_next_tile[0:1, :])
    u_m1 = jnp.where(rows == 0, before, pltpu.roll(u, 1, 0))
    u_p1 = jnp.where(rows == q - 1, after, pltpu.roll(u, q - 1, 0))
    return w[0:1, :] * u_m1 + w[1:2, :] * u + w[2:3, :] * u_p1 + bias


def _order_masks(q, reverse):
    r = lax.broadcasted_iota(jnp.int32, (q, q), 0)
    c = lax.broadcasted_iota(jnp.int32, (q, q), 1)
    return (r <= c) if reverse else (r >= c)


def _ssd_kernel(z_ref, x_ref, xp_ref, xn_ref, bc_ref, bcp_ref, bcn_ref, dt_ref, yf_ref,
                cwx_ref, cbx_ref, cwbc_ref, cbbc_ref, dtb_ref, nega_ref, dskip_ref, ng_ref,
                out_ref, st_ref, *, reverse, n_ctx_chunks, n_chunks):
    q = x_ref.shape[0]
    step = pl.program_id(1)
    cidx = _scan_chunk(step, reverse, n_ctx_chunks, n_chunks)
    first, last = _segment_edges(cidx, n_ctx_chunks, n_chunks)

    @pl.when(step == 0)
    def _():
        st_ref[...] = jnp.zeros_like(st_ref)

    xs = _silu(_conv3(x_ref[...], xp_ref[...], xn_ref[...], cwx_ref[...], cbx_ref[...], first, last))
    bc = _silu(_conv3(bc_ref[...], bcp_ref[...], bcn_ref[...], cwbc_ref[...], cbbc_ref[...], first, last))
    dt_all = _softplus(dt_ref[...] + dtb_ref[...])
    da_all = dt_all * nega_ref[...]
    mask = _order_masks(q, reverse)
    acs = _dot_exact(mask.astype(F32), da_all)
    acs_t = acs.T
    dt_t = dt_all.T
    edge = 0 if reverse else q - 1
    gn = SSD_STATE
    ys = []
    for g in range(SSD_GROUPS):
        bg = bc[:, g * gn:(g + 1) * gn].astype(BF16)
        cg = bc[:, (SSD_GROUPS + g) * gn:(SSD_GROUPS + g + 1) * gn].astype(BF16)
        cb = _dot_nt(cg, bg)
        for hh in range(SSD_HEADS // SSD_GROUPS):
            h = g * (SSD_HEADS // SSD_GROUPS) + hh
            col = (SSD_HEADS if reverse else 0) + h
            a_col = acs[:, col:col + 1]
            a_row = acs_t[col:col + 1, :]
            a_end = a_row[:, edge:edge + 1]
            decay = jnp.where(mask, jnp.exp(jnp.minimum(a_col - a_row, 0.0)), 0.0)
            scores = (cb * decay * dt_t[col:col + 1, :]).astype(BF16)
            xh = xs[:, h * SSD_HEAD_DIM:(h + 1) * SSD_HEAD_DIM]
            st = st_ref[h]
            y = _dot(scores, xh.astype(BF16)) + _dot(cg, st.astype(BF16)) * jnp.exp(a_col)
            to_end = jnp.exp(a_end - a_col) * dt_all[:, col:col + 1]
            st_ref[h] = jnp.exp(a_end) * st + _dot_tn(bg, (xh * to_end).astype(BF16))
            ys.append(y)
    y = jnp.concatenate(ys, axis=1)
    if not reverse:
        out_ref[...] = y
    else:
        y = (y + yf_ref[...] + dskip_ref[...] * xs) * _silu(z_ref[...])
        out_ref[...] = (_rms(y) * ng_ref[...]).astype(out_ref.dtype)


def _ssd_pass(z, x, bc, dt, y_fwd, consts, reverse, n_ctx_chunks):
    B, T, _ = x.shape
    n_chunks = T // SCAN_CHUNK
    main, prev, nxt, const = _scan_specs(reverse, n_ctx_chunks, n_chunks, SCAN_CHUNK)
    w = BRANCH_W
    wbc = 2 * SSD_GROUPS * SSD_STATE
    in_specs = [main(w), main(w), prev(w), nxt(w), main(wbc), prev(wbc), nxt(wbc), main(LANES), main(w),
                const((3, w)), const((1, w)), const((3, wbc)), const((1, wbc)), const((1, LANES)),
                const((1, LANES)), const((1, w)), const((1, w))]
    return pl.pallas_call(
        functools.partial(_ssd_kernel, reverse=reverse, n_ctx_chunks=n_ctx_chunks, n_chunks=n_chunks),
        grid=(B, n_chunks),
        in_specs=in_specs,
        out_specs=main(w),
        out_shape=jax.ShapeDtypeStruct((B, T, w), BF16 if reverse else F32),
        scratch_shapes=[pltpu.VMEM((SSD_HEADS, SSD_STATE, SSD_HEAD_DIM), F32)],
        compiler_params=pltpu.CompilerParams(dimension_semantics=("parallel", "arbitrary"),
                                             vmem_limit_bytes=VMEM_LIMIT),
        name="ssd_bwd_finish" if reverse else "ssd_fwd",
    )(z, x, x, x, bc, bc, bc, dt, y_fwd, *consts)


def _ssd_branch(z, x, bc, dt, conv_w, conv_b, dt_bias, a_log, d_skip, norm_g, n_ctx_chunks):
    w = BRANCH_W
    pad = LANES - 2 * SSD_HEADS
    dtb = jnp.pad(dt_bias.astype(F32).reshape(1, -1), ((0, 0), (0, pad)))
    nega = jnp.pad(-jnp.exp(a_log.astype(F32)).reshape(1, -1), ((0, 0), (0, pad)))
    consts = (conv_w[:, :w], conv_b[:w].reshape(1, w), conv_w[:, w:], conv_b[w:].reshape(1, -1), dtb, nega,
              jnp.repeat(d_skip.astype(F32), SSD_HEAD_DIM).reshape(1, w), norm_g.reshape(1, w))
    y_f = _ssd_pass(z, x, bc, dt, x, consts, False, n_ctx_chunks)
    return _ssd_pass(z, x, bc, dt, y_f, consts, True, n_ctx_chunks)


def _mlstm_kernel(qk_ref, qkp_ref, qkn_ref, v_ref, o_ref, gt_ref, hf_ref, cw_ref, cb_ref, gb_ref, ng_ref,
                  out_ref, c_ref, m_ref, *, reverse, n_ctx_chunks, n_chunks):
    q = qk_ref.shape[0]
    dh = ML_HEAD_DIM
    step = pl.program_id(1)
    cidx = _scan_chunk(step, reverse, n_ctx_chunks, n_chunks)
    first, last = _segment_edges(cidx, n_ctx_chunks, n_chunks)

    @pl.when(step == 0)
    def _():
        c_ref[...] = jnp.zeros_like(c_ref)
        m_ref[...] = jnp.zeros_like(m_ref)

    qk = _silu(_conv3(qk_ref[...], qkp_ref[...], qkn_ref[...], cw_ref[...], cb_ref[...], first, last))
    gates = gt_ref[...] + gb_ref[...]
    logf_all = _log_sigmoid(gates)
    mask = _order_masks(q, reverse)
    bcum = _dot_exact(mask.astype(F32), logf_all)
    bcum_t = bcum.T
    gates_t = gates.T
    edge = 0 if reverse else q - 1
    lane = lax.broadcasted_iota(jnp.int32, (q, dh), 1)
    ones_col = jnp.where(lane == 0, 1.0, 0.0).astype(BF16)
    hs = []
    for h in range(ML_HEADS):
        li = (ML_HEADS if reverse else 0) + h
        lf = 2 * ML_HEADS + li
        qh = qk[:, h * dh:(h + 1) * dh].astype(BF16)
        kh = qk[:, BRANCH_W + h * dh:BRANCH_W + (h + 1) * dh] * (dh ** -0.5)
        v_aug = jnp.concatenate([v_ref[:, h * dh:(h + 1) * dh].astype(BF16), ones_col], axis=1)
        b_col = bcum[:, lf:lf + 1]
        b_row = bcum_t[lf:lf + 1, :]
        i_col = gates[:, li:li + 1]
        i_row = gates_t[li:li + 1, :]
        b_tot = b_row[:, edge:edge + 1]
        m_prev = m_ref[h:h + 1, 0:1]
        c_prev = c_ref[h]
        dmat = jnp.where(mask, b_col - b_row + i_row, MASK_NEG)
        inter = b_col + m_prev
        m_t = jnp.maximum(inter, jnp.max(dmat, axis=1, keepdims=True))
        wgt = jnp.where(mask, jnp.exp(jnp.minimum(dmat - m_t, 0.0)), 0.0) * _dot_nt(qh, kh.astype(BF16))
        w_int = jnp.exp(inter - m_t)
        cross = _dot(qh, c_prev.astype(BF16))
        num = _dot(wgt.astype(BF16), v_aug[:, :dh]) + w_int * cross[:, :dh]
        den = jnp.sum(wgt, axis=1, keepdims=True) + w_int * cross[:, dh:dh + 1]
        hs.append(num / jnp.maximum(jnp.abs(den), jnp.exp(-m_t)))
        a = b_tot - b_col + i_col
        m_new = jnp.maximum(b_tot + m_prev, jnp.max(a, axis=0, keepdims=True))
        kw = (kh * jnp.exp(a - m_new)).astype(BF16)
        c_ref[h] = jnp.exp(b_tot + m_prev - m_new) * c_prev + _dot_tn(kw, v_aug)
        m_ref[h:h + 1, :] = jnp.broadcast_to(m_new, (1, LANES))
    if not reverse:
        out_ref[...] = jnp.concatenate(hs, axis=1)
    else:
        hf = hf_ref[...]
        outs = [_rms(hs[h] + hf[:, h * dh:(h + 1) * dh]) for h in range(ML_HEADS)]
        out_ref[...] = (_sigmoid(o_ref[...]) * jnp.concatenate(outs, axis=1) * ng_ref[...]).astype(out_ref.dtype)


def _mlstm_pass(qk, v, o, gates, h_fwd, consts, reverse, n_ctx_chunks):
    B, T, _ = v.shape
    n_chunks = T // SCAN_CHUNK
    main, prev, nxt, const = _scan_specs(reverse, n_ctx_chunks, n_chunks, SCAN_CHUNK)
    w = BRANCH_W
    in_specs = [main(2 * w), prev(2 * w), nxt(2 * w), main(w), main(w), main(LANES), main(w),
                const((3, 2 * w)), const((1, 2 * w)), const((1, LANES)), const((1, w))]
    return pl.pallas_call(
        functools.partial(_mlstm_kernel, reverse=reverse, n_ctx_chunks=n_ctx_chunks, n_chunks=n_chunks),
        grid=(B, n_chunks),
        in_specs=in_specs,
        out_specs=main(w),
        out_shape=jax.ShapeDtypeStruct((B, T, w), BF16 if reverse else F32),
        scratch_shapes=[pltpu.VMEM((ML_HEADS, ML_HEAD_DIM, 2 * ML_HEAD_DIM), F32),
                        pltpu.VMEM((SUBLANES, LANES), F32)],
        compiler_params=pltpu.CompilerParams(dimension_semantics=("parallel", "arbitrary"),
                                             vmem_limit_bytes=VMEM_LIMIT),
        name="mlstm_bwd_finish" if reverse else "mlstm_fwd",
    )(qk, qk, qk, v, o, gates, h_fwd, *consts)


def _mlstm_branch(qk, v, o, gates, conv_w, conv_b, i_bias, f_bias, norm_g, n_ctx_chunks):
    w = BRANCH_W
    gb = jnp.concatenate([i_bias.astype(F32).reshape(-1), f_bias.astype(F32).reshape(-1)])
    gb = jnp.pad(gb.reshape(1, -1), ((0, 0), (0, LANES - 4 * ML_HEADS)))
    consts = (conv_w, conv_b.reshape(1, 2 * w), gb, norm_g.reshape(1, w))
    h_f = _mlstm_pass(qk, v, o, gates, v, consts, False, n_ctx_chunks)
    return _mlstm_pass(qk, v, o, gates, h_f, consts, True, n_ctx_chunks)


def _hgrn_level_matrix(q, block, reverse):
    t = lax.broadcasted_iota(jnp.int32, (q, q), 0)
    r = lax.broadcasted_iota(jnp.int32, (q, q), 1)
    blk = t // block
    start = blk * block
    end = start + block - 1
    odd = (blk % 2) == 1
    if not reverse:
        lo = jnp.where(odd, start, t + 1)
        hi = jnp.where(odd, t, end)
    else:
        lo = jnp.where(odd, start, t)
        hi = jnp.where(odd, t - 1, end)
    return jnp.where((r >= lo) & (r <= hi), 1.0, 0.0)


def _hgrn_kernel(q_ref, v_ref, f_ref, g_ref, of_ref, lb_ref, ng_ref, out_ref,
                 st_ref, b_ref, qs_ref, ks_ref, ol_ref, *, reverse):
    q = q_ref.shape[0]
    dh = HG_HEAD_DIM
    step = pl.program_id(1)

    @pl.when(step == 0)
    def _():
        st_ref[...] = jnp.zeros_like(st_ref)

    lb = lb_ref[...]
    pre = f_ref[...]
    log_lb = jnp.log(jnp.maximum(lb, LB_FLOOR))
    log_ub = jnp.log1p(-lb)
    lo = log_ub + _log_sigmoid(pre)
    logf = jnp.maximum(log_lb, lo) + jnp.log1p(jnp.exp(-jnp.abs(log_lb - lo)))
    key = (1.0 - lb) * _sigmoid(-pre)
    qv = _silu(q_ref[...])
    mask = _order_masks(q, reverse)
    maskf = mask.astype(F32)
    bq = _dot_exact(maskf, logf)
    b_ref[...] = bq
    qs_ref[...] = qv
    ks_ref[...] = key

    def leaf_body(l, carry):
        rows = pl.ds(pl.multiple_of(l * HG_LEAF, HG_LEAF), HG_LEAF)
        bl = b_ref[rows, :]
        ql = qs_ref[rows, :]
        kl = ks_ref[rows, :]
        vl = v_ref[rows, :]
        tt = lax.broadcasted_iota(jnp.int32, (HG_LEAF, dh), 0)
        acc = [jnp.zeros((HG_LEAF, dh), F32) for _ in range(HG_HEADS)]
        for s in range(HG_LEAF):
            ok = (tt <= s) if reverse else (tt >= s)
            for h in range(HG_HEADS):
                sl = slice(h * dh, (h + 1) * dh)
                diff = bl[:, sl] - bl[s:s + 1, sl]
                e = jnp.where(ok, jnp.exp(jnp.minimum(diff, 0.0)), 0.0)
                att = jnp.sum(ql[:, sl] * e * kl[s:s + 1, sl], axis=1, keepdims=True)
                acc[h] = acc[h] + att * vl[s:s + 1, sl]
        ol_ref[rows, :] = jnp.concatenate(acc, axis=1)
        return carry

    lax.fori_loop(0, q // HG_LEAF, leaf_body, 0)

    t_idx = lax.broadcasted_iota(jnp.int32, (q, q), 0)
    s_idx = lax.broadcasted_iota(jnp.int32, (q, q), 1)
    row_t = lax.broadcasted_iota(jnp.int32, (q, dh), 0)
    levels = []
    block = HG_LEAF
    while block < q:
        fac = jnp.exp(_dot_exact(_hgrn_level_matrix(q, block, reverse), logf))
        tb, sb = t_idx // block, s_idx // block
        if not reverse:
            pair = ((tb % 2) == 1) & (sb == tb - 1)
            is_query = ((row_t // block) % 2) == 1
        else:
            pair = ((tb % 2) == 0) & (sb == tb + 1)
            is_query = ((row_t // block) % 2) == 0
        levels.append((fac, pair, is_query))
        block *= 2
    edge = 0 if reverse else q - 1
    b_end = bq[edge:edge + 1, :]
    q_in = qv * jnp.exp(bq)
    k_out = key * jnp.exp(b_end - bq)
    outs = []
    for h in range(HG_HEADS):
        sl = slice(h * dh, (h + 1) * dh)
        att = jnp.zeros((q, q), F32)
        for fac, pair, is_query in levels:
            qt = jnp.where(is_query, qv[:, sl] * fac[:, sl], 0.0).astype(BF16)
            kt = jnp.where(is_query, 0.0, key[:, sl] * fac[:, sl]).astype(BF16)
            att = att + jnp.where(pair, _dot_nt(qt, kt), 0.0)
        vh = v_ref[:, sl].astype(BF16)
        st = st_ref[h]
        o = ol_ref[:, sl] + _dot(att.astype(BF16), vh) + _dot_nt(q_in[:, sl].astype(BF16), st.astype(BF16))
        st_ref[h] = jnp.exp(b_end[:, sl]) * st + _dot_tn(vh, k_out[:, sl].astype(BF16))
        outs.append(o)
    if not reverse:
        out_ref[...] = jnp.concatenate(outs, axis=1)
    else:
        of = of_ref[...]
        fin = [_rms(outs[h] + of[:, h * dh:(h + 1) * dh]) for h in range(HG_HEADS)]
        out_ref[...] = (jnp.concatenate(fin, axis=1) * ng_ref[...] * _sigmoid(g_ref[...])).astype(out_ref.dtype)


def _hgrn_pass(qr, v, f, g, o_fwd, lb, norm_g, reverse, n_ctx_chunks):
    B, T, w = v.shape
    n_chunks = T // SCAN_CHUNK
    main, _, _, const = _scan_specs(reverse, n_ctx_chunks, n_chunks, SCAN_CHUNK)
    return pl.pallas_call(
        functools.partial(_hgrn_kernel, reverse=reverse),
        grid=(B, n_chunks),
        in_specs=[main(w), main(w), main(w), main(w), main(w), const((1, w)), const((1, w))],
        out_specs=main(w),
        out_shape=jax.ShapeDtypeStruct((B, T, w), BF16 if reverse else F32),
        scratch_shapes=[pltpu.VMEM((HG_HEADS, HG_HEAD_DIM, HG_HEAD_DIM), F32)]
                       + [pltpu.VMEM((SCAN_CHUNK, w), F32) for _ in range(4)],
        compiler_params=pltpu.CompilerParams(dimension_semantics=("parallel", "arbitrary"),
                                             vmem_limit_bytes=VMEM_LIMIT),
        name="hgrn_bwd_finish" if reverse else "hgrn_fwd",
    )(qr, v, f, g, o_fwd, lb.reshape(1, w), norm_g.reshape(1, w))


def _hgrn_branch(qr, v, f_fwd, f_bwd, g, lb, norm_g, n_ctx_chunks):
    o_f = _hgrn_pass(qr, v, f_fwd, g, v, lb, norm_g, False, n_ctx_chunks)
    return _hgrn_pass(qr, v, f_bwd, g, o_f, lb, norm_g, True, n_ctx_chunks)


def _short_conv(u, w, bias):
    y = lax.conv_general_dilated(u, w[:, None, :].astype(u.dtype), window_strides=(1,), padding='SAME',
                                 dimension_numbers=('NWC', 'WIO', 'NWC'), feature_group_count=u.shape[-1])
    return y + bias.astype(u.dtype)


def _hyena_filter_spectra(n, w1, b1, w2, b2, w3, freq):
    pos = jnp.arange(n, dtype=F32)
    t = pos / max(n - 1, 1)
    bands = jnp.arange(1, HY_BANDS + 1, dtype=F32)
    ang = (2.0 * math.pi / n) * pos[:, None] * bands[None, :]
    feats = jnp.concatenate([t[:, None], jnp.cos(ang), jnp.sin(ang)], axis=-1)
    hid = jnp.sin(freq[0] * (feats @ w1 + b1))
    hid = jnp.sin(freq[1] * (hid @ w2 + b2))
    h = (hid @ w3).reshape(n, HY_ORDER, 2, HY_WIDTH)
    deltas = jnp.abs(jnp.linspace(math.log(HY_DECAY_TARGET) / HY_LONG_PCT,
                                  math.log(HY_DECAY_TARGET) / HY_SHORT_PCT, HY_WIDTH, dtype=F32))
    h = h * jnp.exp(-t[:, None] * deltas[None, :])[:, None, None, :]
    two_sided = jnp.concatenate([h[:, :, 0], jnp.zeros((1, HY_ORDER, HY_WIDTH), F32),
                                 jnp.flip(h[1:, :, 1], axis=0)], axis=0)
    two_sided = two_sided / (jnp.sum(jnp.abs(two_sided), axis=0, keepdims=True) + EPS)
    return jnp.fft.rfft(two_sided, axis=0)


def _fft_longconv(u, spec, skip):
    n = u.shape[1]
    uf = jnp.fft.rfft(u, n=2 * n, axis=1)
    y = jnp.fft.irfft(uf * spec, n=2 * n, axis=1)[:, :n]
    return y + u * skip


def _hyena_branch(p, conv_w, conv_b, w1, b1, w2, b2, w3, freq, skip):
    u = _short_conv(p, conv_w, conv_b).astype(F32)
    parts = jnp.split(u, HY_ORDER + 1, axis=-1)
    spec = _hyena_filter_spectra(p.shape[1], w1.astype(F32), b1.astype(F32), w2.astype(F32),
                                 b2.astype(F32), w3.astype(F32), freq.astype(F32))
    z = parts[0]
    for o in range(HY_ORDER):
        z = parts[o + 1] * _fft_longconv(z, spec[:, o], skip[o].astype(F32))
    return z


def _merge_kernel(ya_ref, yb_ref, yc_ref, yd_ref, gate_ref, x_ref, g1_ref, sh_ref, sc_ref, ng_ref,
                  wb_ref, wo_ref, wr_ref, x1_ref, h2_ref, aff_ref):
    d = x_ref.shape[1]
    merged = None
    for i, y_ref in enumerate((ya_ref, yb_ref, yc_ref, yd_ref)):
        term = _sigmoid(gate_ref[:, i * d:(i + 1) * d]) * _dot(y_ref[...], wb_ref[i])
        merged = term if merged is None else merged + term
    x1 = x_ref[...] + g1_ref[...] * _dot(merged.astype(BF16), wo_ref[...])
    x1_ref[...] = x1
    h2 = _rms(x1) * ng_ref[...] * (1.0 + sc_ref[...]) + sh_ref[...]
    h2_ref[...] = h2.astype(h2_ref.dtype)
    logits = _dot_exact(h2, wr_ref[...])
    lane = lax.broadcasted_iota(jnp.int32, logits.shape, 1)
    logits = jnp.where(lane < N_EXPERTS, logits, MASK_NEG)
    e = jnp.exp(logits - jnp.max(logits, axis=1, keepdims=True))
    aff_ref[...] = e / jnp.sum(e, axis=1, keepdims=True)


def _merge(ys, gates, xc, g1, shift, scale, gain, wb, wo, wr, n_ctx_tiles):
    B, T, D = xc.shape
    w = BRANCH_W
    tile = lambda width: pl.BlockSpec((None, ROW_TILE, width), lambda b, i: (b, i, 0))
    mod_spec = pl.BlockSpec((None, None, 1, D), lambda b, i: (b, jnp.where(i < n_ctx_tiles, 1, 0), 0, 0))
    const = lambda shape: pl.BlockSpec(shape, lambda b, i: (0,) * len(shape))
    return pl.pallas_call(
        _merge_kernel,
        grid=(B, T // ROW_TILE),
        in_specs=[tile(w), tile(w), tile(w), tile(w), tile(N_BRANCHES * D), tile(D), mod_spec, mod_spec, mod_spec,
                  const((1, D)), const(wb.shape), const(wo.shape), const(wr.shape)],
        out_specs=[tile(D), tile(D), tile(LANES)],
        out_shape=[jax.ShapeDtypeStruct((B, T, D), F32), jax.ShapeDtypeStruct((B, T, D), BF16),
                   jax.ShapeDtypeStruct((B, T, LANES), F32)],
        compiler_params=pltpu.CompilerParams(dimension_semantics=("parallel", "parallel"),
                                             vmem_limit_bytes=VMEM_LIMIT),
        name="merge_out_router",
    )(*ys, gates, xc, g1, shift, scale, gain.reshape(1, D), wb, wo, wr)


def _expert_kernel(x_ref, s_ref, wg_ref, wu_ref, wd_ref, o_ref):
    x = x_ref[...]
    hid = _silu(_dot(x, wg_ref[...])) * _dot(x, wu_ref[...])
    o_ref[...] = _dot(hid.astype(BF16), wd_ref[...]) * s_ref[...]


def _experts(xg, score, wg, wu, wd):
    E, R, D = xg.shape
    F = wg.shape[2]
    tm = min(R, 512)
    return pl.pallas_call(
        _expert_kernel,
        grid=(E, R // tm),
        in_specs=[pl.BlockSpec((None, tm, D), lambda e, i: (e, i, 0)),
                  pl.BlockSpec((None, tm, 1), lambda e, i: (e, i, 0)),
                  pl.BlockSpec((None, D, F), lambda e, i: (e, 0, 0)),
                  pl.BlockSpec((None, D, F), lambda e, i: (e, 0, 0)),
                  pl.BlockSpec((None, F, D), lambda e, i: (e, 0, 0))],
        out_specs=pl.BlockSpec((None, tm, D), lambda e, i: (e, i, 0)),
        out_shape=jax.ShapeDtypeStruct((E, R, D), F32),
        compiler_params=pltpu.CompilerParams(dimension_semantics=("parallel", "arbitrary"),
                                             vmem_limit_bytes=VMEM_LIMIT),
        name="expert_swiglu",
    )(xg, score, wg, wu, wd)


def _expert_choice(h2, aff, seg_start, seg_len, wg, wu, wd):
    B, T, D = h2.shape
    cap = EC_CAPACITY * seg_len // N_EXPERTS
    a = aff[:, seg_start:seg_start + seg_len, :N_EXPERTS]
    score, idx = lax.top_k(jnp.swapaxes(a, 1, 2), cap)
    rows = idx + seg_start + (jnp.arange(B, dtype=idx.dtype) * T)[:, None, None]
    rows = jnp.swapaxes(rows, 0, 1).reshape(N_EXPERTS, B * cap)
    score = jnp.swapaxes(score, 0, 1).reshape(N_EXPERTS, B * cap, 1)
    xg = jnp.take(h2.reshape(B * T, D), rows, axis=0)
    ye = _experts(xg, score, wg, wu, wd)
    return rows.reshape(-1), ye.reshape(-1, D)


def _residual_kernel(x_ref, m_ref, g2_ref, gain_ref, o_ref, *, final):
    x = x_ref[...] + g2_ref[...] * m_ref[...]
    o_ref[...] = _rms(x) * gain_ref[...] if final else x


def _residual(x1, moe, g2, gain, final, n_ctx_tiles):
    B, T, D = x1.shape
    tile = pl.BlockSpec((None, ROW_TILE, D), lambda b, i: (b, i, 0))
    mod_spec = pl.BlockSpec((None, None, 1, D), lambda b, i: (b, jnp.where(i < n_ctx_tiles, 1, 0), 0, 0))
    return pl.pallas_call(
        functools.partial(_residual_kernel, final=final),
        grid=(B, T // ROW_TILE),
        in_specs=[tile, tile, mod_spec, pl.BlockSpec((1, D), lambda b, i: (0, 0))],
        out_specs=tile,
        out_shape=jax.ShapeDtypeStruct((B, T, D), F32),
        compiler_params=pltpu.CompilerParams(dimension_semantics=("parallel", "parallel")),
        name="moe_residual_final_norm" if final else "moe_residual",
    )(x1, moe, g2, gain.reshape(1, D))


def _to_col_major(t):
    b, n = t.shape[:2]
    rows = n // GRID_W
    return t.reshape((b, rows, GRID_W) + t.shape[2:]).swapaxes(1, 2).reshape(t.shape)


def _to_row_major(t):
    b, n = t.shape[:2]
    rows = n // GRID_W
    return t.reshape((b, GRID_W, rows) + t.shape[2:]).swapaxes(1, 2).reshape(t.shape)


def _pad_cols(w, width):
    return jnp.pad(w, ((0, 0), (0, width - w.shape[1])))


def kernel(x, c, ctx, c_ctx, w_ada, b_ada, norm1_g, norm2_g, w_in, ssd_conv_w, ssd_conv_b, ssd_dt_bias, ssd_a_log, ssd_d, ssd_norm_g, hy_conv_w, hy_conv_b, hy_w1, hy_b1, hy_w2, hy_b2, hy_w3, hy_freq, hy_skip, ml_conv_w, ml_conv_b, ml_i_bias, ml_f_bias, ml_norm_g, hg_lb_logits, hg_norm_g, w_branch, w_out, w_router, w_gate, w_up, w_down, final_g):
    B, n, D = x.shape
    n_ctx = ctx.shape[1]
    T = n_ctx + n
    assert n_ctx % ROW_TILE == 0 and n % ROW_TILE == 0 and n % GRID_W == 0
    n_ctx_tiles = n_ctx // ROW_TILE
    n_ctx_chunks = n_ctx // SCAN_CHUNK
    w = BRANCH_W

    p_lb = jax.nn.softmax(hg_lb_logits.astype(F32), axis=0)
    lower_bounds = jnp.maximum(jnp.cumsum(p_lb, axis=0) - p_lb[0], 0.0)
    xc = jnp.concatenate([ctx, x], axis=1)
    depth = w_in.shape[0]
    for l in range(depth):
        last_layer = l == depth - 1
        mod = jax.nn.silu(c) @ w_ada[l] + b_ada[l]
        mod_c = jax.nn.silu(c_ctx) @ w_ada[l] + b_ada[l]
        mods = jnp.stack([mod, jnp.broadcast_to(mod_c, mod.shape)], axis=1).reshape(B, 2, 6, 1, D)
        sh1, sc1, g1, sh2, sc2, g2 = (mods[:, :, i] for i in range(6))

        wl = w_in[l].astype(BF16)
        o_ssd, o_hy, o_ml, o_hg, o_gate = 0, SSD_COLS, SSD_COLS + HY_COLS, SSD_COLS + HY_COLS + ML_COLS, \
            SSD_COLS + HY_COLS + ML_COLS + HG_COLS
        cols = lambda a, b_: wl[:, a:b_]
        z, xs, bc, dt, qk, v, o, gts = _proj(xc, sh1, sc1, norm1_g[l], [
            cols(o_ssd, o_ssd + w), cols(o_ssd + w, o_ssd + 2 * w), cols(o_ssd + 2 * w, o_ssd + w + SSD_CONV_CH),
            _pad_cols(cols(o_ssd + w + SSD_CONV_CH, o_hy), LANES),
            cols(o_ml, o_ml + 2 * w), cols(o_ml + 2 * w, o_ml + 3 * w), cols(o_ml + 3 * w, o_ml + 4 * w),
            _pad_cols(cols(o_ml + 4 * w, o_hg), LANES)], n_ctx_tiles)
        p_hy, gate_pre = _proj(xc, sh1, sc1, norm1_g[l], [cols(o_hy, o_ml), cols(o_gate, o_gate + N_BRANCHES * D)],
                               n_ctx_tiles)
        xc_cm = jnp.concatenate([xc[:, :n_ctx], _to_col_major(xc[:, n_ctx:])], axis=1)
        hq, hi, hff, hfb, hgt = _proj(xc_cm, sh1, sc1, norm1_g[l],
                                      [cols(o_hg + i * w, o_hg + (i + 1) * w) for i in range(5)], n_ctx_tiles)

        ya = _ssd_branch(z, xs, bc, dt, ssd_conv_w[l], ssd_conv_b[l], ssd_dt_bias[l], ssd_a_log[l], ssd_d[l],
                         ssd_norm_g[l], n_ctx_chunks)
        yc = _mlstm_branch(qk, v, o, gts, ml_conv_w[l], ml_conv_b[l], ml_i_bias[l], ml_f_bias[l], ml_norm_g[l],
                           n_ctx_chunks)
        yd_cm = _hgrn_branch(hq, hi, hff, hfb, hgt, lower_bounds[l], hg_norm_g[l], n_ctx_chunks)
        yd = jnp.concatenate([yd_cm[:, :n_ctx], _to_row_major(yd_cm[:, n_ctx:])], axis=1)
        hyena = lambda p: _hyena_branch(p, hy_conv_w[l], hy_conv_b[l], hy_w1[l], hy_b1[l], hy_w2[l], hy_b2[l],
                                        hy_w3[l], hy_freq[l], hy_skip[l])
        yb_ctx = jnp.zeros((B, n_ctx, w), F32) if last_layer else hyena(p_hy[:, :n_ctx])
        yb = jnp.concatenate([yb_ctx, hyena(p_hy[:, n_ctx:])], axis=1).astype(BF16)

        wr = _pad_cols(w_router[l].astype(F32), LANES)
        x1, h2, aff = _merge((ya, yb, yc, yd), gate_pre, xc, g1, sh2, sc2, norm2_g[l],
                             w_branch[l].astype(BF16), w_out[l].astype(BF16), wr, n_ctx_tiles)
        wg, wu, wd = w_gate[l].astype(BF16), w_up[l].astype(BF16), w_down[l].astype(BF16)
        rows, vals = _expert_choice(h2, aff, n_ctx, n, wg, wu, wd)
        moe = jnp.zeros((B * T, D), F32).at[rows].add(vals)
        if not last_layer:
            rows_c, vals_c = _expert_choice(h2, aff, 0, n_ctx, wg, wu, wd)
            moe = moe.at[rows_c].add(vals_c)
        xc = _residual(x1, moe.reshape(B, T, D), g2, final_g, last_layer, n_ctx_tiles)
    return xc[:, n_ctx:]
```

```python
import functools
import math
import jax
import jax.numpy as jnp
from jax import lax
import numpy as np
from jax.experimental import pallas as pl
from jax.experimental.pallas import tpu as pltpu


D_MODEL = 1024
DEPTH = 2
GRID_W = 64
BRANCH_W = D_MODEL // 2
N_BRANCHES = 4
EPS = 1e-6
MASK_NEG = -1e30
LB_FLOOR = 1e-30

SSD_HEAD_DIM = 64
SSD_HEADS = BRANCH_W // SSD_HEAD_DIM
SSD_GROUPS = 2
SSD_STATE = 64
SSD_CONV_CH = BRANCH_W + 2 * SSD_GROUPS * SSD_STATE
SSD_COLS = BRANCH_W + SSD_CONV_CH + 2 * SSD_HEADS

HY_WIDTH = BRANCH_W
HY_ORDER = 2
HY_BANDS = 8
HY_DECAY_TARGET = 1e-2
HY_SHORT_PCT = 0.3
HY_LONG_PCT = 1.5
HY_COLS = (HY_ORDER + 1) * HY_WIDTH

ML_HEADS = 4
ML_HEAD_DIM = BRANCH_W // ML_HEADS
ML_COLS = 4 * BRANCH_W + 4 * ML_HEADS

HG_HEADS = 4
HG_HEAD_DIM = BRANCH_W // HG_HEADS
HG_COLS = 5 * BRANCH_W
HG_LEAF = 8

N_EXPERTS = 16
EC_CAPACITY = 2

LANES = 128
SUBLANES = 8
ROW_TILE = 256
SCAN_CHUNK = 128
VMEM_LIMIT = 56 * 1024 * 1024

F32 = jnp.float32
BF16 = jnp.bfloat16
HIGHEST = lax.Precision.HIGHEST


def _dot(a, b):
    return jnp.dot(a, b, preferred_element_type=F32)


def _dot_nt(a, b):
    return lax.dot_general(a, b, (((1,), (1,)), ((), ())), preferred_element_type=F32)


def _dot_tn(a, b):
    return lax.dot_general(a, b, (((0,), (0,)), ((), ())), preferred_element_type=F32)


def _dot_exact(a, b):
    return jnp.dot(a, b, preferred_element_type=F32, precision=HIGHEST)


def _dot_mask(mask, x):
    hi = x.astype(BF16)
    rem = x - hi.astype(F32)
    mid = rem.astype(BF16)
    lo = (rem - mid.astype(F32)).astype(BF16)
    return _dot(mask, hi) + _dot(mask, mid) + _dot(mask, lo)


def _sigmoid(x):
    return 1.0 / (1.0 + jnp.exp(-x))


def _silu(x):
    return x * _sigmoid(x)


def _softplus(x):
    return jnp.maximum(x, 0.0) + jnp.log1p(jnp.exp(-jnp.abs(x)))


def _log_sigmoid(x):
    return jnp.minimum(x, 0.0) - jnp.log1p(jnp.exp(-jnp.abs(x)))


def _rms(x):
    return x * lax.rsqrt(jnp.mean(x * x, axis=-1, keepdims=True) + EPS)


def _proj_kernel(x_ref, sh_ref, sc_ref, g_ref, *refs, n_out):
    h = (_rms(x_ref[...]) * g_ref[...] * (1.0 + sc_ref[...]) + sh_ref[...]).astype(BF16)
    for w_ref, o_ref in zip(refs[:n_out], refs[n_out:]):
        o_ref[...] = _dot(h, w_ref[...])


def _proj(xc, shift, scale, gain, weights, n_ctx_tiles):
    B, T, D = xc.shape
    mod_spec = pl.BlockSpec((None, None, 1, D), lambda b, i: (b, jnp.where(i < n_ctx_tiles, 1, 0), 0, 0))
    return pl.pallas_call(
        functools.partial(_proj_kernel, n_out=len(weights)),
        grid=(B, T // ROW_TILE),
        in_specs=[pl.BlockSpec((None, ROW_TILE, D), lambda b, i: (b, i, 0)), mod_spec, mod_spec,
                  pl.BlockSpec((1, D), lambda b, i: (0, 0))]
                 + [pl.BlockSpec(w.shape, lambda b, i: (0, 0)) for w in weights],
        out_specs=[pl.BlockSpec((None, ROW_TILE, w.shape[1]), lambda b, i: (b, i, 0)) for w in weights],
        out_shape=[jax.ShapeDtypeStruct((B, T, w.shape[1]), F32) for w in weights],
        compiler_params=pltpu.CompilerParams(dimension_semantics=("parallel", "parallel"),
                                             vmem_limit_bytes=VMEM_LIMIT),
        name="norm_mod_proj",
    )(xc, shift, scale, gain.reshape(1, D), *weights)


def _scan_chunk(step, reverse, n_ctx_chunks, n_chunks):
    if not reverse:
        return step
    return jnp.where(step < n_ctx_chunks, n_ctx_chunks - 1 - step, n_chunks - 1 - (step - n_ctx_chunks))


def _scan_specs(reverse, n_ctx_chunks, n_chunks, chunk):
    cidx = lambda s: _scan_chunk(s, reverse, n_ctx_chunks, n_chunks)
    per_tile = chunk // SUBLANES
    n_tiles = n_chunks * per_tile

    def main(width):
        return pl.BlockSpec((None, chunk, width), lambda b, s: (b, cidx(s), 0))

    def prev(width):
        return pl.BlockSpec((None, SUBLANES, width), lambda b, s: (b, jnp.maximum(cidx(s) * per_tile - 1, 0), 0))

    def nxt(width):
        return pl.BlockSpec((None, SUBLANES, width),
                            lambda b, s: (b, jnp.minimum((cidx(s) + 1) * per_tile, n_tiles - 1), 0))

    def const(shape):
        return pl.BlockSpec(shape, lambda b, s: (0,) * len(shape))

    return main, prev, nxt, const


def _segment_edges(chunk_idx, n_ctx_chunks, n_chunks):
    first = jnp.logical_or(chunk_idx == 0, chunk_idx == n_ctx_chunks)
    last = jnp.logical_or(chunk_idx == n_ctx_chunks - 1, chunk_idx == n_chunks - 1)
    return first, last


def _conv3(u, u_prev_tile, u_next_tile, w, bias, first, last):
    q = u.shape[0]
    rows = lax.broadcasted_iota(jnp.int32, u.shape, 0)
    before = jnp.where(first, 0.0, u_prev_tile[SUBLANES - 1:SUBLANES, :])
    after = jnp.where(last, 0.0, u_next_tile[0:1, :])
    u_m1 = jnp.where(rows == 0, before, pltpu.roll(u, 1, 0))
    u_p1 = jnp.where(rows == q - 1, after, pltpu.roll(u, q - 1, 0))
    return w[0:1, :] * u_m1 + w[1:2, :] * u + w[2:3, :] * u_p1 + bias


def _order_masks(q, reverse):
    r = lax.broadcasted_iota(jnp.int32, (q, q), 0)
    c = lax.broadcasted_iota(jnp.int32, (q, q), 1)
    return (r <= c) if reverse else (r >= c)


def _ssd_kernel(z_ref, x_ref, xp_ref, xn_ref, bc_ref, bcp_ref, bcn_ref, dt_ref, yf_ref,
                cwx_ref, cbx_ref, cwbc_ref, cbbc_ref, dtb_ref, nega_ref, dskip_ref, ng_ref,
                out_ref, st_ref, *, reverse, n_ctx_chunks, n_chunks):
    q = x_ref.shape[0]
    step = pl.program_id(1)
    cidx = _scan_chunk(step, reverse, n_ctx_chunks, n_chunks)
    first, last = _segment_edges(cidx, n_ctx_chunks, n_chunks)

    @pl.when(step == 0)
    def _():
        st_ref[...] = jnp.zeros_like(st_ref)

    xs = _silu(_conv3(x_ref[...], xp_ref[...], xn_ref[...], cwx_ref[...], cbx_ref[...], first, last))
    bc = _silu(_conv3(bc_ref[...], bcp_ref[...], bcn_ref[...], cwbc_ref[...], cbbc_ref[...], first, last))
    dt_all = _softplus(dt_ref[...] + dtb_ref[...])
    da_all = dt_all * nega_ref[...]
    mask = _order_masks(q, reverse)
    acs = _dot_mask(mask.astype(BF16), da_all)
    acs_t = acs.T
    dt_t = dt_all.T
    edge = 0 if reverse else q - 1
    gn = SSD_STATE
    ys = []
    for g in range(SSD_GROUPS):
        bg = bc[:, g * gn:(g + 1) * gn].astype(BF16)
        cg = bc[:, (SSD_GROUPS + g) * gn:(SSD_GROUPS + g + 1) * gn].astype(BF16)
        cb = _dot_nt(cg, bg)
        for hh in range(SSD_HEADS // SSD_GROUPS):
            h = g * (SSD_HEADS // SSD_GROUPS) + hh
            col = (SSD_HEADS if reverse else 0) + h
            a_col = acs[:, col:col + 1]
            a_row = acs_t[col:col + 1, :]
            a_end = a_row[:, edge:edge + 1]
            decay = jnp.where(mask, jnp.exp(jnp.minimum(a_col - a_row, 0.0)), 0.0)
            scores = (cb * decay * dt_t[col:col + 1, :]).astype(BF16)
            xh = xs[:, h * SSD_HEAD_DIM:(h + 1) * SSD_HEAD_DIM]
            st = st_ref[h]
            y = _dot(scores, xh.astype(BF16)) + _dot(cg, st.astype(BF16)) * jnp.exp(a_col)
            to_end = jnp.exp(a_end - a_col) * dt_all[:, col:col + 1]
            st_ref[h] = jnp.exp(a_end) * st + _dot_tn(bg, (xh * to_end).astype(BF16))
            ys.append(y)
    y = jnp.concatenate(ys, axis=1)
    if not reverse:
        out_ref[...] = y
    else:
        y = (y + yf_ref[...] + dskip_ref[...] * xs) * _silu(z_ref[...])
        out_ref[...] = (_rms(y) * ng_ref[...]).astype(out_ref.dtype)


def _ssd_pass(z, x, bc, dt, y_fwd, consts, reverse, n_ctx_chunks):
    B, T, _ = x.shape
    n_chunks = T // SCAN_CHUNK
    main, prev, nxt, const = _scan_specs(reverse, n_ctx_chunks, n_chunks, SCAN_CHUNK)
    w = BRANCH_W
    wbc = 2 * SSD_GROUPS * SSD_STATE
    in_specs = [main(w), main(w), prev(w), nxt(w), main(wbc), prev(wbc), nxt(wbc), main(LANES), main(w),
                const((3, w)), const((1, w)), const((3, wbc)), const((1, wbc)), const((1, LANES)),
                const((1, LANES)), const((1, w)), const((1, w))]
    return pl.pallas_call(
        functools.partial(_ssd_kernel, reverse=reverse, n_ctx_chunks=n_ctx_chunks, n_chunks=n_chunks),
        grid=(B, n_chunks),
        in_specs=in_specs,
        out_specs=main(w),
        out_shape=jax.ShapeDtypeStruct((B, T, w), BF16 if reverse else F32),
        scratch_shapes=[pltpu.VMEM((SSD_HEADS, SSD_STATE, SSD_HEAD_DIM), F32)],
        compiler_params=pltpu.CompilerParams(dimension_semantics=("parallel", "arbitrary"),
                                             vmem_limit_bytes=VMEM_LIMIT),
        name="ssd_bwd_finish" if reverse else "ssd_fwd",
    )(z, x, x, x, bc, bc, bc, dt, y_fwd, *consts)


def _ssd_branch(z, x, bc, dt, conv_w, conv_b, dt_bias, a_log, d_skip, norm_g, n_ctx_chunks):
    w = BRANCH_W
    pad = LANES - 2 * SSD_HEADS
    dtb = jnp.pad(dt_bias.astype(F32).reshape(1, -1), ((0, 0), (0, pad)))
    nega = jnp.pad(-jnp.exp(a_log.astype(F32)).reshape(1, -1), ((0, 0), (0, pad)))
    consts = (conv_w[:, :w], conv_b[:w].reshape(1, w), conv_w[:, w:], conv_b[w:].reshape(1, -1), dtb, nega,
              jnp.repeat(d_skip.astype(F32), SSD_HEAD_DIM).reshape(1, w), norm_g.reshape(1, w))
    y_f = _ssd_pass(z, x, bc, dt, x, consts, False, n_ctx_chunks)
    return _ssd_pass(z, x, bc, dt, y_f, consts, True, n_ctx_chunks)


def _mlstm_kernel(qk_ref, qkp_ref, qkn_ref, v_ref, o_ref, gt_ref, hf_ref, cw_ref, cb_ref, gb_ref, ng_ref,
                  out_ref, c_ref, m_ref, *, reverse, n_ctx_chunks, n_chunks):
    q = qk_ref.shape[0]
    dh = ML_HEAD_DIM
    step = pl.program_id(1)
    cidx = _scan_chunk(step, reverse, n_ctx_chunks, n_chunks)
    first, last = _segment_edges(cidx, n_ctx_chunks, n_chunks)

    @pl.when(step == 0)
    def _():
        c_ref[...] = jnp.zeros_like(c_ref)
        m_ref[...] = jnp.zeros_like(m_ref)

    qk = _silu(_conv3(qk_ref[...], qkp_ref[...], qkn_ref[...], cw_ref[...], cb_ref[...], first, last))
    gates = gt_ref[...] + gb_ref[...]
    logf_all = _log_sigmoid(gates)
    mask = _order_masks(q, reverse)
    bcum = _dot_mask(mask.astype(BF16), logf_all)
    bcum_t = bcum.T
    gates_t = gates.T
    edge = 0 if reverse else q - 1
    lane = lax.broadcasted_iota(jnp.int32, (q, dh), 1)
    ones_col = jnp.where(lane == 0, 1.0, 0.0).astype(BF16)
    hs = []
    for h in range(ML_HEADS):
        li = (ML_HEADS if reverse else 0) + h
        lf = 2 * ML_HEADS + li
        qh = qk[:, h * dh:(h + 1) * dh].astype(BF16)
        kh = qk[:, BRANCH_W + h * dh:BRANCH_W + (h + 1) * dh] * (dh ** -0.5)
        v_aug = jnp.concatenate([v_ref[:, h * dh:(h + 1) * dh].astype(BF16), ones_col], axis=1)
        b_col = bcum[:, lf:lf + 1]
        b_row = bcum_t[lf:lf + 1, :]
        i_col = gates[:, li:li + 1]
        i_row = gates_t[li:li + 1, :]
        b_tot = b_row[:, edge:edge + 1]
        m_prev = m_ref[h:h + 1, 0:1]
        c_prev = c_ref[h]
        dmat = jnp.where(mask, b_col - b_row + i_row, MASK_NEG)
        inter = b_col + m_prev
        m_t = jnp.maximum(inter, jnp.max(dmat, axis=1, keepdims=True))
        wgt = jnp.where(mask, jnp.exp(jnp.minimum(dmat - m_t, 0.0)), 0.0) * _dot_nt(qh, kh.astype(BF16))
        w_int = jnp.exp(inter - m_t)
        cross = _dot(qh, c_prev.astype(BF16))
        num = _dot(wgt.astype(BF16), v_aug[:, :dh]) + w_int * cross[:, :dh]
        den = jnp.sum(wgt, axis=1, keepdims=True) + w_int * cross[:, dh:dh + 1]
        hs.append(num / jnp.maximum(jnp.abs(den), jnp.exp(-m_t)))
        a = b_tot - b_col + i_col
        m_new = jnp.maximum(b_tot + m_prev, jnp.max(a, axis=0, keepdims=True))
        kw = (kh * jnp.exp(a - m_new)).astype(BF16)
        c_ref[h] = jnp.exp(b_tot + m_prev - m_new) * c_prev + _dot_tn(kw, v_aug)
        m_ref[h:h + 1, :] = jnp.broadcast_to(m_new, (1, LANES))
    if not reverse:
        out_ref[...] = jnp.concatenate(hs, axis=1)
    else:
        hf = hf_ref[...]
        outs = [_rms(hs[h] + hf[:, h * dh:(h + 1) * dh]) for h in range(ML_HEADS)]
        out_ref[...] = (_sigmoid(o_ref[...]) * jnp.concatenate(outs, axis=1) * ng_ref[...]).astype(out_ref.dtype)


def _mlstm_pass(qk, v, o, gates, h_fwd, consts, reverse, n_ctx_chunks):
    B, T, _ = v.shape
    n_chunks = T // SCAN_CHUNK
    main, prev, nxt, const = _scan_specs(reverse, n_ctx_chunks, n_chunks, SCAN_CHUNK)
    w = BRANCH_W
    in_specs = [main(2 * w), prev(2 * w), nxt(2 * w), main(w), main(w), main(LANES), main(w),
                const((3, 2 * w)), const((1, 2 * w)), const((1, LANES)), const((1, w))]
    return pl.pallas_call(
        functools.partial(_mlstm_kernel, reverse=reverse, n_ctx_chunks=n_ctx_chunks, n_chunks=n_chunks),
        grid=(B, n_chunks),
        in_specs=in_specs,
        out_specs=main(w),
        out_shape=jax.ShapeDtypeStruct((B, T, w), BF16 if reverse else F32),
        scratch_shapes=[pltpu.VMEM((ML_HEADS, ML_HEAD_DIM, 2 * ML_HEAD_DIM), F32),
                        pltpu.VMEM((SUBLANES, LANES), F32)],
        compiler_params=pltpu.CompilerParams(dimension_semantics=("parallel", "arbitrary"),
                                             vmem_limit_bytes=VMEM_LIMIT),
        name="mlstm_bwd_finish" if reverse else "mlstm_fwd",
    )(qk, qk, qk, v, o, gates, h_fwd, *consts)


def _mlstm_branch(qk, v, o, gates, conv_w, conv_b, i_bias, f_bias, norm_g, n_ctx_chunks):
    w = BRANCH_W
    gb = jnp.concatenate([i_bias.astype(F32).reshape(-1), f_bias.astype(F32).reshape(-1)])
    gb = jnp.pad(gb.reshape(1, -1), ((0, 0), (0, LANES - 4 * ML_HEADS)))
    consts = (conv_w, conv_b.reshape(1, 2 * w), gb, norm_g.reshape(1, w))
    h_f = _mlstm_pass(qk, v, o, gates, v, consts, False, n_ctx_chunks)
    return _mlstm_pass(qk, v, o, gates, h_f, consts, True, n_ctx_chunks)


def _hgrn_level_matrix(q, block, reverse):
    t = lax.broadcasted_iota(jnp.int32, (q, q), 0)
    r = lax.broadcasted_iota(jnp.int32, (q, q), 1)
    blk = t // block
    start = blk * block
    end = start + block - 1
    odd = (blk % 2) == 1
    if not reverse:
        lo = jnp.where(odd, start, t + 1)
        hi = jnp.where(odd, t, end)
    else:
        lo = jnp.where(odd, start, t)
        hi = jnp.where(odd, t - 1, end)
    return jnp.where((r >= lo) & (r <= hi), 1.0, 0.0)


def _hgrn_select_matrix(q, reverse):
    mats = [_order_masks(q, reverse).astype(F32)]
    block = HG_LEAF
    while block < q:
        mats.append(_hgrn_level_matrix(q, block, reverse))
        block *= 2
    return jnp.concatenate(mats, axis=0).astype(BF16)


def _hgrn_kernel(q_ref, v_ref, f_ref, g_ref, of_ref, lb_ref, ng_ref, sel_ref, out_ref,
                 st_ref, b_ref, qs_ref, ks_ref, ol_ref, *, reverse):
    q = q_ref.shape[0]
    dh = HG_HEAD_DIM
    step = pl.program_id(1)

    @pl.when(step == 0)
    def _():
        st_ref[...] = jnp.zeros_like(st_ref)

    lb = lb_ref[...]
    pre = f_ref[...]
    log_lb = jnp.log(jnp.maximum(lb, LB_FLOOR))
    log_ub = jnp.log1p(-lb)
    e_pre = jnp.exp(-jnp.abs(pre))
    lo = log_ub + jnp.minimum(pre, 0.0) - jnp.log1p(e_pre)
    logf = jnp.maximum(log_lb, lo) + jnp.log1p(jnp.exp(-jnp.abs(log_lb - lo)))
    key = (1.0 - lb) * (jnp.where(pre >= 0.0, e_pre, 1.0) / (1.0 + e_pre))
    qv = _silu(q_ref[...])
    sums = _dot_mask(sel_ref[...], logf)
    bq = sums[:q]
    b_ref[...] = bq
    qs_ref[...] = qv
    ks_ref[...] = key

    def leaf_body(l, carry):
        rows = pl.ds(pl.multiple_of(l * HG_LEAF, HG_LEAF), HG_LEAF)
        bl = b_ref[rows, :]
        ql = qs_ref[rows, :]
        kl = ks_ref[rows, :]
        vl = v_ref[rows, :]
        tt = lax.broadcasted_iota(jnp.int32, (HG_LEAF, dh), 0)
        acc = [jnp.zeros((HG_LEAF, dh), F32) for _ in range(HG_HEADS)]
        for s in range(HG_LEAF):
            ok = (tt <= s) if reverse else (tt >= s)
            for h in range(HG_HEADS):
                sl = slice(h * dh, (h + 1) * dh)
                diff = bl[:, sl] - bl[s:s + 1, sl]
                e = jnp.exp(jnp.where(ok, diff, MASK_NEG))
                att = jnp.sum(ql[:, sl] * e * kl[s:s + 1, sl], axis=1, keepdims=True)
                acc[h] = acc[h] + att * vl[s:s + 1, sl]
        ol_ref[rows, :] = jnp.concatenate(acc, axis=1)
        return carry

    lax.fori_loop(0, q // HG_LEAF, leaf_body, 0)

    t_idx = lax.broadcasted_iota(jnp.int32, (q, q), 0)
    s_idx = lax.broadcasted_iota(jnp.int32, (q, q), 1)
    row_t = lax.broadcasted_iota(jnp.int32, (q, dh), 0)
    levels = []
    block = HG_LEAF
    while block < q:
        fac = jnp.exp(sums[(len(levels) + 1) * q:(len(levels) + 2) * q])
        tb, sb = t_idx // block, s_idx // block
        if not reverse:
            pair = ((tb % 2) == 1) & (sb == tb - 1)
            is_query = ((row_t // block) % 2) == 1
        else:
            pair = ((tb % 2) == 0) & (sb == tb + 1)
            is_query = ((row_t // block) % 2) == 0
        levels.append((fac, pair, is_query))
        block *= 2
    edge = 0 if reverse else q - 1
    b_end = bq[edge:edge + 1, :]
    q_in = qv * jnp.exp(bq)
    k_out = key * jnp.exp(b_end - bq)
    outs = []
    for h in range(HG_HEADS):
        sl = slice(h * dh, (h + 1) * dh)
        att = jnp.zeros((q, q), F32)
        for fac, pair, is_query in levels:
            qt = jnp.where(is_query, qv[:, sl] * fac[:, sl], 0.0).astype(BF16)
            kt = jnp.where(is_query, 0.0, key[:, sl] * fac[:, sl]).astype(BF16)
            att = att + jnp.where(pair, _dot_nt(qt, kt), 0.0)
        vh = v_ref[:, sl].astype(BF16)
        st = st_ref[h]
        o = ol_ref[:, sl] + _dot(att.astype(BF16), vh) + _dot_nt(q_in[:, sl].astype(BF16), st.astype(BF16))
        st_ref[h] = jnp.exp(b_end[:, sl]) * st + _dot_tn(vh, k_out[:, sl].astype(BF16))
        outs.append(o)
    if not reverse:
        out_ref[...] = jnp.concatenate(outs, axis=1)
    else:
        of = of_ref[...]
        fin = [_rms(outs[h] + of[:, h * dh:(h + 1) * dh]) for h in range(HG_HEADS)]
        out_ref[...] = (jnp.concatenate(fin, axis=1) * ng_ref[...] * _sigmoid(g_ref[...])).astype(out_ref.dtype)


def _hgrn_pass(qr, v, f, g, o_fwd, lb, norm_g, reverse, n_ctx_chunks):
    B, T, w = v.shape
    n_chunks = T // SCAN_CHUNK
    main, _, _, const = _scan_specs(reverse, n_ctx_chunks, n_chunks, SCAN_CHUNK)
    sel = _hgrn_select_matrix(SCAN_CHUNK, reverse)
    return pl.pallas_call(
        functools.partial(_hgrn_kernel, reverse=reverse),
        grid=(B, n_chunks),
        in_specs=[main(w), main(w), main(w), main(w), main(w), const((1, w)), const((1, w)), const(sel.shape)],
        out_specs=main(w),
        out_shape=jax.ShapeDtypeStruct((B, T, w), BF16 if reverse else F32),
        scratch_shapes=[pltpu.VMEM((HG_HEADS, HG_HEAD_DIM, HG_HEAD_DIM), F32)]
                       + [pltpu.VMEM((SCAN_CHUNK, w), F32) for _ in range(4)],
        compiler_params=pltpu.CompilerParams(dimension_semantics=("parallel", "arbitrary"),
                                             vmem_limit_bytes=VMEM_LIMIT),
        name="hgrn_bwd_finish" if reverse else "hgrn_fwd",
    )(qr, v, f, g, o_fwd, lb.reshape(1, w), norm_g.reshape(1, w), sel)


def _hgrn_branch(qr, v, f_fwd, f_bwd, g, lb, norm_g, n_ctx_chunks):
    o_f = _hgrn_pass(qr, v, f_fwd, g, v, lb, norm_g, False, n_ctx_chunks)
    return _hgrn_pass(qr, v, f_bwd, g, o_f, lb, norm_g, True, n_ctx_chunks)


def _hyena_filter_time(n, w1, b1, w2, b2, w3, freq):
    pos = jnp.arange(n, dtype=F32)
    t = pos / max(n - 1, 1)
    bands = jnp.arange(1, HY_BANDS + 1, dtype=F32)
    ang = (2.0 * math.pi / n) * pos[:, None] * bands[None, :]
    feats = jnp.concatenate([t[:, None], jnp.cos(ang), jnp.sin(ang)], axis=-1)
    hid = jnp.sin(freq[0] * (feats @ w1 + b1))
    hid = jnp.sin(freq[1] * (hid @ w2 + b2))
    h = (hid @ w3).reshape(n, HY_ORDER, 2, HY_WIDTH)
    deltas = jnp.abs(jnp.linspace(math.log(HY_DECAY_TARGET) / HY_LONG_PCT,
                                  math.log(HY_DECAY_TARGET) / HY_SHORT_PCT, HY_WIDTH, dtype=F32))
    h = h * jnp.exp(-t[:, None] * deltas[None, :])[:, None, None, :]
    two_sided = jnp.concatenate([h[:, :, 0], jnp.zeros((1, HY_ORDER, HY_WIDTH), F32),
                                 jnp.flip(h[1:, :, 1], axis=0)], axis=0)
    return two_sided / (jnp.sum(jnp.abs(two_sided), axis=0, keepdims=True) + EPS)


def _hyena_conv_kernel(p_ref, pp_ref, pn_ref, w_ref, b_ref, v_ref, x1_ref, x2_ref, *, n_ctx_tiles, n_tiles):
    i = pl.program_id(1)
    first, last = _segment_edges(i, n_ctx_tiles, n_tiles)
    u = _conv3(p_ref[...], pp_ref[...], pn_ref[...], w_ref[...], b_ref[...], first, last)
    wd = HY_WIDTH
    v_ref[...] = u[:, :wd]
    x1_ref[...] = u[:, wd:2 * wd]
    x2_ref[...] = u[:, 2 * wd:]


def _hyena_conv(p, conv_w, conv_b, n_ctx_tiles):
    B, T, C = p.shape
    n_tiles = T // ROW_TILE
    main, prev, nxt, const = _scan_specs(False, n_ctx_tiles, n_tiles, ROW_TILE)
    return pl.pallas_call(
        functools.partial(_hyena_conv_kernel, n_ctx_tiles=n_ctx_tiles, n_tiles=n_tiles),
        grid=(B, n_tiles),
        in_specs=[main(C), prev(C), nxt(C), const((3, C)), const((1, C))],
        out_specs=[main(HY_WIDTH)] * 3,
        out_shape=[jax.ShapeDtypeStruct((B, T, HY_WIDTH), F32)] * 3,
        compiler_params=pltpu.CompilerParams(dimension_semantics=("parallel", "parallel")),
        name="hyena_short_conv",
    )(p, p, p, conv_w, conv_b.reshape(1, C))


DFT_N2 = 256


def _split_bf16(x):
    hi = x.astype(BF16)
    return hi, (x - hi.astype(F32)).astype(BF16)


def _dot_split(m_hi, m_lo, x):
    x_hi, x_lo = _split_bf16(x)
    return _dot(m_hi, x_hi) + _dot(m_hi, x_lo) + _dot(m_lo, x_hi)


def _dft_tables(n):
    L = 2 * n
    n1_full = L // DFT_N2
    nh = n1_full // 2
    nk = nh + 1
    nkp = -(-nk // SUBLANES) * SUBLANES
    k1 = jnp.arange(nkp, dtype=jnp.int32)
    valid = (k1 < nk)[:, None]

    def stage1(rows):
        n1 = jnp.arange(rows, dtype=jnp.int32)
        ang = (2.0 * math.pi / n1_full) * ((k1[:, None] * n1[None, :]) % n1_full).astype(F32)
        return jnp.concatenate([jnp.where(valid, jnp.cos(ang), 0.0), jnp.where(valid, -jnp.sin(ang), 0.0)], axis=0)

    n1 = jnp.arange(nh, dtype=jnp.int32)
    ang = (2.0 * math.pi / n1_full) * ((n1[:, None] * k1[None, :]) % n1_full).astype(F32)
    ck = jnp.where((k1 == 0) | (k1 == nh), 1.0, 2.0) * jnp.where(k1 < nk, 1.0, 0.0) / L
    stage3 = jnp.concatenate([jnp.cos(ang) * ck[None, :], -jnp.sin(ang) * ck[None, :]], axis=1)
    n2 = jnp.arange(DFT_N2, dtype=jnp.int32)
    idx = (n2[None, :, None] * (k1[:, None, None] + n1_full * n2[None, None, :])) % L
    ang2 = (2.0 * math.pi / L) * idx.astype(F32)
    gr, gi = jnp.cos(ang2), -jnp.sin(ang2)
    grt, git = jnp.swapaxes(gr, 1, 2), jnp.swapaxes(gi, 1, 2)
    m_fwd = jnp.concatenate([jnp.concatenate([grt, -git], axis=2), jnp.concatenate([git, grt], axis=2)], axis=1)
    m_inv = jnp.swapaxes(m_fwd, 1, 2)
    return dict(nh=nh, nk=nk, nkp=nkp, f1_data=_split_bf16(stage1(nh)), f1_filter=_split_bf16(stage1(n1_full)),
                f3=_split_bf16(stage3), m_fwd=_split_bf16(m_fwd), m_inv=_split_bf16(m_inv))


def _dft_in_kernel(u_ref, fh_ref, fl_ref, o_ref):
    nkp = o_ref.shape[1]
    r = _dot_split(fh_ref[...], fl_ref[...], u_ref[...])
    o_ref[0] = r[:nkp]
    o_ref[1] = r[nkp:]


def _dft_in(u, f_hi, f_lo, col_tile):
    G, rows, cols = u.shape
    nkp = f_hi.shape[0] // 2
    return pl.pallas_call(
        _dft_in_kernel,
        grid=(G, cols // col_tile),
        in_specs=[pl.BlockSpec((None, rows, col_tile), lambda g, j: (g, 0, j)),
                  pl.BlockSpec(f_hi.shape, lambda g, j: (0, 0)), pl.BlockSpec(f_lo.shape, lambda g, j: (0, 0))],
        out_specs=pl.BlockSpec((None, 2, nkp, col_tile), lambda g, j: (g, 0, 0, j)),
        out_shape=jax.ShapeDtypeStruct((G, 2, nkp, cols), F32),
        compiler_params=pltpu.CompilerParams(dimension_semantics=("parallel", "parallel"),
                                             vmem_limit_bytes=VMEM_LIMIT),
        name="hyena_dft_rows",
    )(u, f_hi, f_lo)


def _dft_mid_kernel(a_ref, h_ref, mfh_ref, mfl_ref, mih_ref, mil_ref, o_ref, *, nk, spectrum_only):
    k1 = pl.program_id(0)
    half = DFT_N2

    @pl.when(k1 < nk)
    def _():
        x = jnp.concatenate([a_ref[0], a_ref[1]], axis=0)
        u = _dot_split(mfh_ref[...], mfl_ref[...], x)
        if spectrum_only:
            y = u
        else:
            ur, ui = u[:half], u[half:]
            hr, hi = h_ref[0], h_ref[1]
            v = jnp.concatenate([ur * hr - ui * hi, ur * hi + ui * hr], axis=0)
            y = _dot_split(mih_ref[...], mil_ref[...], v)
        o_ref[0] = y[:half]
        o_ref[1] = y[half:]

    @pl.when(k1 >= nk)
    def _():
        o_ref[...] = jnp.zeros_like(o_ref)


def _dft_mid(a, h, tables, spectrum_only):
    G, _, nkp, n2, wd = a.shape
    slab = lambda g_of: pl.BlockSpec((None, 2, None, n2, wd), g_of)
    mat = pl.BlockSpec((None, 2 * n2, 2 * n2), lambda k, g: (k, 0, 0))
    return pl.pallas_call(
        functools.partial(_dft_mid_kernel, nk=tables["nk"], spectrum_only=spectrum_only),
        grid=(nkp, G),
        in_specs=[slab(lambda k, g: (g, 0, k, 0, 0)),
                  pl.BlockSpec((2, None, n2, wd), lambda k, g: (0, k, 0, 0)), mat, mat, mat, mat],
        out_specs=slab(lambda k, g: (g, 0, k, 0, 0)),
        out_shape=jax.ShapeDtypeStruct(a.shape, F32),
        compiler_params=pltpu.CompilerParams(dimension_semantics=("parallel", "parallel"),
                                             vmem_limit_bytes=VMEM_LIMIT),
        name="hyena_spectrum" if spectrum_only else "hyena_dft_mid",
    )(a, h, *tables["m_fwd"], *tables["m_inv"])


def _dft_out_kernel(b_ref, u_ref, x_ref, skip_ref, fh_ref, fl_ref, o_ref):
    bb = jnp.concatenate([b_ref[0], b_ref[1]], axis=0)
    y = _dot_split(fh_ref[...], fl_ref[...], bb)
    o_ref[...] = x_ref[...] * (y + u_ref[...] * skip_ref[...])


def _dft_out(bq, u, xg, skip_row, f_hi, f_lo, col_tile):
    G, rows, cols = u.shape
    nkp = bq.shape[2]
    tile = pl.BlockSpec((None, rows, col_tile), lambda g, j: (g, 0, j))
    return pl.pallas_call(
        _dft_out_kernel,
        grid=(G, cols // col_tile),
        in_specs=[pl.BlockSpec((None, 2, nkp, col_tile), lambda g, j: (g, 0, 0, j)), tile, tile,
                  pl.BlockSpec((1, col_tile), lambda g, j: (0, 0)),
                  pl.BlockSpec(f_hi.shape, lambda g, j: (0, 0)), pl.BlockSpec(f_lo.shape, lambda g, j: (0, 0))],
        out_specs=tile,
        out_shape=jax.ShapeDtypeStruct(u.shape, F32),
        compiler_params=pltpu.CompilerParams(dimension_semantics=("parallel", "parallel"),
                                             vmem_limit_bytes=VMEM_LIMIT),
        name="hyena_dft_rows_inverse",
    )(bq, u, xg, skip_row, f_hi, f_lo)


def _hyena_latent(v, x1, x2, filt, skip):
    B, n, wd = v.shape
    tb = _dft_tables(n)
    nh, nkp = tb["nh"], tb["nkp"]
    cols = DFT_N2 * wd
    col_tile = 4 * wd
    hs = jnp.swapaxes(filt, 0, 1).reshape(HY_ORDER, 2 * nh, cols)
    spec = _dft_in(hs, *tb["f1_filter"], col_tile).reshape(HY_ORDER, 2, nkp, DFT_N2, wd)
    spec = _dft_mid(spec, spec[0], tb, True)
    z = v.reshape(B, nh, cols)
    for o, xg in enumerate((x1, x2)):
        a = _dft_in(z, *tb["f1_data"], col_tile).reshape(B, 2, nkp, DFT_N2, wd)
        bq = _dft_mid(a, spec[o], tb, False).reshape(B, 2, nkp, cols)
        skip_row = jnp.tile(skip[o].astype(F32), col_tile // wd).reshape(1, col_tile)
        z = _dft_out(bq, z, xg.reshape(B, nh, cols), skip_row, *tb["f3"], col_tile)
    return z.reshape(B, n, wd)


def _hyena_context(v, x1, x2, filt, skip):
    n = v.shape[1]
    spec = jnp.fft.rfft(filt, axis=0)
    z = v
    for o, xg in enumerate((x1, x2)):
        zf = jnp.fft.rfft(z, n=2 * n, axis=1)
        y = jnp.fft.irfft(zf * spec[:, o], n=2 * n, axis=1)[:, :n]
        z = xg * (y + z * skip[o].astype(F32))
    return z


def _merge_kernel(ya_ref, yb_ref, yc_ref, yd_ref, gate_ref, x_ref, g1_ref, sh_ref, sc_ref, ng_ref,
                  wb_ref, wo_ref, wr_ref, x1_ref, h2_ref, aff_ref):
    d = x_ref.shape[1]
    merged = None
    for i, y_ref in enumerate((ya_ref, yb_ref, yc_ref, yd_ref)):
        term = _sigmoid(gate_ref[:, i * d:(i + 1) * d]) * _dot(y_ref[...], wb_ref[i])
        merged = term if merged is None else merged + term
    x1 = x_ref[...] + g1_ref[...] * _dot(merged.astype(BF16), wo_ref[...])
    x1_ref[...] = x1
    h2 = _rms(x1) * ng_ref[...] * (1.0 + sc_ref[...]) + sh_ref[...]
    h2_ref[...] = h2.astype(h2_ref.dtype)
    logits = _dot_exact(h2, wr_ref[...])
    lane = lax.broadcasted_iota(jnp.int32, logits.shape, 1)
    logits = jnp.where(lane < N_EXPERTS, logits, MASK_NEG)
    e = jnp.exp(logits - jnp.max(logits, axis=1, keepdims=True))
    aff_ref[...] = e / jnp.sum(e, axis=1, keepdims=True)


def _merge(ys, gates, xc, g1, shift, scale, gain, wb, wo, wr, n_ctx_tiles):
    B, T, D = xc.shape
    w = BRANCH_W
    tile = lambda width: pl.BlockSpec((None, ROW_TILE, width), lambda b, i: (b, i, 0))
    mod_spec = pl.BlockSpec((None, None, 1, D), lambda b, i: (b, jnp.where(i < n_ctx_tiles, 1, 0), 0, 0))
    const = lambda shape: pl.BlockSpec(shape, lambda b, i: (0,) * len(shape))
    return pl.pallas_call(
        _merge_kernel,
        grid=(B, T // ROW_TILE),
        in_specs=[tile(w), tile(w), tile(w), tile(w), tile(N_BRANCHES * D), tile(D), mod_spec, mod_spec, mod_spec,
                  const((1, D)), const(wb.shape), const(wo.shape), const(wr.shape)],
        out_specs=[tile(D), tile(D), tile(LANES)],
        out_shape=[jax.ShapeDtypeStruct((B, T, D), F32), jax.ShapeDtypeStruct((B, T, D), BF16),
                   jax.ShapeDtypeStruct((B, T, LANES), F32)],
        compiler_params=pltpu.CompilerParams(dimension_semantics=("parallel", "parallel"),
                                             vmem_limit_bytes=VMEM_LIMIT),
        name="merge_out_router",
    )(*ys, gates, xc, g1, shift, scale, gain.reshape(1, D), wb, wo, wr)


def _expert_kernel(x_ref, s_ref, wg_ref, wu_ref, wd_ref, o_ref):
    x = x_ref[...]
    hid = _silu(_dot(x, wg_ref[...])) * _dot(x, wu_ref[...])
    o_ref[...] = _dot(hid.astype(BF16), wd_ref[...]) * s_ref[...]


def _experts(xg, score, wg, wu, wd):
    E, R, D = xg.shape
    F = wg.shape[2]
    tm = min(R, 512)
    return pl.pallas_call(
        _expert_kernel,
        grid=(E, R // tm),
        in_specs=[pl.BlockSpec((None, tm, D), lambda e, i: (e, i, 0)),
                  pl.BlockSpec((None, tm, 1), lambda e, i: (e, i, 0)),
                  pl.BlockSpec((None, D, F), lambda e, i: (e, 0, 0)),
                  pl.BlockSpec((None, D, F), lambda e, i: (e, 0, 0)),
                  pl.BlockSpec((None, F, D), lambda e, i: (e, 0, 0))],
        out_specs=pl.BlockSpec((None, tm, D), lambda e, i: (e, i, 0)),
        out_shape=jax.ShapeDtypeStruct((E, R, D), F32),
        compiler_params=pltpu.CompilerParams(dimension_semantics=("parallel", "arbitrary"),
                                             vmem_limit_bytes=VMEM_LIMIT),
        name="expert_swiglu",
    )(xg, score, wg, wu, wd)


def _expert_choice(h2, aff, seg_start, seg_len, wg, wu, wd):
    B, T, D = h2.shape
    cap = EC_CAPACITY * seg_len // N_EXPERTS
    a = aff[:, seg_start:seg_start + seg_len, :N_EXPERTS]
    score, idx = lax.top_k(jnp.swapaxes(a, 1, 2), cap)
    rows = idx + seg_start + (jnp.arange(B, dtype=idx.dtype) * T)[:, None, None]
    rows = jnp.swapaxes(rows, 0, 1).reshape(N_EXPERTS, B * cap)
    score = jnp.swapaxes(score, 0, 1).reshape(N_EXPERTS, B * cap, 1)
    xg = jnp.take(h2.reshape(B * T, D), rows, axis=0)
    ye = _experts(xg, score, wg, wu, wd)
    return rows.reshape(-1), ye.reshape(-1, D)


def _residual_kernel(x_ref, m_ref, g2_ref, gain_ref, o_ref, *, final):
    x = x_ref[...] + g2_ref[...] * m_ref[...]
    o_ref[...] = _rms(x) * gain_ref[...] if final else x


def _residual(x1, moe, g2, gain, final, n_ctx_tiles):
    B, T, D = x1.shape
    tile = pl.BlockSpec((None, ROW_TILE, D), lambda b, i: (b, i, 0))
    mod_spec = pl.BlockSpec((None, None, 1, D), lambda b, i: (b, jnp.where(i < n_ctx_tiles, 1, 0), 0, 0))
    return pl.pallas_call(
        functools.partial(_residual_kernel, final=final),
        grid=(B, T // ROW_TILE),
        in_specs=[tile, tile, mod_spec, pl.BlockSpec((1, D), lambda b, i: (0, 0))],
        out_specs=tile,
        out_shape=jax.ShapeDtypeStruct((B, T, D), F32),
        compiler_params=pltpu.CompilerParams(dimension_semantics=("parallel", "parallel")),
        name="moe_residual_final_norm" if final else "moe_residual",
    )(x1, moe, g2, gain.reshape(1, D))


def _to_col_major(t):
    b, n = t.shape[:2]
    rows = n // GRID_W
    return t.reshape((b, rows, GRID_W) + t.shape[2:]).swapaxes(1, 2).reshape(t.shape)


def _to_row_major(t):
    b, n = t.shape[:2]
    rows = n // GRID_W
    return t.reshape((b, GRID_W, rows) + t.shape[2:]).swapaxes(1, 2).reshape(t.shape)


def _pad_cols(w, width):
    return jnp.pad(w, ((0, 0), (0, width - w.shape[1])))


def kernel(x, c, ctx, c_ctx, w_ada, b_ada, norm1_g, norm2_g, w_in, ssd_conv_w, ssd_conv_b, ssd_dt_bias, ssd_a_log, ssd_d, ssd_norm_g, hy_conv_w, hy_conv_b, hy_w1, hy_b1, hy_w2, hy_b2, hy_w3, hy_freq, hy_skip, ml_conv_w, ml_conv_b, ml_i_bias, ml_f_bias, ml_norm_g, hg_lb_logits, hg_norm_g, w_branch, w_out, w_router, w_gate, w_up, w_down, final_g):
    B, n, D = x.shape
    n_ctx = ctx.shape[1]
    T = n_ctx + n
    assert n_ctx % ROW_TILE == 0 and n % ROW_TILE == 0 and n % GRID_W == 0
    n_ctx_tiles = n_ctx // ROW_TILE
    n_ctx_chunks = n_ctx // SCAN_CHUNK
    w = BRANCH_W

    p_lb = jax.nn.softmax(hg_lb_logits.astype(F32), axis=0)
    lower_bounds = jnp.maximum(jnp.cumsum(p_lb, axis=0) - p_lb[0], 0.0)
    xc = jnp.concatenate([ctx, x], axis=1)
    depth = w_in.shape[0]
    for l in range(depth):
        last_layer = l == depth - 1
        mod = jax.nn.silu(c) @ w_ada[l] + b_ada[l]
        mod_c = jax.nn.silu(c_ctx) @ w_ada[l] + b_ada[l]
        mods = jnp.stack([mod, jnp.broadcast_to(mod_c, mod.shape)], axis=1).reshape(B, 2, 6, 1, D)
        sh1, sc1, g1, sh2, sc2, g2 = (mods[:, :, i] for i in range(6))

        wl = w_in[l].astype(BF16)
        o_ssd, o_hy, o_ml, o_hg, o_gate = 0, SSD_COLS, SSD_COLS + HY_COLS, SSD_COLS + HY_COLS + ML_COLS, \
            SSD_COLS + HY_COLS + ML_COLS + HG_COLS
        cols = lambda a, b_: wl[:, a:b_]
        z, xs, bc, dt, qk, v, o, gts = _proj(xc, sh1, sc1, norm1_g[l], [
            cols(o_ssd, o_ssd + w), cols(o_ssd + w, o_ssd + 2 * w), cols(o_ssd + 2 * w, o_ssd + w + SSD_CONV_CH),
            _pad_cols(cols(o_ssd + w + SSD_CONV_CH, o_hy), LANES),
            cols(o_ml, o_ml + 2 * w), cols(o_ml + 2 * w, o_ml + 3 * w), cols(o_ml + 3 * w, o_ml + 4 * w),
            _pad_cols(cols(o_ml + 4 * w, o_hg), LANES)], n_ctx_tiles)
        p_hy, gate_pre = _proj(xc, sh1, sc1, norm1_g[l], [cols(o_hy, o_ml), cols(o_gate, o_gate + N_BRANCHES * D)],
                               n_ctx_tiles)
        xc_cm = jnp.concatenate([xc[:, :n_ctx], _to_col_major(xc[:, n_ctx:])], axis=1)
        hq, hi, hff, hfb, hgt = _proj(xc_cm, sh1, sc1, norm1_g[l],
                                      [cols(o_hg + i * w, o_hg + (i + 1) * w) for i in range(5)], n_ctx_tiles)

        ya = _ssd_branch(z, xs, bc, dt, ssd_conv_w[l], ssd_conv_b[l], ssd_dt_bias[l], ssd_a_log[l], ssd_d[l],
                         ssd_norm_g[l], n_ctx_chunks)
        yc = _mlstm_branch(qk, v, o, gts, ml_conv_w[l], ml_conv_b[l], ml_i_bias[l], ml_f_bias[l], ml_norm_g[l],
                           n_ctx_chunks)
        yd_cm = _hgrn_branch(hq, hi, hff, hfb, hgt, lower_bounds[l], hg_norm_g[l], n_ctx_chunks)
        yd = jnp.concatenate([yd_cm[:, :n_ctx], _to_row_major(yd_cm[:, n_ctx:])], axis=1)
        hv, hx1, hx2 = _hyena_conv(p_hy, hy_conv_w[l], hy_conv_b[l], n_ctx_tiles)
        filt = lambda m: _hyena_filter_time(m, hy_w1[l].astype(F32), hy_b1[l].astype(F32), hy_w2[l].astype(F32),
                                            hy_b2[l].astype(F32), hy_w3[l].astype(F32), hy_freq[l].astype(F32))
        lat = lambda t: t[:, n_ctx:]
        yb_lat = _hyena_latent(lat(hv), lat(hx1), lat(hx2), filt(n), hy_skip[l])
        if last_layer:
            yb_ctx = jnp.zeros((B, n_ctx, w), F32)
        else:
            head = lambda t: t[:, :n_ctx]
            yb_ctx = _hyena_context(head(hv), head(hx1), head(hx2), filt(n_ctx), hy_skip[l])
        yb = jnp.concatenate([yb_ctx, yb_lat], axis=1).astype(BF16)

        wr = _pad_cols(w_router[l].astype(F32), LANES)
        x1, h2, aff = _merge((ya, yb, yc, yd), gate_pre, xc, g1, sh2, sc2, norm2_g[l],
                             w_branch[l].astype(BF16), w_out[l].astype(BF16), wr, n_ctx_tiles)
        wg, wu, wd = w_gate[l].astype(BF16), w_up[l].astype(BF16), w_down[l].astype(BF16)
        rows, vals = _expert_choice(h2, aff, n_ctx, n, wg, wu, wd)
        moe = jnp.zeros((B * T, D), F32).at[rows].add(vals)
        if not last_layer:
            rows_c, vals_c = _expert_choice(h2, aff, 0, n_ctx, wg, wu, wd)
            moe = moe.at[rows_c].add(vals_c)
        xc = _residual(x1, moe.reshape(B, T, D), g2, final_g, last_layer, n_ctx_tiles)
    return xc[:, n_ctx:]
```

```python
import functools
import math
import jax
import jax.numpy as jnp
from jax import lax
import numpy as np
from jax.experimental import pallas as pl
from jax.experimental.pallas import tpu as pltpu


D_MODEL = 1024
DEPTH = 2
GRID_W = 64
BRANCH_W = D_MODEL // 2
N_BRANCHES = 4
EPS = 1e-6
MASK_NEG = -1e30
LB_FLOOR = 1e-30

SSD_HEAD_DIM = 64
SSD_HEADS = BRANCH_W // SSD_HEAD_DIM
SSD_GROUPS = 2
SSD_STATE = 64
SSD_CONV_CH = BRANCH_W + 2 * SSD_GROUPS * SSD_STATE
SSD_COLS = BRANCH_W + SSD_CONV_CH + 2 * SSD_HEADS

HY_WIDTH = BRANCH_W
HY_ORDER = 2
HY_BANDS = 8
HY_DECAY_TARGET = 1e-2
HY_SHORT_PCT = 0.3
HY_LONG_PCT = 1.5
HY_COLS = (HY_ORDER + 1) * HY_WIDTH

ML_HEADS = 4
ML_HEAD_DIM = BRANCH_W // ML_HEADS
ML_COLS = 4 * BRANCH_W + 4 * ML_HEADS

HG_HEADS = 4
HG_HEAD_DIM = BRANCH_W // HG_HEADS
HG_COLS = 5 * BRANCH_W
HG_LEAF = 8

N_EXPERTS = 16
EC_CAPACITY = 2

LANES = 128
SUBLANES = 8
ROW_TILE = 256
SCAN_CHUNK = 128
VMEM_LIMIT = 56 * 1024 * 1024

F32 = jnp.float32
BF16 = jnp.bfloat16


def _dot(a, b):
    return jnp.dot(a, b, preferred_element_type=F32)


def _dot_nt(a, b):
    return lax.dot_general(a, b, (((1,), (1,)), ((), ())), preferred_element_type=F32)


def _dot_tn(a, b):
    return lax.dot_general(a, b, (((0,), (0,)), ((), ())), preferred_element_type=F32)


def _dot_mask(mask, x):
    hi = x.astype(BF16)
    rem = x - hi.astype(F32)
    mid = rem.astype(BF16)
    lo = (rem - mid.astype(F32)).astype(BF16)
    return _dot(mask, hi) + _dot(mask, mid) + _dot(mask, lo)


def _sigmoid(x):
    return 1.0 / (1.0 + jnp.exp(-x))


def _silu(x):
    return x * _sigmoid(x)


def _softplus(x):
    return jnp.maximum(x, 0.0) + jnp.log1p(jnp.exp(-jnp.abs(x)))


def _log_sigmoid(x):
    return jnp.minimum(x, 0.0) - jnp.log1p(jnp.exp(-jnp.abs(x)))


def _rms(x):
    return x * lax.rsqrt(jnp.mean(x * x, axis=-1, keepdims=True) + EPS)


def _proj_kernel(x_ref, sh_ref, sc_ref, g_ref, *refs, n_out):
    h = (_rms(x_ref[...]) * g_ref[...] * (1.0 + sc_ref[...]) + sh_ref[...]).astype(BF16)
    for w_ref, o_ref in zip(refs[:n_out], refs[n_out:]):
        o_ref[...] = _dot(h, w_ref[...])


def _proj(xc, shift, scale, gain, weights, n_ctx_tiles):
    B, T, D = xc.shape
    mod_spec = pl.BlockSpec((None, None, 1, D), lambda b, i: (b, jnp.where(i < n_ctx_tiles, 1, 0), 0, 0))
    return pl.pallas_call(
        functools.partial(_proj_kernel, n_out=len(weights)),
        grid=(B, T // ROW_TILE),
        in_specs=[pl.BlockSpec((None, ROW_TILE, D), lambda b, i: (b, i, 0)), mod_spec, mod_spec,
                  pl.BlockSpec((1, D), lambda b, i: (0, 0))]
                 + [pl.BlockSpec(w.shape, lambda b, i: (0, 0)) for w in weights],
        out_specs=[pl.BlockSpec((None, ROW_TILE, w.shape[1]), lambda b, i: (b, i, 0)) for w in weights],
        out_shape=[jax.ShapeDtypeStruct((B, T, w.shape[1]), F32) for w in weights],
        compiler_params=pltpu.CompilerParams(dimension_semantics=("parallel", "parallel"),
                                             vmem_limit_bytes=VMEM_LIMIT),
        name="norm_mod_proj",
    )(xc, shift, scale, gain.reshape(1, D), *weights)


def _scan_chunk(step, reverse, n_ctx_chunks, n_chunks):
    if not reverse:
        return step
    return jnp.where(step < n_ctx_chunks, n_ctx_chunks - 1 - step, n_chunks - 1 - (step - n_ctx_chunks))


def _scan_specs(reverse, n_ctx_chunks, n_chunks, chunk):
    cidx = lambda s: _scan_chunk(s, reverse, n_ctx_chunks, n_chunks)
    per_tile = chunk // SUBLANES
    n_tiles = n_chunks * per_tile

    def main(width):
        return pl.BlockSpec((None, chunk, width), lambda b, s: (b, cidx(s), 0))

    def prev(width):
        return pl.BlockSpec((None, SUBLANES, width), lambda b, s: (b, jnp.maximum(cidx(s) * per_tile - 1, 0), 0))

    def nxt(width):
        return pl.BlockSpec((None, SUBLANES, width),
                            lambda b, s: (b, jnp.minimum((cidx(s) + 1) * per_tile, n_tiles - 1), 0))

    def const(shape):
        return pl.BlockSpec(shape, lambda b, s: (0,) * len(shape))

    return main, prev, nxt, const


def _segment_edges(chunk_idx, n_ctx_chunks, n_chunks):
    first = jnp.logical_or(chunk_idx == 0, chunk_idx == n_ctx_chunks)
    last = jnp.logical_or(chunk_idx == n_ctx_chunks - 1, chunk_idx == n_chunks - 1)
    return first, last


def _conv3(u, u_prev_tile, u_next_tile, w, bias, first, last):
    q = u.shape[0]
    rows = lax.broadcasted_iota(jnp.int32, u.shape, 0)
    before = jnp.where(first, 0.0, u_prev_tile[SUBLANES - 1:SUBLANES, :])
    after = jnp.where(last, 0.0, u_next_tile[0:1, :])
    u_m1 = jnp.where(rows == 0, before, pltpu.roll(u, 1, 0))
    u_p1 = jnp.where(rows == q - 1, after, pltpu.roll(u, q - 1, 0))
    return w[0:1, :] * u_m1 + w[1:2, :] * u + w[2:3, :] * u_p1 + bias


def _order_masks(q, reverse):
    r = lax.broadcasted_iota(jnp.int32, (q, q), 0)
    c = lax.broadcasted_iota(jnp.int32, (q, q), 1)
    return (r <= c) if reverse else (r >= c)


def _ssd_kernel(z_ref, x_ref, xp_ref, xn_ref, bc_ref, bcp_ref, bcn_ref, dt_ref, yf_ref,
                cwx_ref, cbx_ref, cwbc_ref, cbbc_ref, dtb_ref, nega_ref, dskip_ref, ng_ref,
                out_ref, st_ref, *, reverse, n_ctx_chunks, n_chunks):
    q = x_ref.shape[0]
    step = pl.program_id(1)
    cidx = _scan_chunk(step, reverse, n_ctx_chunks, n_chunks)
    first, last = _segment_edges(cidx, n_ctx_chunks, n_chunks)

    @pl.when(step == 0)
    def _():
        st_ref[...] = jnp.zeros_like(st_ref)

    xs = _silu(_conv3(x_ref[...], xp_ref[...], xn_ref[...], cwx_ref[...], cbx_ref[...], first, last))
    bc = _silu(_conv3(bc_ref[...], bcp_ref[...], bcn_ref[...], cwbc_ref[...], cbbc_ref[...], first, last))
    dt_all = _softplus(dt_ref[...] + dtb_ref[...])
    da_all = dt_all * nega_ref[...]
    mask = _order_masks(q, reverse)
    acs = _dot_mask(mask.astype(BF16), da_all)
    acs_t = acs.T
    dt_t = dt_all.T
    edge = 0 if reverse else q - 1
    gn = SSD_STATE
    ys = []
    for g in range(SSD_GROUPS):
        bg = bc[:, g * gn:(g + 1) * gn].astype(BF16)
        cg = bc[:, (SSD_GROUPS + g) * gn:(SSD_GROUPS + g + 1) * gn].astype(BF16)
        cb = _dot_nt(cg, bg)
        for hh in range(SSD_HEADS // SSD_GROUPS):
            h = g * (SSD_HEADS // SSD_GROUPS) + hh
            col = (SSD_HEADS if reverse else 0) + h
            a_col = acs[:, col:col + 1]
            a_row = acs_t[col:col + 1, :]
            a_end = a_row[:, edge:edge + 1]
            decay = jnp.where(mask, jnp.exp(jnp.minimum(a_col - a_row, 0.0)), 0.0)
            scores = (cb * decay * dt_t[col:col + 1, :]).astype(BF16)
            xh = xs[:, h * SSD_HEAD_DIM:(h + 1) * SSD_HEAD_DIM]
            st = st_ref[h]
            y = _dot(scores, xh.astype(BF16)) + _dot(cg, st.astype(BF16)) * jnp.exp(a_col)
            to_end = jnp.exp(a_end - a_col) * dt_all[:, col:col + 1]
            st_ref[h] = jnp.exp(a_end) * st + _dot_tn(bg, (xh * to_end).astype(BF16))
            ys.append(y)
    y = jnp.concatenate(ys, axis=1)
    if not reverse:
        out_ref[...] = y
    else:
        y = (y + yf_ref[...] + dskip_ref[...] * xs) * _silu(z_ref[...])
        out_ref[...] = (_rms(y) * ng_ref[...]).astype(out_ref.dtype)


def _ssd_pass(z, x, bc, dt, y_fwd, consts, reverse, n_ctx_chunks):
    B, T, _ = x.shape
    n_chunks = T // SCAN_CHUNK
    main, prev, nxt, const = _scan_specs(reverse, n_ctx_chunks, n_chunks, SCAN_CHUNK)
    w = BRANCH_W
    wbc = 2 * SSD_GROUPS * SSD_STATE
    in_specs = [main(w), main(w), prev(w), nxt(w), main(wbc), prev(wbc), nxt(wbc), main(LANES), main(w),
                const((3, w)), const((1, w)), const((3, wbc)), const((1, wbc)), const((1, LANES)),
                const((1, LANES)), const((1, w)), const((1, w))]
    return pl.pallas_call(
        functools.partial(_ssd_kernel, reverse=reverse, n_ctx_chunks=n_ctx_chunks, n_chunks=n_chunks),
        grid=(B, n_chunks),
        in_specs=in_specs,
        out_specs=main(w),
        out_shape=jax.ShapeDtypeStruct((B, T, w), BF16 if reverse else F32),
        scratch_shapes=[pltpu.VMEM((SSD_HEADS, SSD_STATE, SSD_HEAD_DIM), F32)],
        compiler_params=pltpu.CompilerParams(dimension_semantics=("parallel", "arbitrary"),
                                             vmem_limit_bytes=VMEM_LIMIT),
        name="ssd_bwd_finish" if reverse else "ssd_fwd",
    )(z, x, x, x, bc, bc, bc, dt, y_fwd, *consts)


def _ssd_branch(z, x, bc, dt, conv_w, conv_b, dt_bias, a_log, d_skip, norm_g, n_ctx_chunks):
    w = BRANCH_W
    pad = LANES - 2 * SSD_HEADS
    dtb = jnp.pad(dt_bias.astype(F32).reshape(1, -1), ((0, 0), (0, pad)))
    nega = jnp.pad(-jnp.exp(a_log.astype(F32)).reshape(1, -1), ((0, 0), (0, pad)))
    consts = (conv_w[:, :w], conv_b[:w].reshape(1, w), conv_w[:, w:], conv_b[w:].reshape(1, -1), dtb, nega,
              jnp.repeat(d_skip.astype(F32), SSD_HEAD_DIM).reshape(1, w), norm_g.reshape(1, w))
    y_f = _ssd_pass(z, x, bc, dt, x, consts, False, n_ctx_chunks)
    return _ssd_pass(z, x, bc, dt, y_f, consts, True, n_ctx_chunks)


def _mlstm_kernel(qk_ref, qkp_ref, qkn_ref, v_ref, o_ref, gt_ref, hf_ref, cw_ref, cb_ref, gb_ref, ng_ref,
                  out_ref, c_ref, m_ref, *, reverse, n_ctx_chunks, n_chunks):
    q = qk_ref.shape[0]
    dh = ML_HEAD_DIM
    step = pl.program_id(1)
    cidx = _scan_chunk(step, reverse, n_ctx_chunks, n_chunks)
    first, last = _segment_edges(cidx, n_ctx_chunks, n_chunks)

    @pl.when(step == 0)
    def _():
        c_ref[...] = jnp.zeros_like(c_ref)
        m_ref[...] = jnp.zeros_like(m_ref)

    qk = _silu(_conv3(qk_ref[...], qkp_ref[...], qkn_ref[...], cw_ref[...], cb_ref[...], first, last))
    gates = gt_ref[...] + gb_ref[...]
    logf_all = _log_sigmoid(gates)
    mask = _order_masks(q, reverse)
    bcum = _dot_mask(mask.astype(BF16), logf_all)
    bcum_t = bcum.T
    gates_t = gates.T
    edge = 0 if reverse else q - 1
    lane = lax.broadcasted_iota(jnp.int32, (q, dh), 1)
    ones_col = jnp.where(lane == 0, 1.0, 0.0).astype(BF16)
    hs = []
    for h in range(ML_HEADS):
        li = (ML_HEADS if reverse else 0) + h
        lf = 2 * ML_HEADS + li
        qh = qk[:, h * dh:(h + 1) * dh].astype(BF16)
        kh = qk[:, BRANCH_W + h * dh:BRANCH_W + (h + 1) * dh] * (dh ** -0.5)
        v_aug = jnp.concatenate([v_ref[:, h * dh:(h + 1) * dh].astype(BF16), ones_col], axis=1)
        b_col = bcum[:, lf:lf + 1]
        b_row = bcum_t[lf:lf + 1, :]
        i_col = gates[:, li:li + 1]
        i_row = gates_t[li:li + 1, :]
        b_tot = b_row[:, edge:edge + 1]
        m_prev = m_ref[h:h + 1, 0:1]
        c_prev = c_ref[h]
        dmat = jnp.where(mask, b_col - b_row + i_row, MASK_NEG)
        inter = b_col + m_prev
        m_t = jnp.maximum(inter, jnp.max(dmat, axis=1, keepdims=True))
        wgt = jnp.where(mask, jnp.exp(jnp.minimum(dmat - m_t, 0.0)), 0.0) * _dot_nt(qh, kh.astype(BF16))
        w_int = jnp.exp(inter - m_t)
        cross = _dot(qh, c_prev.astype(BF16))
        num = _dot(wgt.astype(BF16), v_aug[:, :dh]) + w_int * cross[:, :dh]
        den = jnp.sum(wgt, axis=1, keepdims=True) + w_int * cross[:, dh:dh + 1]
        hs.append(num / jnp.maximum(jnp.abs(den), jnp.exp(-m_t)))
        a = b_tot - b_col + i_col
        m_new = jnp.maximum(b_tot + m_prev, jnp.max(a, axis=0, keepdims=True))
        kw = (kh * jnp.exp(a - m_new)).astype(BF16)
        c_ref[h] = jnp.exp(b_tot + m_prev - m_new) * c_prev + _dot_tn(kw, v_aug)
        m_ref[h:h + 1, :] = jnp.broadcast_to(m_new, (1, LANES))
    if not reverse:
        out_ref[...] = jnp.concatenate(hs, axis=1)
    else:
        hf = hf_ref[...]
        outs = [_rms(hs[h] + hf[:, h * dh:(h + 1) * dh]) for h in range(ML_HEADS)]
        out_ref[...] = (_sigmoid(o_ref[...]) * jnp.concatenate(outs, axis=1) * ng_ref[...]).astype(out_ref.dtype)


def _mlstm_pass(qk, v, o, gates, h_fwd, consts, reverse, n_ctx_chunks):
    B, T, _ = v.shape
    n_chunks = T // SCAN_CHUNK
    main, prev, nxt, const = _scan_specs(reverse, n_ctx_chunks, n_chunks, SCAN_CHUNK)
    w = BRANCH_W
    in_specs = [main(2 * w), prev(2 * w), nxt(2 * w), main(w), main(w), main(LANES), main(w),
                const((3, 2 * w)), const((1, 2 * w)), const((1, LANES)), const((1, w))]
    return pl.pallas_call(
        functools.partial(_mlstm_kernel, reverse=reverse, n_ctx_chunks=n_ctx_chunks, n_chunks=n_chunks),
        grid=(B, n_chunks),
        in_specs=in_specs,
        out_specs=main(w),
        out_shape=jax.ShapeDtypeStruct((B, T, w), BF16 if reverse else F32),
        scratch_shapes=[pltpu.VMEM((ML_HEADS, ML_HEAD_DIM, 2 * ML_HEAD_DIM), F32),
                        pltpu.VMEM((SUBLANES, LANES), F32)],
        compiler_params=pltpu.CompilerParams(dimension_semantics=("parallel", "arbitrary"),
                                             vmem_limit_bytes=VMEM_LIMIT),
        name="mlstm_bwd_finish" if reverse else "mlstm_fwd",
    )(qk, qk, qk, v, o, gates, h_fwd, *consts)


def _mlstm_branch(qk, v, o, gates, conv_w, conv_b, i_bias, f_bias, norm_g, n_ctx_chunks):
    w = BRANCH_W
    gb = jnp.concatenate([i_bias.astype(F32).reshape(-1), f_bias.astype(F32).reshape(-1)])
    gb = jnp.pad(gb.reshape(1, -1), ((0, 0), (0, LANES - 4 * ML_HEADS)))
    consts = (conv_w, conv_b.reshape(1, 2 * w), gb, norm_g.reshape(1, w))
    h_f = _mlstm_pass(qk, v, o, gates, v, consts, False, n_ctx_chunks)
    return _mlstm_pass(qk, v, o, gates, h_f, consts, True, n_ctx_chunks)


def _hgrn_level_matrix(q, block, reverse):
    t = lax.broadcasted_iota(jnp.int32, (q, q), 0)
    r = lax.broadcasted_iota(jnp.int32, (q, q), 1)
    blk = t // block
    start = blk * block
    end = start + block - 1
    odd = (blk % 2) == 1
    if not reverse:
        lo = jnp.where(odd, start, t + 1)
        hi = jnp.where(odd, t, end)
    else:
        lo = jnp.where(odd, start, t)
        hi = jnp.where(odd, t - 1, end)
    return jnp.where((r >= lo) & (r <= hi), 1.0, 0.0)


def _hgrn_select_matrix(q, reverse):
    mats = [_order_masks(q, reverse).astype(F32)]
    block = HG_LEAF
    while block < q:
        mats.append(_hgrn_level_matrix(q, block, reverse))
        block *= 2
    return jnp.concatenate(mats, axis=0).astype(BF16)


def _hgrn_kernel(q_ref, v_ref, f_ref, g_ref, of_ref, lb_ref, ng_ref, sel_ref, out_ref,
                 st_ref, b_ref, qs_ref, ks_ref, ol_ref, *, reverse):
    q = q_ref.shape[0]
    dh = HG_HEAD_DIM
    step = pl.program_id(1)

    @pl.when(step == 0)
    def _():
        st_ref[...] = jnp.zeros_like(st_ref)

    lb = lb_ref[...]
    pre = f_ref[...]
    log_lb = jnp.log(jnp.maximum(lb, LB_FLOOR))
    log_ub = jnp.log1p(-lb)
    e_pre = jnp.exp(-jnp.abs(pre))
    lo = log_ub + jnp.minimum(pre, 0.0) - jnp.log1p(e_pre)
    logf = jnp.maximum(log_lb, lo) + jnp.log1p(jnp.exp(-jnp.abs(log_lb - lo)))
    key = (1.0 - lb) * (jnp.where(pre >= 0.0, e_pre, 1.0) / (1.0 + e_pre))
    qv = _silu(q_ref[...])
    sums = _dot_mask(sel_ref[...], logf)
    bq = sums[:q]
    b_ref[...] = bq
    qs_ref[...] = qv
    ks_ref[...] = key

    def leaf_body(l, carry):
        rows = pl.ds(pl.multiple_of(l * HG_LEAF, HG_LEAF), HG_LEAF)
        bl = b_ref[rows, :]
        ql = qs_ref[rows, :]
        kl = ks_ref[rows, :]
        vl = v_ref[rows, :]
        tt = lax.broadcasted_iota(jnp.int32, (HG_LEAF, dh), 0)
        acc = [jnp.zeros((HG_LEAF, dh), F32) for _ in range(HG_HEADS)]
        for s in range(HG_LEAF):
            ok = (tt <= s) if reverse else (tt >= s)
            for h in range(HG_HEADS):
                sl = slice(h * dh, (h + 1) * dh)
                diff = bl[:, sl] - bl[s:s + 1, sl]
                e = jnp.exp(jnp.where(ok, diff, MASK_NEG))
                att = jnp.sum(ql[:, sl] * e * kl[s:s + 1, sl], axis=1, keepdims=True)
                acc[h] = acc[h] + att * vl[s:s + 1, sl]
        ol_ref[rows, :] = jnp.concatenate(acc, axis=1)
        return carry

    lax.fori_loop(0, q // HG_LEAF, leaf_body, 0)

    t_idx = lax.broadcasted_iota(jnp.int32, (q, q), 0)
    s_idx = lax.broadcasted_iota(jnp.int32, (q, q), 1)
    row_t = lax.broadcasted_iota(jnp.int32, (q, dh), 0)
    levels = []
    block = HG_LEAF
    while block < q:
        fac = jnp.exp(sums[(len(levels) + 1) * q:(len(levels) + 2) * q])
        tb, sb = t_idx // block, s_idx // block
        if not reverse:
            pair = ((tb % 2) == 1) & (sb == tb - 1)
            is_query = ((row_t // block) % 2) == 1
        else:
            pair = ((tb % 2) == 0) & (sb == tb + 1)
            is_query = ((row_t // block) % 2) == 0
        levels.append((fac, pair, is_query))
        block *= 2
    edge = 0 if reverse else q - 1
    b_end = bq[edge:edge + 1, :]
    q_in = qv * jnp.exp(bq)
    k_out = key * jnp.exp(b_end - bq)
    outs = []
    for h in range(HG_HEADS):
        sl = slice(h * dh, (h + 1) * dh)
        att = jnp.zeros((q, q), F32)
        for fac, pair, is_query in levels:
            qt = jnp.where(is_query, qv[:, sl] * fac[:, sl], 0.0).astype(BF16)
            kt = jnp.where(is_query, 0.0, key[:, sl] * fac[:, sl]).astype(BF16)
            att = att + jnp.where(pair, _dot_nt(qt, kt), 0.0)
        vh = v_ref[:, sl].astype(BF16)
        st = st_ref[h]
        o = ol_ref[:, sl] + _dot(att.astype(BF16), vh) + _dot_nt(q_in[:, sl].astype(BF16), st.astype(BF16))
        st_ref[h] = jnp.exp(b_end[:, sl]) * st + _dot_tn(vh, k_out[:, sl].astype(BF16))
        outs.append(o)
    if not reverse:
        out_ref[...] = jnp.concatenate(outs, axis=1)
    else:
        of = of_ref[...]
        fin = [_rms(outs[h] + of[:, h * dh:(h + 1) * dh]) for h in range(HG_HEADS)]
        out_ref[...] = (jnp.concatenate(fin, axis=1) * ng_ref[...] * _sigmoid(g_ref[...])).astype(out_ref.dtype)


def _hgrn_pass(qr, v, f, g, o_fwd, lb, norm_g, reverse, n_ctx_chunks):
    B, T, w = v.shape
    n_chunks = T // SCAN_CHUNK
    main, _, _, const = _scan_specs(reverse, n_ctx_chunks, n_chunks, SCAN_CHUNK)
    sel = _hgrn_select_matrix(SCAN_CHUNK, reverse)
    return pl.pallas_call(
        functools.partial(_hgrn_kernel, reverse=reverse),
        grid=(B, n_chunks),
        in_specs=[main(w), main(w), main(w), main(w), main(w), const((1, w)), const((1, w)), const(sel.shape)],
        out_specs=main(w),
        out_shape=jax.ShapeDtypeStruct((B, T, w), BF16 if reverse else F32),
        scratch_shapes=[pltpu.VMEM((HG_HEADS, HG_HEAD_DIM, HG_HEAD_DIM), F32)]
                       + [pltpu.VMEM((SCAN_CHUNK, w), F32) for _ in range(4)],
        compiler_params=pltpu.CompilerParams(dimension_semantics=("parallel", "arbitrary"),
                                             vmem_limit_bytes=VMEM_LIMIT),
        name="hgrn_bwd_finish" if reverse else "hgrn_fwd",
    )(qr, v, f, g, o_fwd, lb.reshape(1, w), norm_g.reshape(1, w), sel)


def _hgrn_branch(qr, v, f_fwd, f_bwd, g, lb, norm_g, n_ctx_chunks):
    o_f = _hgrn_pass(qr, v, f_fwd, g, v, lb, norm_g, False, n_ctx_chunks)
    return _hgrn_pass(qr, v, f_bwd, g, o_f, lb, norm_g, True, n_ctx_chunks)


def _hyena_filter_hidden(n, w1, b1, w2, b2, freq):
    pos = jnp.arange(n, dtype=F32)
    t = pos / max(n - 1, 1)
    bands = jnp.arange(1, HY_BANDS + 1, dtype=F32)
    ang = (2.0 * math.pi / n) * pos[:, None] * bands[None, :]
    feats = jnp.concatenate([t[:, None], jnp.cos(ang), jnp.sin(ang)], axis=-1)
    hid = jnp.sin(freq[0] * (feats @ w1 + b1))
    return jnp.sin(freq[1] * (hid @ w2 + b2)), t


def _hyena_deltas():
    return jnp.abs(jnp.linspace(math.log(HY_DECAY_TARGET) / HY_LONG_PCT,
                                math.log(HY_DECAY_TARGET) / HY_SHORT_PCT, HY_WIDTH, dtype=F32))


def _hyena_filter_time(n, w1, b1, w2, b2, w3, freq):
    hid, t = _hyena_filter_hidden(n, w1, b1, w2, b2, freq)
    h = (hid @ w3).reshape(n, HY_ORDER, 2, HY_WIDTH)
    h = h * jnp.exp(-t[:, None] * _hyena_deltas()[None, :])[:, None, None, :]
    two_sided = jnp.concatenate([h[:, :, 0], jnp.zeros((1, HY_ORDER, HY_WIDTH), F32),
                                 jnp.flip(h[1:, :, 1], axis=0)], axis=0)
    return two_sided / (jnp.sum(jnp.abs(two_sided), axis=0, keepdims=True) + EPS)


def _filter_kernel(hid_ref, t_ref, wh_ref, wl_ref, delta_ref, o_ref):
    hid = hid_ref[...]
    hid_hi, hid_lo = _split_bf16(hid)
    h = _dot(hid_hi, wh_ref[...]) + _dot(hid_lo, wh_ref[...]) + _dot(hid_hi, wl_ref[...])
    decay = jnp.exp(-t_ref[...] * delta_ref[...])
    wd = HY_WIDTH
    lag0 = jnp.logical_and(pl.program_id(0) == 0, lax.broadcasted_iota(jnp.int32, decay.shape, 0) == 0)
    for g in range(2 * HY_ORDER):
        val = h[:, g * wd:(g + 1) * wd] * decay
        o_ref[g] = jnp.where(lag0, 0.0, val) if g % 2 == 1 else val


def _hyena_filter_sides(n, w1, b1, w2, b2, w3, freq):
    hid, t = _hyena_filter_hidden(n, w1, b1, w2, b2, freq)
    wh, wl = _split_bf16(w3)
    g4, wd = 2 * HY_ORDER, HY_WIDTH
    sides = pl.pallas_call(
        _filter_kernel,
        grid=(n // ROW_TILE,),
        in_specs=[pl.BlockSpec((ROW_TILE, hid.shape[1]), lambda i: (i, 0)), pl.BlockSpec((ROW_TILE, 1), lambda i: (i, 0)),
                  pl.BlockSpec(wh.shape, lambda i: (0, 0)), pl.BlockSpec(wl.shape, lambda i: (0, 0)),
                  pl.BlockSpec((1, wd), lambda i: (0, 0))],
        out_specs=pl.BlockSpec((g4, ROW_TILE, wd), lambda i: (0, i, 0)),
        out_shape=jax.ShapeDtypeStruct((g4, n, wd), F32),
        compiler_params=pltpu.CompilerParams(dimension_semantics=("parallel",)),
        name="hyena_filter_sides",
    )(hid, t.reshape(n, 1), wh, wl, _hyena_deltas().reshape(1, wd))
    norm = jnp.sum(jnp.abs(sides), axis=1).reshape(HY_ORDER, 2, wd).sum(axis=1)
    return sides, 1.0 / (norm + EPS)


def _hyena_conv_kernel(p_ref, pp_ref, pn_ref, w_ref, b_ref, v_ref, x1_ref, x2_ref, *, n_ctx_tiles, n_tiles):
    i = pl.program_id(1)
    first, last = _segment_edges(i, n_ctx_tiles, n_tiles)
    u = _conv3(p_ref[...], pp_ref[...], pn_ref[...], w_ref[...], b_ref[...], first, last)
    wd = HY_WIDTH
    v_ref[...] = u[:, :wd]
    x1_ref[...] = u[:, wd:2 * wd]
    x2_ref[...] = u[:, 2 * wd:]


def _hyena_conv(p, conv_w, conv_b, n_ctx_tiles):
    B, T, C = p.shape
    n_tiles = T // ROW_TILE
    main, prev, nxt, const = _scan_specs(False, n_ctx_tiles, n_tiles, ROW_TILE)
    return pl.pallas_call(
        functools.partial(_hyena_conv_kernel, n_ctx_tiles=n_ctx_tiles, n_tiles=n_tiles),
        grid=(B, n_tiles),
        in_specs=[main(C), prev(C), nxt(C), const((3, C)), const((1, C))],
        out_specs=[main(HY_WIDTH)] * 3,
        out_shape=[jax.ShapeDtypeStruct((B, T, HY_WIDTH), F32)] * 3,
        compiler_params=pltpu.CompilerParams(dimension_semantics=("parallel", "parallel")),
        name="hyena_short_conv",
    )(p, p, p, conv_w, conv_b.reshape(1, C))


DFT_N2 = 256
DFT_ROW_TILE = 16


def _split_bf16(x):
    hi = x.astype(BF16)
    return hi, (x - hi.astype(F32)).astype(BF16)


def _dot_split(m_hi, m_lo, x):
    x_hi, x_lo = _split_bf16(x)
    return _dot(m_hi, x_hi) + _dot(m_hi, x_lo) + _dot(m_lo, x_hi)


def _dft_tables(n):
    L = 2 * n
    n1_full = L // DFT_N2
    nh = n1_full // 2
    nk = nh + 1
    nkp = -(-nk // SUBLANES) * SUBLANES
    k1 = jnp.arange(nkp, dtype=jnp.int32)
    valid = (k1 < nk)[:, None]

    def stage1(rows):
        n1 = jnp.arange(rows, dtype=jnp.int32)
        ang = (2.0 * math.pi / n1_full) * ((k1[:, None] * n1[None, :]) % n1_full).astype(F32)
        return jnp.concatenate([jnp.where(valid, jnp.cos(ang), 0.0), jnp.where(valid, -jnp.sin(ang), 0.0)], axis=0)

    n1 = jnp.arange(nh, dtype=jnp.int32)
    ang = (2.0 * math.pi / n1_full) * ((n1[:, None] * k1[None, :]) % n1_full).astype(F32)
    ck = jnp.where((k1 == 0) | (k1 == nh), 1.0, 2.0) * jnp.where(k1 < nk, 1.0, 0.0) / L
    stage3 = jnp.concatenate([jnp.cos(ang) * ck[None, :], -jnp.sin(ang) * ck[None, :]], axis=1)
    n2 = jnp.arange(DFT_N2, dtype=jnp.int32)
    idx = (n2[None, :, None] * (k1[:, None, None] + n1_full * n2[None, None, :])) % L
    ang2 = (2.0 * math.pi / L) * idx.astype(F32)
    gr, gi = jnp.cos(ang2), -jnp.sin(ang2)
    grt, git = jnp.swapaxes(gr, 1, 2), jnp.swapaxes(gi, 1, 2)
    m_fwd = jnp.concatenate([jnp.concatenate([grt, -git], axis=2), jnp.concatenate([git, grt], axis=2)], axis=1)
    m_inv = jnp.swapaxes(m_fwd, 1, 2)
    return dict(nh=nh, nk=nk, nkp=nkp, f1=_split_bf16(stage1(nh)), f3=_split_bf16(stage3),
                m_fwd=_split_bf16(m_fwd), m_inv=_split_bf16(m_inv))


def _dft_in_kernel(u_ref, fh_ref, fl_ref, o_ref):
    nkp = o_ref.shape[1]
    fh, fl = fh_ref[...], fl_ref[...]
    for i in range(u_ref.shape[1]):
        r = _dot_split(fh, fl, u_ref[:, i, :])
        o_ref[0, :, i, :] = r[:nkp]
        o_ref[1, :, i, :] = r[nkp:]


def _dft_in(u, f_hi, f_lo):
    G, rows, n2, wd = u.shape
    nkp = f_hi.shape[0] // 2
    t2 = DFT_ROW_TILE
    return pl.pallas_call(
        _dft_in_kernel,
        grid=(G, n2 // t2),
        in_specs=[pl.BlockSpec((None, rows, t2, wd), lambda g, j: (g, 0, j, 0)),
                  pl.BlockSpec(f_hi.shape, lambda g, j: (0, 0)), pl.BlockSpec(f_lo.shape, lambda g, j: (0, 0))],
        out_specs=pl.BlockSpec((None, 2, nkp, t2, wd), lambda g, j: (g, 0, 0, j, 0)),
        out_shape=jax.ShapeDtypeStruct((G, 2, nkp, n2, wd), F32),
        compiler_params=pltpu.CompilerParams(dimension_semantics=("parallel", "parallel"),
                                             vmem_limit_bytes=VMEM_LIMIT),
        name="hyena_dft_rows",
    )(u, f_hi, f_lo)


def _dft_mid_kernel(a_ref, hf_ref, hb_ref, sc_ref, mfh_ref, mfl_ref, mih_ref, mil_ref, o_ref, *, nk, spectrum_only):
    k1 = pl.program_id(0)
    half = DFT_N2

    @pl.when(k1 < nk)
    def _():
        x = jnp.concatenate([a_ref[0], a_ref[1]], axis=0)
        u = _dot_split(mfh_ref[...], mfl_ref[...], x)
        if spectrum_only:
            y = u
        else:
            ur, ui = u[:half], u[half:]
            hr = (hf_ref[0] + hb_ref[0]) * sc_ref[...]
            hi = (hf_ref[1] - hb_ref[1]) * sc_ref[...]
            v = jnp.concatenate([ur * hr - ui * hi, ur * hi + ui * hr], axis=0)
            y = _dot_split(mih_ref[...], mil_ref[...], v)
        o_ref[0] = y[:half]
        o_ref[1] = y[half:]

    @pl.when(k1 >= nk)
    def _():
        o_ref[...] = jnp.zeros_like(o_ref)


def _dft_mid(a, spec, order, scale_row, tables, spectrum_only):
    G, _, nkp, n2, wd = a.shape
    slab = lambda g_of: pl.BlockSpec((None, 2, None, n2, wd), g_of)
    mat = pl.BlockSpec((None, 2 * n2, 2 * n2), lambda k, g: (k, 0, 0))
    return pl.pallas_call(
        functools.partial(_dft_mid_kernel, nk=tables["nk"], spectrum_only=spectrum_only),
        grid=(nkp, G),
        in_specs=[slab(lambda k, g: (g, 0, k, 0, 0)), slab(lambda k, g: (2 * order, 0, k, 0, 0)),
                  slab(lambda k, g: (2 * order + 1, 0, k, 0, 0)), pl.BlockSpec((1, wd), lambda k, g: (0, 0)),
                  mat, mat, mat, mat],
        out_specs=slab(lambda k, g: (g, 0, k, 0, 0)),
        out_shape=jax.ShapeDtypeStruct(a.shape, F32),
        compiler_params=pltpu.CompilerParams(dimension_semantics=("parallel", "parallel"),
                                             vmem_limit_bytes=VMEM_LIMIT),
        name="hyena_spectrum" if spectrum_only else "hyena_dft_mid",
    )(a, spec, spec, scale_row, *tables["m_fwd"], *tables["m_inv"])


def _dft_out_kernel(b_ref, u_ref, x_ref, skip_ref, fh_ref, fl_ref, o_ref):
    fh, fl, skip = fh_ref[...], fl_ref[...], skip_ref[...]
    for i in range(u_ref.shape[1]):
        bb = jnp.concatenate([b_ref[0, :, i, :], b_ref[1, :, i, :]], axis=0)
        y = _dot_split(fh, fl, bb)
        o_ref[:, i, :] = x_ref[:, i, :] * (y + u_ref[:, i, :] * skip)


def _dft_out(bq, u, xg, skip_row, f_hi, f_lo):
    G, rows, n2, wd = u.shape
    nkp = bq.shape[2]
    t2 = DFT_ROW_TILE
    tile = pl.BlockSpec((None, rows, t2, wd), lambda g, j: (g, 0, j, 0))
    return pl.pallas_call(
        _dft_out_kernel,
        grid=(G, n2 // t2),
        in_specs=[pl.BlockSpec((None, 2, nkp, t2, wd), lambda g, j: (g, 0, 0, j, 0)), tile, tile,
                  pl.BlockSpec((1, wd), lambda g, j: (0, 0)),
                  pl.BlockSpec(f_hi.shape, lambda g, j: (0, 0)), pl.BlockSpec(f_lo.shape, lambda g, j: (0, 0))],
        out_specs=tile,
        out_shape=jax.ShapeDtypeStruct(u.shape, F32),
        compiler_params=pltpu.CompilerParams(dimension_semantics=("parallel", "parallel"),
                                             vmem_limit_bytes=VMEM_LIMIT),
        name="hyena_dft_rows_inverse",
    )(bq, u, xg, skip_row, f_hi, f_lo)


def _hyena_latent(v, x1, x2, sides, scale, skip):
    B, n, wd = v.shape
    tb = _dft_tables(n)
    view = lambda t: t.reshape(t.shape[0], tb["nh"], DFT_N2, wd)
    spec = _dft_in(view(sides), *tb["f1"])
    spec = _dft_mid(spec, spec, 0, scale[0:1], tb, True)
    z = view(v)
    for o, xg in enumerate((x1, x2)):
        a = _dft_in(z, *tb["f1"])
        bq = _dft_mid(a, spec, o, scale[o:o + 1], tb, False)
        z = _dft_out(bq, z, view(xg), skip[o].astype(F32).reshape(1, wd), *tb["f3"])
    return z.reshape(B, n, wd)


def _hyena_context(v, x1, x2, filt, skip):
    n = v.shape[1]
    spec = jnp.fft.rfft(filt, axis=0)
    z = v
    for o, xg in enumerate((x1, x2)):
        zf = jnp.fft.rfft(z, n=2 * n, axis=1)
        y = jnp.fft.irfft(zf * spec[:, o], n=2 * n, axis=1)[:, :n]
        z = xg * (y + z * skip[o].astype(F32))
    return z


def _merge_kernel(ya_ref, yb_ref, yc_ref, yd_ref, gate_ref, x_ref, g1_ref, sh_ref, sc_ref, ng_ref,
                  wb_ref, wo_ref, wr_ref, x1_ref, h2_ref, aff_ref):
    d = x_ref.shape[1]
    merged = None
    for i, y_ref in enumerate((ya_ref, yb_ref, yc_ref, yd_ref)):
        term = _sigmoid(gate_ref[:, i * d:(i + 1) * d]) * _dot(y_ref[...], wb_ref[i])
        merged = term if merged is None else merged + term
    x1 = x_ref[...] + g1_ref[...] * _dot(merged.astype(BF16), wo_ref[...])
    x1_ref[...] = x1
    h2 = _rms(x1) * ng_ref[...] * (1.0 + sc_ref[...]) + sh_ref[...]
    h2_ref[...] = h2.astype(h2_ref.dtype)
    h2_hi, h2_lo = _split_bf16(h2)
    logits = _dot(h2_hi, wr_ref[0]) + _dot(h2_lo, wr_ref[0]) + _dot(h2_hi, wr_ref[1])
    lane = lax.broadcasted_iota(jnp.int32, logits.shape, 1)
    logits = jnp.where(lane < N_EXPERTS, logits, MASK_NEG)
    e = jnp.exp(logits - jnp.max(logits, axis=1, keepdims=True))
    aff_ref[...] = e / jnp.sum(e, axis=1, keepdims=True)


def _merge(ys, gates, xc, g1, shift, scale, gain, wb, wo, wr, n_ctx_tiles):
    B, T, D = xc.shape
    w = BRANCH_W
    tile = lambda width: pl.BlockSpec((None, ROW_TILE, width), lambda b, i: (b, i, 0))
    mod_spec = pl.BlockSpec((None, None, 1, D), lambda b, i: (b, jnp.where(i < n_ctx_tiles, 1, 0), 0, 0))
    const = lambda shape: pl.BlockSpec(shape, lambda b, i: (0,) * len(shape))
    return pl.pallas_call(
        _merge_kernel,
        grid=(B, T // ROW_TILE),
        in_specs=[tile(w), tile(w), tile(w), tile(w), tile(N_BRANCHES * D), tile(D), mod_spec, mod_spec, mod_spec,
                  const((1, D)), const(wb.shape), const(wo.shape), const(wr.shape)],
        out_specs=[tile(D), tile(D), tile(LANES)],
        out_shape=[jax.ShapeDtypeStruct((B, T, D), F32), jax.ShapeDtypeStruct((B, T, D), BF16),
                   jax.ShapeDtypeStruct((B, T, LANES), F32)],
        compiler_params=pltpu.CompilerParams(dimension_semantics=("parallel", "parallel"),
                                             vmem_limit_bytes=VMEM_LIMIT),
        name="merge_out_router",
    )(*ys, gates, xc, g1, shift, scale, gain.reshape(1, D), wb, wo, wr)


def _expert_kernel(x_ref, s_ref, wg_ref, wu_ref, wd_ref, o_ref):
    x = x_ref[...]
    hid = _silu(_dot(x, wg_ref[...])) * _dot(x, wu_ref[...])
    o_ref[...] = _dot(hid.astype(BF16), wd_ref[...]) * s_ref[...]


def _experts(xg, score, wg, wu, wd):
    E, R, D = xg.shape
    F = wg.shape[2]
    tm = min(R, 512)
    return pl.pallas_call(
        _expert_kernel,
        grid=(E, R // tm),
        in_specs=[pl.BlockSpec((None, tm, D), lambda e, i: (e, i, 0)),
                  pl.BlockSpec((None, tm, 1), lambda e, i: (e, i, 0)),
                  pl.BlockSpec((None, D, F), lambda e, i: (e, 0, 0)),
                  pl.BlockSpec((None, D, F), lambda e, i: (e, 0, 0)),
                  pl.BlockSpec((None, F, D), lambda e, i: (e, 0, 0))],
        out_specs=pl.BlockSpec((None, tm, D), lambda e, i: (e, i, 0)),
        out_shape=jax.ShapeDtypeStruct((E, R, D), F32),
        compiler_params=pltpu.CompilerParams(dimension_semantics=("parallel", "arbitrary"),
                                             vmem_limit_bytes=VMEM_LIMIT),
        name="expert_swiglu",
    )(xg, score, wg, wu, wd)


def _expert_choice(h2, aff, seg_start, seg_len, wg, wu, wd):
    B, T, D = h2.shape
    cap = EC_CAPACITY * seg_len // N_EXPERTS
    a = aff[:, seg_start:seg_start + seg_len, :N_EXPERTS]
    score, idx = lax.top_k(jnp.swapaxes(a, 1, 2), cap)
    rows = idx + seg_start + (jnp.arange(B, dtype=idx.dtype) * T)[:, None, None]
    rows = jnp.swapaxes(rows, 0, 1).reshape(N_EXPERTS, B * cap)
    score = jnp.swapaxes(score, 0, 1).reshape(N_EXPERTS, B * cap, 1)
    xg = jnp.take(h2.reshape(B * T, D), rows, axis=0)
    ye = _experts(xg, score, wg, wu, wd)
    return rows.reshape(-1), ye.reshape(-1, D)


def _residual_kernel(x_ref, m_ref, g2_ref, gain_ref, o_ref, *, final):
    x = x_ref[...] + g2_ref[...] * m_ref[...]
    o_ref[...] = _rms(x) * gain_ref[...] if final else x


def _residual(x1, moe, g2, gain, final, n_ctx_tiles):
    B, T, D = x1.shape
    tile = pl.BlockSpec((None, ROW_TILE, D), lambda b, i: (b, i, 0))
    mod_spec = pl.BlockSpec((None, None, 1, D), lambda b, i: (b, jnp.where(i < n_ctx_tiles, 1, 0), 0, 0))
    return pl.pallas_call(
        functools.partial(_residual_kernel, final=final),
        grid=(B, T // ROW_TILE),
        in_specs=[tile, tile, mod_spec, pl.BlockSpec((1, D), lambda b, i: (0, 0))],
        out_specs=tile,
        out_shape=jax.ShapeDtypeStruct((B, T, D), F32),
        compiler_params=pltpu.CompilerParams(dimension_semantics=("parallel", "parallel")),
        name="moe_residual_final_norm" if final else "moe_residual",
    )(x1, moe, g2, gain.reshape(1, D))


def _to_col_major(t):
    b, n = t.shape[:2]
    rows = n // GRID_W
    return t.reshape((b, rows, GRID_W) + t.shape[2:]).swapaxes(1, 2).reshape(t.shape)


def _to_row_major(t):
    b, n = t.shape[:2]
    rows = n // GRID_W
    return t.reshape((b, GRID_W, rows) + t.shape[2:]).swapaxes(1, 2).reshape(t.shape)


def _pad_cols(w, width):
    return jnp.pad(w, ((0, 0), (0, width - w.shape[1])))


def kernel(x, c, ctx, c_ctx, w_ada, b_ada, norm1_g, norm2_g, w_in, ssd_conv_w, ssd_conv_b, ssd_dt_bias, ssd_a_log, ssd_d, ssd_norm_g, hy_conv_w, hy_conv_b, hy_w1, hy_b1, hy_w2, hy_b2, hy_w3, hy_freq, hy_skip, ml_conv_w, ml_conv_b, ml_i_bias, ml_f_bias, ml_norm_g, hg_lb_logits, hg_norm_g, w_branch, w_out, w_router, w_gate, w_up, w_down, final_g):
    B, n, D = x.shape
    n_ctx = ctx.shape[1]
    T = n_ctx + n
    assert n_ctx % ROW_TILE == 0 and n % ROW_TILE == 0 and n % GRID_W == 0
    n_ctx_tiles = n_ctx // ROW_TILE
    n_ctx_chunks = n_ctx // SCAN_CHUNK
    w = BRANCH_W

    p_lb = jax.nn.softmax(hg_lb_logits.astype(F32), axis=0)
    lower_bounds = jnp.maximum(jnp.cumsum(p_lb, axis=0) - p_lb[0], 0.0)
    xc = jnp.concatenate([ctx, x], axis=1)
    depth = w_in.shape[0]
    for l in range(depth):
        last_layer = l == depth - 1
        mod = jax.nn.silu(c) @ w_ada[l] + b_ada[l]
        mod_c = jax.nn.silu(c_ctx) @ w_ada[l] + b_ada[l]
        mods = jnp.stack([mod, jnp.broadcast_to(mod_c, mod.shape)], axis=1).reshape(B, 2, 6, 1, D)
        sh1, sc1, g1, sh2, sc2, g2 = (mods[:, :, i] for i in range(6))

        wl = w_in[l].astype(BF16)
        o_ssd, o_hy, o_ml, o_hg, o_gate = 0, SSD_COLS, SSD_COLS + HY_COLS, SSD_COLS + HY_COLS + ML_COLS, \
            SSD_COLS + HY_COLS + ML_COLS + HG_COLS
        cols = lambda a, b_: wl[:, a:b_]
        z, xs, bc, dt, qk, v, o, gts = _proj(xc, sh1, sc1, norm1_g[l], [
            cols(o_ssd, o_ssd + w), cols(o_ssd + w, o_ssd + 2 * w), cols(o_ssd + 2 * w, o_ssd + w + SSD_CONV_CH),
            _pad_cols(cols(o_ssd + w + SSD_CONV_CH, o_hy), LANES),
            cols(o_ml, o_ml + 2 * w), cols(o_ml + 2 * w, o_ml + 3 * w), cols(o_ml + 3 * w, o_ml + 4 * w),
            _pad_cols(cols(o_ml + 4 * w, o_hg), LANES)], n_ctx_tiles)
        p_hy, gate_pre = _proj(xc, sh1, sc1, norm1_g[l], [cols(o_hy, o_ml), cols(o_gate, o_gate + N_BRANCHES * D)],
                               n_ctx_tiles)
        xc_cm = jnp.concatenate([xc[:, :n_ctx], _to_col_major(xc[:, n_ctx:])], axis=1)
        hq, hi, hff, hfb, hgt = _proj(xc_cm, sh1, sc1, norm1_g[l],
                                      [cols(o_hg + i * w, o_hg + (i + 1) * w) for i in range(5)], n_ctx_tiles)

        ya = _ssd_branch(z, xs, bc, dt, ssd_conv_w[l], ssd_conv_b[l], ssd_dt_bias[l], ssd_a_log[l], ssd_d[l],
                         ssd_norm_g[l], n_ctx_chunks)
        yc = _mlstm_branch(qk, v, o, gts, ml_conv_w[l], ml_conv_b[l], ml_i_bias[l], ml_f_bias[l], ml_norm_g[l],
                           n_ctx_chunks)
        yd_cm = _hgrn_branch(hq, hi, hff, hfb, hgt, lower_bounds[l], hg_norm_g[l], n_ctx_chunks)
        yd = jnp.concatenate([yd_cm[:, :n_ctx], _to_row_major(yd_cm[:, n_ctx:])], axis=1)
        hv, hx1, hx2 = _hyena_conv(p_hy, hy_conv_w[l], hy_conv_b[l], n_ctx_tiles)
        hy_params = tuple(p_[l].astype(F32) for p_ in (hy_w1, hy_b1, hy_w2, hy_b2, hy_w3, hy_freq))
        lat = lambda t: t[:, n_ctx:]
        sides, side_scale = _hyena_filter_sides(n, *hy_params)
        yb_lat = _hyena_latent(lat(hv), lat(hx1), lat(hx2), sides, side_scale, hy_skip[l])
        if last_layer:
            yb_ctx = jnp.zeros((B, n_ctx, w), F32)
        else:
            head = lambda t: t[:, :n_ctx]
            yb_ctx = _hyena_context(head(hv), head(hx1), head(hx2), _hyena_filter_time(n_ctx, *hy_params), hy_skip[l])
        yb = jnp.concatenate([yb_ctx, yb_lat], axis=1).astype(BF16)

        wr = jnp.stack(_split_bf16(_pad_cols(w_router[l].astype(F32), LANES)))
        x1, h2, aff = _merge((ya, yb, yc, yd), gate_pre, xc, g1, sh2, sc2, norm2_g[l],
                             w_branch[l].astype(BF16), w_out[l].astype(BF16), wr, n_ctx_tiles)
        wg, wu, wd = w_gate[l].astype(BF16), w_up[l].astype(BF16), w_down[l].astype(BF16)
        rows, vals = _expert_choice(h2, aff, n_ctx, n, wg, wu, wd)
        moe = jnp.zeros((B * T, D), F32).at[rows].add(vals)
        if not last_layer:
            rows_c, vals_c = _expert_choice(h2, aff, 0, n_ctx, wg, wu, wd)
            moe = moe.at[rows_c].add(vals_c)
        xc = _residual(x1, moe.reshape(B, T, D), g2, final_g, last_layer, n_ctx_tiles)
    return xc[:, n_ctx:]
```

```python
import functools
import math
import jax
import jax.numpy as jnp
from jax import lax
import numpy as np
from jax.experimental import pallas as pl
from jax.experimental.pallas import tpu as pltpu


D_MODEL = 1024
DEPTH = 2
GRID_W = 64
BRANCH_W = D_MODEL // 2
N_BRANCHES = 4
EPS = 1e-6
MASK_NEG = -1e30
LB_FLOOR = 1e-30

SSD_HEAD_DIM = 64
SSD_HEADS = BRANCH_W // SSD_HEAD_DIM
SSD_GROUPS = 2
SSD_STATE = 64
SSD_CONV_CH = BRANCH_W + 2 * SSD_GROUPS * SSD_STATE
SSD_COLS = BRANCH_W + SSD_CONV_CH + 2 * SSD_HEADS

HY_WIDTH = BRANCH_W
HY_ORDER = 2
HY_BANDS = 8
HY_DECAY_TARGET = 1e-2
HY_SHORT_PCT = 0.3
HY_LONG_PCT = 1.5
HY_COLS = (HY_ORDER + 1) * HY_WIDTH

ML_HEADS = 4
ML_HEAD_DIM = BRANCH_W // ML_HEADS
ML_COLS = 4 * BRANCH_W + 4 * ML_HEADS

HG_HEADS = 4
HG_HEAD_DIM = BRANCH_W // HG_HEADS
HG_COLS = 5 * BRANCH_W
HG_LEAF = 8

N_EXPERTS = 16
EC_CAPACITY = 2

LANES = 128
SUBLANES = 8
ROW_TILE = 256
SCAN_CHUNK = 128
VMEM_LIMIT = 56 * 1024 * 1024

F32 = jnp.float32
BF16 = jnp.bfloat16


def _dot(a, b):
    return jnp.dot(a, b, preferred_element_type=F32)


def _dot_nt(a, b):
    return lax.dot_general(a, b, (((1,), (1,)), ((), ())), preferred_element_type=F32)


def _dot_tn(a, b):
    return lax.dot_general(a, b, (((0,), (0,)), ((), ())), preferred_element_type=F32)


def _dot_mask(mask, x):
    hi = x.astype(BF16)
    rem = x - hi.astype(F32)
    mid = rem.astype(BF16)
    lo = (rem - mid.astype(F32)).astype(BF16)
    return _dot(mask, hi) + _dot(mask, mid) + _dot(mask, lo)


def _sigmoid(x):
    return 0.5 * jnp.tanh(0.5 * x) + 0.5


def _silu(x):
    return x * _sigmoid(x)


def _softplus(x):
    return jnp.maximum(x, 0.0) + jnp.log1p(jnp.exp(-jnp.abs(x)))


def _log_sigmoid(x):
    return jnp.minimum(x, 0.0) - jnp.log1p(jnp.exp(-jnp.abs(x)))


def _rms(x):
    return x * lax.rsqrt(jnp.mean(x * x, axis=-1, keepdims=True) + EPS)


def _proj_kernel(x_ref, sh_ref, sc_ref, g_ref, *refs, n_out):
    h = (_rms(x_ref[...]) * g_ref[...] * (1.0 + sc_ref[...]) + sh_ref[...]).astype(BF16)
    for w_ref, o_ref in zip(refs[:n_out], refs[n_out:]):
        o_ref[...] = _dot(h, w_ref[...])


def _proj(xc, shift, scale, gain, weights, n_ctx_tiles):
    B, T, D = xc.shape
    mod_spec = pl.BlockSpec((None, None, 1, D), lambda b, i: (b, jnp.where(i < n_ctx_tiles, 1, 0), 0, 0))
    return pl.pallas_call(
        functools.partial(_proj_kernel, n_out=len(weights)),
        grid=(B, T // ROW_TILE),
        in_specs=[pl.BlockSpec((None, ROW_TILE, D), lambda b, i: (b, i, 0)), mod_spec, mod_spec,
                  pl.BlockSpec((1, D), lambda b, i: (0, 0))]
                 + [pl.BlockSpec(w.shape, lambda b, i: (0, 0)) for w in weights],
        out_specs=[pl.BlockSpec((None, ROW_TILE, w.shape[1]), lambda b, i: (b, i, 0)) for w in weights],
        out_shape=[jax.ShapeDtypeStruct((B, T, w.shape[1]), F32) for w in weights],
        compiler_params=pltpu.CompilerParams(dimension_semantics=("parallel", "parallel"),
                                             vmem_limit_bytes=VMEM_LIMIT),
        name="norm_mod_proj",
    )(xc, shift, scale, gain.reshape(1, D), *weights)


def _scan_chunk(step, reverse, n_ctx_chunks, n_chunks):
    if not reverse:
        return step
    return jnp.where(step < n_ctx_chunks, n_ctx_chunks - 1 - step, n_chunks - 1 - (step - n_ctx_chunks))


def _scan_specs(reverse, n_ctx_chunks, n_chunks, chunk):
    cidx = lambda s: _scan_chunk(s, reverse, n_ctx_chunks, n_chunks)
    per_tile = chunk // SUBLANES
    n_tiles = n_chunks * per_tile

    def main(width):
        return pl.BlockSpec((None, chunk, width), lambda b, s: (b, cidx(s), 0))

    def prev(width):
        return pl.BlockSpec((None, SUBLANES, width), lambda b, s: (b, jnp.maximum(cidx(s) * per_tile - 1, 0), 0))

    def nxt(width):
        return pl.BlockSpec((None, SUBLANES, width),
                            lambda b, s: (b, jnp.minimum((cidx(s) + 1) * per_tile, n_tiles - 1), 0))

    def const(shape):
        return pl.BlockSpec(shape, lambda b, s: (0,) * len(shape))

    return main, prev, nxt, const


def _segment_edges(chunk_idx, n_ctx_chunks, n_chunks):
    first = jnp.logical_or(chunk_idx == 0, chunk_idx == n_ctx_chunks)
    last = jnp.logical_or(chunk_idx == n_ctx_chunks - 1, chunk_idx == n_chunks - 1)
    return first, last


def _conv3(u, u_prev_tile, u_next_tile, w, bias, first, last):
    q = u.shape[0]
    rows = lax.broadcasted_iota(jnp.int32, u.shape, 0)
    before = jnp.where(first, 0.0, u_prev_tile[SUBLANES - 1:SUBLANES, :])
    after = jnp.where(last, 0.0, u_next_tile[0:1, :])
    u_m1 = jnp.where(rows == 0, before, pltpu.roll(u, 1, 0))
    u_p1 = jnp.where(rows == q - 1, after, pltpu.roll(u, q - 1, 0))
    return w[0:1, :] * u_m1 + w[1:2, :] * u + w[2:3, :] * u_p1 + bias


def _order_masks(q, reverse):
    r = lax.broadcasted_iota(jnp.int32, (q, q), 0)
    c = lax.broadcasted_iota(jnp.int32, (q, q), 1)
    return (r <= c) if reverse else (r >= c)


def _ssd_kernel(z_ref, x_ref, xp_ref, xn_ref, bc_ref, bcp_ref, bcn_ref, dt_ref, yf_ref,
                cwx_ref, cbx_ref, cwbc_ref, cbbc_ref, dtb_ref, nega_ref, dskip_ref, ng_ref,
                out_ref, st_ref, *, reverse, n_ctx_chunks, n_chunks):
    q = x_ref.shape[0]
    step = pl.program_id(1)
    cidx = _scan_chunk(step, reverse, n_ctx_chunks, n_chunks)
    first, last = _segment_edges(cidx, n_ctx_chunks, n_chunks)

    @pl.when(step == 0)
    def _():
        st_ref[...] = jnp.zeros_like(st_ref)

    xs = _silu(_conv3(x_ref[...], xp_ref[...], xn_ref[...], cwx_ref[...], cbx_ref[...], first, last))
    bc = _silu(_conv3(bc_ref[...], bcp_ref[...], bcn_ref[...], cwbc_ref[...], cbbc_ref[...], first, last))
    dt_all = _softplus(dt_ref[...] + dtb_ref[...])
    da_all = dt_all * nega_ref[...]
    mask = _order_masks(q, reverse)
    acs = _dot_mask(mask.astype(BF16), da_all)
    acs_t = acs.T
    dt_t = dt_all.T
    edge = 0 if reverse else q - 1
    gn = SSD_STATE
    ys = []
    for g in range(SSD_GROUPS):
        bg = bc[:, g * gn:(g + 1) * gn].astype(BF16)
        cg = bc[:, (SSD_GROUPS + g) * gn:(SSD_GROUPS + g + 1) * gn].astype(BF16)
        cb = _dot_nt(cg, bg)
        for hh in range(SSD_HEADS // SSD_GROUPS):
            h = g * (SSD_HEADS // SSD_GROUPS) + hh
            col = (SSD_HEADS if reverse else 0) + h
            a_col = acs[:, col:col + 1]
            a_row = acs_t[col:col + 1, :]
            a_end = a_row[:, edge:edge + 1]
            decay = jnp.where(mask, jnp.exp(jnp.minimum(a_col - a_row, 0.0)), 0.0)
            scores = (cb * decay * dt_t[col:col + 1, :]).astype(BF16)
            xh = xs[:, h * SSD_HEAD_DIM:(h + 1) * SSD_HEAD_DIM]
            st = st_ref[h]
            y = _dot(scores, xh.astype(BF16)) + _dot(cg, st.astype(BF16)) * jnp.exp(a_col)
            to_end = jnp.exp(a_end - a_col) * dt_all[:, col:col + 1]
            st_ref[h] = jnp.exp(a_end) * st + _dot_tn(bg, (xh * to_end).astype(BF16))
            ys.append(y)
    y = jnp.concatenate(ys, axis=1)
    if not reverse:
        out_ref[...] = y
    else:
        y = (y + yf_ref[...] + dskip_ref[...] * xs) * _silu(z_ref[...])
        out_ref[...] = (_rms(y) * ng_ref[...]).astype(out_ref.dtype)


def _ssd_pass(z, x, bc, dt, y_fwd, consts, reverse, n_ctx_chunks):
    B, T, _ = x.shape
    n_chunks = T // SCAN_CHUNK
    main, prev, nxt, const = _scan_specs(reverse, n_ctx_chunks, n_chunks, SCAN_CHUNK)
    w = BRANCH_W
    wbc = 2 * SSD_GROUPS * SSD_STATE
    in_specs = [main(w), main(w), prev(w), nxt(w), main(wbc), prev(wbc), nxt(wbc), main(LANES), main(w),
                const((3, w)), const((1, w)), const((3, wbc)), const((1, wbc)), const((1, LANES)),
                const((1, LANES)), const((1, w)), const((1, w))]
    return pl.pallas_call(
        functools.partial(_ssd_kernel, reverse=reverse, n_ctx_chunks=n_ctx_chunks, n_chunks=n_chunks),
        grid=(B, n_chunks),
        in_specs=in_specs,
        out_specs=main(w),
        out_shape=jax.ShapeDtypeStruct((B, T, w), BF16 if reverse else F32),
        scratch_shapes=[pltpu.VMEM((SSD_HEADS, SSD_STATE, SSD_HEAD_DIM), F32)],
        compiler_params=pltpu.CompilerParams(dimension_semantics=("parallel", "arbitrary"),
                                             vmem_limit_bytes=VMEM_LIMIT),
        name="ssd_bwd_finish" if reverse else "ssd_fwd",
    )(z, x, x, x, bc, bc, bc, dt, y_fwd, *consts)


def _ssd_branch(z, x, bc, dt, conv_w, conv_b, dt_bias, a_log, d_skip, norm_g, n_ctx_chunks):
    w = BRANCH_W
    pad = LANES - 2 * SSD_HEADS
    dtb = jnp.pad(dt_bias.astype(F32).reshape(1, -1), ((0, 0), (0, pad)))
    nega = jnp.pad(-jnp.exp(a_log.astype(F32)).reshape(1, -1), ((0, 0), (0, pad)))
    consts = (conv_w[:, :w], conv_b[:w].reshape(1, w), conv_w[:, w:], conv_b[w:].reshape(1, -1), dtb, nega,
              jnp.repeat(d_skip.astype(F32), SSD_HEAD_DIM).reshape(1, w), norm_g.reshape(1, w))
    y_f = _ssd_pass(z, x, bc, dt, x, consts, False, n_ctx_chunks)
    return _ssd_pass(z, x, bc, dt, y_f, consts, True, n_ctx_chunks)


def _mlstm_kernel(qk_ref, qkp_ref, qkn_ref, v_ref, o_ref, gt_ref, hf_ref, cw_ref, cb_ref, gb_ref, ng_ref,
                  out_ref, c_ref, m_ref, *, reverse, n_ctx_chunks, n_chunks):
    q = qk_ref.shape[0]
    dh = ML_HEAD_DIM
    step = pl.program_id(1)
    cidx = _scan_chunk(step, reverse, n_ctx_chunks, n_chunks)
    first, last = _segment_edges(cidx, n_ctx_chunks, n_chunks)

    @pl.when(step == 0)
    def _():
        c_ref[...] = jnp.zeros_like(c_ref)
        m_ref[...] = jnp.zeros_like(m_ref)

    qk = _silu(_conv3(qk_ref[...], qkp_ref[...], qkn_ref[...], cw_ref[...], cb_ref[...], first, last))
    gates = gt_ref[...] + gb_ref[...]
    logf_all = _log_sigmoid(gates)
    mask = _order_masks(q, reverse)
    bcum = _dot_mask(mask.astype(BF16), logf_all)
    bcum_t = bcum.T
    gates_t = gates.T
    edge = 0 if reverse else q - 1
    lane = lax.broadcasted_iota(jnp.int32, (q, dh), 1)
    ones_col = jnp.where(lane == 0, 1.0, 0.0).astype(BF16)
    hs = []
    for h in range(ML_HEADS):
        li = (ML_HEADS if reverse else 0) + h
        lf = 2 * ML_HEADS + li
        qh = qk[:, h * dh:(h + 1) * dh].astype(BF16)
        kh = qk[:, BRANCH_W + h * dh:BRANCH_W + (h + 1) * dh] * (dh ** -0.5)
        v_aug = jnp.concatenate([v_ref[:, h * dh:(h + 1) * dh].astype(BF16), ones_col], axis=1)
        b_col = bcum[:, lf:lf + 1]
        b_row = bcum_t[lf:lf + 1, :]
        i_col = gates[:, li:li + 1]
        i_row = gates_t[li:li + 1, :]
        b_tot = b_row[:, edge:edge + 1]
        m_prev = m_ref[h:h + 1, 0:1]
        c_prev = c_ref[h]
        dmat = jnp.where(mask, b_col - b_row + i_row, MASK_NEG)
        inter = b_col + m_prev
        m_t = jnp.maximum(inter, jnp.max(dmat, axis=1, keepdims=True))
        wgt = jnp.where(mask, jnp.exp(jnp.minimum(dmat - m_t, 0.0)), 0.0) * _dot_nt(qh, kh.astype(BF16))
        w_int = jnp.exp(inter - m_t)
        cross = _dot(qh, c_prev.astype(BF16))
        num = _dot(wgt.astype(BF16), v_aug[:, :dh]) + w_int * cross[:, :dh]
        den = jnp.sum(wgt, axis=1, keepdims=True) + w_int * cross[:, dh:dh + 1]
        hs.append(num / jnp.maximum(jnp.abs(den), jnp.exp(-m_t)))
        a = b_tot - b_col + i_col
        m_new = jnp.maximum(b_tot + m_prev, jnp.max(a, axis=0, keepdims=True))
        kw = (kh * jnp.exp(a - m_new)).astype(BF16)
        c_ref[h] = jnp.exp(b_tot + m_prev - m_new) * c_prev + _dot_tn(kw, v_aug)
        m_ref[h:h + 1, :] = jnp.broadcast_to(m_new, (1, LANES))
    if not reverse:
        out_ref[...] = jnp.concatenate(hs, axis=1)
    else:
        hf = hf_ref[...]
        outs = [_rms(hs[h] + hf[:, h * dh:(h + 1) * dh]) for h in range(ML_HEADS)]
        out_ref[...] = (_sigmoid(o_ref[...]) * jnp.concatenate(outs, axis=1) * ng_ref[...]).astype(out_ref.dtype)


def _mlstm_pass(qk, v, o, gates, h_fwd, consts, reverse, n_ctx_chunks):
    B, T, _ = v.shape
    n_chunks = T // SCAN_CHUNK
    main, prev, nxt, const = _scan_specs(reverse, n_ctx_chunks, n_chunks, SCAN_CHUNK)
    w = BRANCH_W
    in_specs = [main(2 * w), prev(2 * w), nxt(2 * w), main(w), main(w), main(LANES), main(w),
                const((3, 2 * w)), const((1, 2 * w)), const((1, LANES)), const((1, w))]
    return pl.pallas_call(
        functools.partial(_mlstm_kernel, reverse=reverse, n_ctx_chunks=n_ctx_chunks, n_chunks=n_chunks),
        grid=(B, n_chunks),
        in_specs=in_specs,
        out_specs=main(w),
        out_shape=jax.ShapeDtypeStruct((B, T, w), BF16 if reverse else F32),
        scratch_shapes=[pltpu.VMEM((ML_HEADS, ML_HEAD_DIM, 2 * ML_HEAD_DIM), F32),
                        pltpu.VMEM((SUBLANES, LANES), F32)],
        compiler_params=pltpu.CompilerParams(dimension_semantics=("parallel", "arbitrary"),
                                             vmem_limit_bytes=VMEM_LIMIT),
        name="mlstm_bwd_finish" if reverse else "mlstm_fwd",
    )(qk, qk, qk, v, o, gates, h_fwd, *consts)


def _mlstm_branch(qk, v, o, gates, conv_w, conv_b, i_bias, f_bias, norm_g, n_ctx_chunks):
    w = BRANCH_W
    gb = jnp.concatenate([i_bias.astype(F32).reshape(-1), f_bias.astype(F32).reshape(-1)])
    gb = jnp.pad(gb.reshape(1, -1), ((0, 0), (0, LANES - 4 * ML_HEADS)))
    consts = (conv_w, conv_b.reshape(1, 2 * w), gb, norm_g.reshape(1, w))
    h_f = _mlstm_pass(qk, v, o, gates, v, consts, False, n_ctx_chunks)
    return _mlstm_pass(qk, v, o, gates, h_f, consts, True, n_ctx_chunks)


def _hgrn_level_matrix(q, block, reverse):
    t = lax.broadcasted_iota(jnp.int32, (q, q), 0)
    r = lax.broadcasted_iota(jnp.int32, (q, q), 1)
    blk = t // block
    start = blk * block
    end = start + block - 1
    odd = (blk % 2) == 1
    if not reverse:
        lo = jnp.where(odd, start, t + 1)
        hi = jnp.where(odd, t, end)
    else:
        lo = jnp.where(odd, start, t)
        hi = jnp.where(odd, t - 1, end)
    return jnp.where((r >= lo) & (r <= hi), 1.0, 0.0)


def _hgrn_select_matrix(q, reverse):
    mats = [_order_masks(q, reverse).astype(F32)]
    block = HG_LEAF
    while block < q:
        mats.append(_hgrn_level_matrix(q, block, reverse))
        block *= 2
    return jnp.concatenate(mats, axis=0).astype(BF16)


def _hgrn_kernel(q_ref, v_ref, f_ref, g_ref, of_ref, lb_ref, ng_ref, sel_ref, out_ref,
                 st_ref, b_ref, qs_ref, ks_ref, ol_ref, *, reverse):
    q = q_ref.shape[0]
    dh = HG_HEAD_DIM
    step = pl.program_id(1)

    @pl.when(step == 0)
    def _():
        st_ref[...] = jnp.zeros_like(st_ref)

    lb = lb_ref[...]
    pre = f_ref[...]
    log_lb = jnp.log(jnp.maximum(lb, LB_FLOOR))
    log_ub = jnp.log1p(-lb)
    e_pre = jnp.exp(-jnp.abs(pre))
    lo = log_ub + jnp.minimum(pre, 0.0) - jnp.log1p(e_pre)
    logf = jnp.maximum(log_lb, lo) + jnp.log1p(jnp.exp(-jnp.abs(log_lb - lo)))
    key = (1.0 - lb) * (jnp.where(pre >= 0.0, e_pre, 1.0) / (1.0 + e_pre))
    qv = _silu(q_ref[...])
    sums = _dot_mask(sel_ref[...], logf)
    bq = sums[:q]
    b_ref[...] = bq
    qs_ref[...] = qv
    ks_ref[...] = key

    def leaf_body(l, carry):
        rows = pl.ds(pl.multiple_of(l * HG_LEAF, HG_LEAF), HG_LEAF)
        bl = b_ref[rows, :]
        ql = qs_ref[rows, :]
        kl = ks_ref[rows, :]
        vl = v_ref[rows, :]
        tt = lax.broadcasted_iota(jnp.int32, (HG_LEAF, dh), 0)
        acc = [jnp.zeros((HG_LEAF, dh), F32) for _ in range(HG_HEADS)]
        for s in range(HG_LEAF):
            ok = (tt <= s) if reverse else (tt >= s)
            for h in range(HG_HEADS):
                sl = slice(h * dh, (h + 1) * dh)
                diff = bl[:, sl] - bl[s:s + 1, sl]
                e = jnp.exp(jnp.where(ok, diff, MASK_NEG))
                att = jnp.sum(ql[:, sl] * e * kl[s:s + 1, sl], axis=1, keepdims=True)
                acc[h] = acc[h] + att * vl[s:s + 1, sl]
        ol_ref[rows, :] = jnp.concatenate(acc, axis=1)
        return carry

    lax.fori_loop(0, q // HG_LEAF, leaf_body, 0)

    t_idx = lax.broadcasted_iota(jnp.int32, (q, q), 0)
    s_idx = lax.broadcasted_iota(jnp.int32, (q, q), 1)
    row_t = lax.broadcasted_iota(jnp.int32, (q, dh), 0)
    levels = []
    block = HG_LEAF
    while block < q:
        fac = jnp.exp(sums[(len(levels) + 1) * q:(len(levels) + 2) * q])
        tb, sb = t_idx // block, s_idx // block
        if not reverse:
            pair = ((tb % 2) == 1) & (sb == tb - 1)
            is_query = ((row_t // block) % 2) == 1
        else:
            pair = ((tb % 2) == 0) & (sb == tb + 1)
            is_query = ((row_t // block) % 2) == 0
        levels.append((fac, pair, is_query))
        block *= 2
    edge = 0 if reverse else q - 1
    b_end = bq[edge:edge + 1, :]
    q_in = qv * jnp.exp(bq)
    k_out = key * jnp.exp(b_end - bq)
    outs = []
    for h in range(HG_HEADS):
        sl = slice(h * dh, (h + 1) * dh)
        att = jnp.zeros((q, q), F32)
        for fac, pair, is_query in levels:
            qt = jnp.where(is_query, qv[:, sl] * fac[:, sl], 0.0).astype(BF16)
            kt = jnp.where(is_query, 0.0, key[:, sl] * fac[:, sl]).astype(BF16)
            att = att + jnp.where(pair, _dot_nt(qt, kt), 0.0)
        vh = v_ref[:, sl].astype(BF16)
        st = st_ref[h]
        o = ol_ref[:, sl] + _dot(att.astype(BF16), vh) + _dot_nt(q_in[:, sl].astype(BF16), st.astype(BF16))
        st_ref[h] = jnp.exp(b_end[:, sl]) * st + _dot_tn(vh, k_out[:, sl].astype(BF16))
        outs.append(o)
    if not reverse:
        out_ref[...] = jnp.concatenate(outs, axis=1)
    else:
        of = of_ref[...]
        fin = [_rms(outs[h] + of[:, h * dh:(h + 1) * dh]) for h in range(HG_HEADS)]
        out_ref[...] = (jnp.concatenate(fin, axis=1) * ng_ref[...] * _sigmoid(g_ref[...])).astype(out_ref.dtype)


def _hgrn_pass(qr, v, f, g, o_fwd, lb, norm_g, reverse, n_ctx_chunks):
    B, T, w = v.shape
    n_chunks = T // SCAN_CHUNK
    main, _, _, const = _scan_specs(reverse, n_ctx_chunks, n_chunks, SCAN_CHUNK)
    sel = _hgrn_select_matrix(SCAN_CHUNK, reverse)
    return pl.pallas_call(
        functools.partial(_hgrn_kernel, reverse=reverse),
        grid=(B, n_chunks),
        in_specs=[main(w), main(w), main(w), main(w), main(w), const((1, w)), const((1, w)), const(sel.shape)],
        out_specs=main(w),
        out_shape=jax.ShapeDtypeStruct((B, T, w), BF16 if reverse else F32),
        scratch_shapes=[pltpu.VMEM((HG_HEADS, HG_HEAD_DIM, HG_HEAD_DIM), F32)]
                       + [pltpu.VMEM((SCAN_CHUNK, w), F32) for _ in range(4)],
        compiler_params=pltpu.CompilerParams(dimension_semantics=("parallel", "arbitrary"),
                                             vmem_limit_bytes=VMEM_LIMIT),
        name="hgrn_bwd_finish" if reverse else "hgrn_fwd",
    )(qr, v, f, g, o_fwd, lb.reshape(1, w), norm_g.reshape(1, w), sel)


def _hgrn_branch(qr, v, f_fwd, f_bwd, g, lb, norm_g, n_ctx_chunks):
    o_f = _hgrn_pass(qr, v, f_fwd, g, v, lb, norm_g, False, n_ctx_chunks)
    return _hgrn_pass(qr, v, f_bwd, g, o_f, lb, norm_g, True, n_ctx_chunks)


def _hyena_filter_hidden(n, w1, b1, w2, b2, freq):
    pos = jnp.arange(n, dtype=F32)
    t = pos / max(n - 1, 1)
    bands = jnp.arange(1, HY_BANDS + 1, dtype=F32)
    ang = (2.0 * math.pi / n) * pos[:, None] * bands[None, :]
    feats = jnp.concatenate([t[:, None], jnp.cos(ang), jnp.sin(ang)], axis=-1)
    hid = jnp.sin(freq[0] * (feats @ w1 + b1))
    return jnp.sin(freq[1] * (hid @ w2 + b2)), t


def _hyena_deltas():
    return jnp.abs(jnp.linspace(math.log(HY_DECAY_TARGET) / HY_LONG_PCT,
                                math.log(HY_DECAY_TARGET) / HY_SHORT_PCT, HY_WIDTH, dtype=F32))


def _hyena_filter_time(n, w1, b1, w2, b2, w3, freq):
    hid, t = _hyena_filter_hidden(n, w1, b1, w2, b2, freq)
    h = (hid @ w3).reshape(n, HY_ORDER, 2, HY_WIDTH)
    h = h * jnp.exp(-t[:, None] * _hyena_deltas()[None, :])[:, None, None, :]
    two_sided = jnp.concatenate([h[:, :, 0], jnp.zeros((1, HY_ORDER, HY_WIDTH), F32),
                                 jnp.flip(h[1:, :, 1], axis=0)], axis=0)
    return two_sided / (jnp.sum(jnp.abs(two_sided), axis=0, keepdims=True) + EPS)


def _filter_kernel(hid_ref, t_ref, wh_ref, wl_ref, delta_ref, o_ref):
    hid = hid_ref[...]
    hid_hi, hid_lo = _split_bf16(hid)
    h = _dot(hid_hi, wh_ref[...]) + _dot(hid_lo, wh_ref[...]) + _dot(hid_hi, wl_ref[...])
    decay = jnp.exp(-t_ref[...] * delta_ref[...])
    wd = HY_WIDTH
    lag0 = jnp.logical_and(pl.program_id(0) == 0, lax.broadcasted_iota(jnp.int32, decay.shape, 0) == 0)
    for g in range(2 * HY_ORDER):
        val = h[:, g * wd:(g + 1) * wd] * decay
        o_ref[g] = jnp.where(lag0, 0.0, val) if g % 2 == 1 else val


def _hyena_filter_sides(n, w1, b1, w2, b2, w3, freq):
    hid, t = _hyena_filter_hidden(n, w1, b1, w2, b2, freq)
    wh, wl = _split_bf16(w3)
    g4, wd = 2 * HY_ORDER, HY_WIDTH
    sides = pl.pallas_call(
        _filter_kernel,
        grid=(n // ROW_TILE,),
        in_specs=[pl.BlockSpec((ROW_TILE, hid.shape[1]), lambda i: (i, 0)), pl.BlockSpec((ROW_TILE, 1), lambda i: (i, 0)),
                  pl.BlockSpec(wh.shape, lambda i: (0, 0)), pl.BlockSpec(wl.shape, lambda i: (0, 0)),
                  pl.BlockSpec((1, wd), lambda i: (0, 0))],
        out_specs=pl.BlockSpec((g4, ROW_TILE, wd), lambda i: (0, i, 0)),
        out_shape=jax.ShapeDtypeStruct((g4, n, wd), F32),
        compiler_params=pltpu.CompilerParams(dimension_semantics=("parallel",)),
        name="hyena_filter_sides",
    )(hid, t.reshape(n, 1), wh, wl, _hyena_deltas().reshape(1, wd))
    norm = jnp.sum(jnp.abs(sides), axis=1).reshape(HY_ORDER, 2, wd).sum(axis=1)
    return sides, 1.0 / (norm + EPS)


def _hyena_conv_kernel(p_ref, pp_ref, pn_ref, w_ref, b_ref, v_ref, x1_ref, x2_ref, *, n_ctx_tiles, n_tiles):
    i = pl.program_id(1)
    first, last = _segment_edges(i, n_ctx_tiles, n_tiles)
    u = _conv3(p_ref[...], pp_ref[...], pn_ref[...], w_ref[...], b_ref[...], first, last)
    wd = HY_WIDTH
    v_ref[...] = u[:, :wd]
    x1_ref[...] = u[:, wd:2 * wd]
    x2_ref[...] = u[:, 2 * wd:]


def _hyena_conv(p, conv_w, conv_b, n_ctx_tiles):
    B, T, C = p.shape
    n_tiles = T // ROW_TILE
    main, prev, nxt, const = _scan_specs(False, n_ctx_tiles, n_tiles, ROW_TILE)
    return pl.pallas_call(
        functools.partial(_hyena_conv_kernel, n_ctx_tiles=n_ctx_tiles, n_tiles=n_tiles),
        grid=(B, n_tiles),
        in_specs=[main(C), prev(C), nxt(C), const((3, C)), const((1, C))],
        out_specs=[main(HY_WIDTH)] * 3,
        out_shape=[jax.ShapeDtypeStruct((B, T, HY_WIDTH), F32)] * 3,
        compiler_params=pltpu.CompilerParams(dimension_semantics=("parallel", "parallel")),
        name="hyena_short_conv",
    )(p, p, p, conv_w, conv_b.reshape(1, C))


DFT_N2 = 256
DFT_ROW_TILE = SUBLANES


def _split_bf16(x):
    hi = x.astype(BF16)
    return hi, (x - hi.astype(F32)).astype(BF16)


def _dft_tables(n):
    L = 2 * n
    n1_full = L // DFT_N2
    nh = n1_full // 2
    nk = nh + 1
    nkp = -(-nk // SUBLANES) * SUBLANES
    k1 = jnp.arange(nkp, dtype=jnp.int32)
    valid = (k1 < nk)[:, None]

    def stage1(rows):
        n1 = jnp.arange(rows, dtype=jnp.int32)
        ang = (2.0 * math.pi / n1_full) * ((k1[:, None] * n1[None, :]) % n1_full).astype(F32)
        return jnp.concatenate([jnp.where(valid, jnp.cos(ang), 0.0), jnp.where(valid, -jnp.sin(ang), 0.0)], axis=0)

    n1 = jnp.arange(nh, dtype=jnp.int32)
    ang = (2.0 * math.pi / n1_full) * ((n1[:, None] * k1[None, :]) % n1_full).astype(F32)
    ck = jnp.where((k1 == 0) | (k1 == nh), 1.0, 2.0) * jnp.where(k1 < nk, 1.0, 0.0) / L
    stage3 = jnp.concatenate([jnp.cos(ang) * ck[None, :], -jnp.sin(ang) * ck[None, :]], axis=1)
    n2 = jnp.arange(DFT_N2, dtype=jnp.int32)
    idx = (n2[None, :, None] * (k1[:, None, None] + n1_full * n2[None, None, :])) % L
    ang2 = (2.0 * math.pi / L) * idx.astype(F32)
    gr, gi = jnp.cos(ang2), -jnp.sin(ang2)
    grt, git = jnp.swapaxes(gr, 1, 2), jnp.swapaxes(gi, 1, 2)
    m_fwd = jnp.concatenate([jnp.concatenate([grt, -git], axis=2), jnp.concatenate([git, grt], axis=2)], axis=1)
    m_inv = jnp.swapaxes(m_fwd, 1, 2)
    eye = jnp.eye(DFT_ROW_TILE, dtype=F32)
    return dict(nh=nh, nk=nk, nkp=nkp, f1=jnp.kron(stage1(nh), eye).astype(BF16),
                f3=jnp.kron(stage3, eye).astype(BF16), m_fwd=m_fwd.astype(BF16), m_inv=m_inv.astype(BF16))


def _dft_in_kernel(u_ref, f_ref, o_ref):
    rows, t2, wd = u_ref.shape
    nkp = o_ref.shape[1]
    r = _dot(f_ref[...], u_ref[...].reshape(rows * t2, wd).astype(BF16))
    o_ref[0] = r[:nkp * t2].reshape(nkp, t2, wd)
    o_ref[1] = r[nkp * t2:].reshape(nkp, t2, wd)


def _dft_in(u, f1):
    G, rows, n2, wd = u.shape
    t2 = DFT_ROW_TILE
    nkp = f1.shape[0] // (2 * t2)
    return pl.pallas_call(
        _dft_in_kernel,
        grid=(G, n2 // t2),
        in_specs=[pl.BlockSpec((None, rows, t2, wd), lambda g, j: (g, 0, j, 0)),
                  pl.BlockSpec(f1.shape, lambda g, j: (0, 0))],
        out_specs=pl.BlockSpec((None, 2, nkp, t2, wd), lambda g, j: (g, 0, 0, j, 0)),
        out_shape=jax.ShapeDtypeStruct((G, 2, nkp, n2, wd), F32),
        compiler_params=pltpu.CompilerParams(dimension_semantics=("parallel", "parallel"),
                                             vmem_limit_bytes=VMEM_LIMIT),
        name="hyena_dft_rows",
    )(u, f1)


def _spectrum_kernel(af_ref, ab_ref, sc_ref, m_ref, o_ref, *, nk):
    k1 = pl.program_id(0)
    half = DFT_N2

    @pl.when(k1 < nk)
    def _():
        m = m_ref[...]
        uf = _dot(m, jnp.concatenate([af_ref[0], af_ref[1]], axis=0).astype(BF16))
        ub = _dot(m, jnp.concatenate([ab_ref[0], ab_ref[1]], axis=0).astype(BF16))
        o_ref[0] = (uf[:half] + ub[:half]) * sc_ref[...]
        o_ref[1] = (uf[half:] - ub[half:]) * sc_ref[...]

    @pl.when(k1 >= nk)
    def _():
        o_ref[...] = jnp.zeros_like(o_ref)


def _spectrum(a, scale, tables):
    _, _, nkp, n2, wd = a.shape
    slab = lambda g_of: pl.BlockSpec((None, 2, None, n2, wd), g_of)
    return pl.pallas_call(
        functools.partial(_spectrum_kernel, nk=tables["nk"]),
        grid=(nkp, HY_ORDER),
        in_specs=[slab(lambda k, o: (2 * o, 0, k, 0, 0)), slab(lambda k, o: (2 * o + 1, 0, k, 0, 0)),
                  pl.BlockSpec((None, 1, wd), lambda k, o: (o, 0, 0)),
                  pl.BlockSpec((None, 2 * n2, 2 * n2), lambda k, o: (k, 0, 0))],
        out_specs=slab(lambda k, o: (o, 0, k, 0, 0)),
        out_shape=jax.ShapeDtypeStruct((HY_ORDER, 2, nkp, n2, wd), F32),
        compiler_params=pltpu.CompilerParams(dimension_semantics=("parallel", "parallel"),
                                             vmem_limit_bytes=VMEM_LIMIT),
        name="hyena_spectrum",
    )(a, a, scale, tables["m_fwd"])


def _dft_mid_kernel(a_ref, h_ref, mf_ref, mi_ref, o_ref, *, nk):
    k1 = pl.program_id(0)
    half = DFT_N2

    @pl.when(k1 < nk)
    def _():
        u = _dot(mf_ref[...], jnp.concatenate([a_ref[0], a_ref[1]], axis=0).astype(BF16))
        ur, ui = u[:half], u[half:]
        hr, hi = h_ref[0], h_ref[1]
        v = jnp.concatenate([ur * hr - ui * hi, ur * hi + ui * hr], axis=0)
        y = _dot(mi_ref[...], v.astype(BF16))
        o_ref[0] = y[:half]
        o_ref[1] = y[half:]

    @pl.when(k1 >= nk)
    def _():
        o_ref[...] = jnp.zeros_like(o_ref)


def _dft_mid(a, spec, order, tables):
    G, _, nkp, n2, wd = a.shape
    slab = lambda g_of: pl.BlockSpec((None, 2, None, n2, wd), g_of)
    mat = pl.BlockSpec((None, 2 * n2, 2 * n2), lambda k, g: (k, 0, 0))
    return pl.pallas_call(
        functools.partial(_dft_mid_kernel, nk=tables["nk"]),
        grid=(nkp, G),
        in_specs=[slab(lambda k, g: (g, 0, k, 0, 0)), slab(lambda k, g: (order, 0, k, 0, 0)), mat, mat],
        out_specs=slab(lambda k, g: (g, 0, k, 0, 0)),
        out_shape=jax.ShapeDtypeStruct(a.shape, F32),
        compiler_params=pltpu.CompilerParams(dimension_semantics=("parallel", "parallel"),
                                             vmem_limit_bytes=VMEM_LIMIT),
        name="hyena_dft_mid",
    )(a, spec, tables["m_fwd"], tables["m_inv"])


def _dft_out_kernel(b_ref, u_ref, x_ref, skip_ref, f_ref, o_ref):
    rows, t2, wd = u_ref.shape
    nkp = b_ref.shape[1]
    bb = jnp.concatenate([b_ref[0].reshape(nkp * t2, wd), b_ref[1].reshape(nkp * t2, wd)], axis=0)
    y = _dot(f_ref[...], bb.astype(BF16)).reshape(rows, t2, wd)
    o_ref[...] = x_ref[...] * (y + u_ref[...] * skip_ref[...])


def _dft_out(bq, u, xg, skip_row, f3):
    G, rows, n2, wd = u.shape
    nkp = bq.shape[2]
    t2 = DFT_ROW_TILE
    tile = pl.BlockSpec((None, rows, t2, wd), lambda g, j: (g, 0, j, 0))
    return pl.pallas_call(
        _dft_out_kernel,
        grid=(G, n2 // t2),
        in_specs=[pl.BlockSpec((None, 2, nkp, t2, wd), lambda g, j: (g, 0, 0, j, 0)), tile, tile,
                  pl.BlockSpec((1, wd), lambda g, j: (0, 0)), pl.BlockSpec(f3.shape, lambda g, j: (0, 0))],
        out_specs=tile,
        out_shape=jax.ShapeDtypeStruct(u.shape, F32),
        compiler_params=pltpu.CompilerParams(dimension_semantics=("parallel", "parallel"),
                                             vmem_limit_bytes=VMEM_LIMIT),
        name="hyena_dft_rows_inverse",
    )(bq, u, xg, skip_row, f3)


def _hyena_latent(v, x1, x2, sides, scale, skip):
    B, n, wd = v.shape
    tb = _dft_tables(n)
    view = lambda t: t.reshape(t.shape[0], tb["nh"], DFT_N2, wd)
    spec = _spectrum(_dft_in(view(sides), tb["f1"]), scale.reshape(HY_ORDER, 1, wd), tb)
    z = view(v)
    for o, xg in enumerate((x1, x2)):
        bq = _dft_mid(_dft_in(z, tb["f1"]), spec, o, tb)
        z = _dft_out(bq, z, view(xg), skip[o].astype(F32).reshape(1, wd), tb["f3"])
    return z.reshape(B, n, wd)


def _hyena_context(v, x1, x2, filt, skip):
    n = v.shape[1]
    spec = jnp.fft.rfft(filt, axis=0)
    z = v
    for o, xg in enumerate((x1, x2)):
        zf = jnp.fft.rfft(z, n=2 * n, axis=1)
        y = jnp.fft.irfft(zf * spec[:, o], n=2 * n, axis=1)[:, :n]
        z = xg * (y + z * skip[o].astype(F32))
    return z


def _merge_kernel(ya_ref, yb_ref, yc_ref, yd_ref, gate_ref, x_ref, g1_ref, sh_ref, sc_ref, ng_ref,
                  wb_ref, wo_ref, wr_ref, x1_ref, h2_ref, aff_ref):
    d = x_ref.shape[1]
    merged = None
    for i, y_ref in enumerate((ya_ref, yb_ref, yc_ref, yd_ref)):
        term = _sigmoid(gate_ref[:, i * d:(i + 1) * d]) * _dot(y_ref[...], wb_ref[i])
        merged = term if merged is None else merged + term
    x1 = x_ref[...] + g1_ref[...] * _dot(merged.astype(BF16), wo_ref[...])
    x1_ref[...] = x1
    h2 = _rms(x1) * ng_ref[...] * (1.0 + sc_ref[...]) + sh_ref[...]
    h2_ref[...] = h2.astype(h2_ref.dtype)
    h2_hi, h2_lo = _split_bf16(h2)
    logits = _dot(h2_hi, wr_ref[0]) + _dot(h2_lo, wr_ref[0]) + _dot(h2_hi, wr_ref[1])
    lane = lax.broadcasted_iota(jnp.int32, logits.shape, 1)
    logits = jnp.where(lane < N_EXPERTS, logits, MASK_NEG)
    e = jnp.exp(logits - jnp.max(logits, axis=1, keepdims=True))
    aff_ref[...] = e / jnp.sum(e, axis=1, keepdims=True)


def _merge(ys, gates, xc, g1, shift, scale, gain, wb, wo, wr, n_ctx_tiles):
    B, T, D = xc.shape
    w = BRANCH_W
    tile = lambda width: pl.BlockSpec((None, ROW_TILE, width), lambda b, i: (b, i, 0))
    mod_spec = pl.BlockSpec((None, None, 1, D), lambda b, i: (b, jnp.where(i < n_ctx_tiles, 1, 0), 0, 0))
    const = lambda shape: pl.BlockSpec(shape, lambda b, i: (0,) * len(shape))
    return pl.pallas_call(
        _merge_kernel,
        grid=(B, T // ROW_TILE),
        in_specs=[tile(w), tile(w), tile(w), tile(w), tile(N_BRANCHES * D), tile(D), mod_spec, mod_spec, mod_spec,
                  const((1, D)), const(wb.shape), const(wo.shape), const(wr.shape)],
        out_specs=[tile(D), tile(D), tile(LANES)],
        out_shape=[jax.ShapeDtypeStruct((B, T, D), F32), jax.ShapeDtypeStruct((B, T, D), BF16),
                   jax.ShapeDtypeStruct((B, T, LANES), F32)],
        compiler_params=pltpu.CompilerParams(dimension_semantics=("parallel", "parallel"),
                                             vmem_limit_bytes=VMEM_LIMIT),
        name="merge_out_router",
    )(*ys, gates, xc, g1, shift, scale, gain.reshape(1, D), wb, wo, wr)


def _expert_kernel(x_ref, s_ref, wg_ref, wu_ref, wd_ref, o_ref):
    x = x_ref[...]
    hid = _silu(_dot(x, wg_ref[...])) * _dot(x, wu_ref[...])
    o_ref[...] = _dot(hid.astype(BF16), wd_ref[...]) * s_ref[...]


def _experts(xg, score, wg, wu, wd):
    E, R, D = xg.shape
    F = wg.shape[2]
    tm = min(R, 512)
    return pl.pallas_call(
        _expert_kernel,
        grid=(E, R // tm),
        in_specs=[pl.BlockSpec((None, tm, D), lambda e, i: (e, i, 0)),
                  pl.BlockSpec((None, tm, 1), lambda e, i: (e, i, 0)),
                  pl.BlockSpec((None, D, F), lambda e, i: (e, 0, 0)),
                  pl.BlockSpec((None, D, F), lambda e, i: (e, 0, 0)),
                  pl.BlockSpec((None, F, D), lambda e, i: (e, 0, 0))],
        out_specs=pl.BlockSpec((None, tm, D), lambda e, i: (e, i, 0)),
        out_shape=jax.ShapeDtypeStruct((E, R, D), F32),
        compiler_params=pltpu.CompilerParams(dimension_semantics=("parallel", "arbitrary"),
                                             vmem_limit_bytes=VMEM_LIMIT),
        name="expert_swiglu",
    )(xg, score, wg, wu, wd)


def _expert_choice(h2, aff, seg_start, seg_len, wg, wu, wd):
    B, T, D = h2.shape
    cap = EC_CAPACITY * seg_len // N_EXPERTS
    a = aff[:, seg_start:seg_start + seg_len, :N_EXPERTS]
    score, idx = lax.top_k(jnp.swapaxes(a, 1, 2), cap)
    rows = idx + seg_start + (jnp.arange(B, dtype=idx.dtype) * T)[:, None, None]
    rows = jnp.swapaxes(rows, 0, 1).reshape(N_EXPERTS, B * cap)
    score = jnp.swapaxes(score, 0, 1).reshape(N_EXPERTS, B * cap, 1)
    xg = jnp.take(h2.reshape(B * T, D), rows, axis=0)
    ye = _experts(xg, score, wg, wu, wd)
    return rows.reshape(-1), ye.reshape(-1, D)


def _residual_kernel(x_ref, m_ref, g2_ref, gain_ref, o_ref, *, final):
    x = x_ref[...] + g2_ref[...] * m_ref[...]
    o_ref[...] = _rms(x) * gain_ref[...] if final else x


def _residual(x1, moe, g2, gain, final, n_ctx_tiles):
    B, T, D = x1.shape
    tile = pl.BlockSpec((None, ROW_TILE, D), lambda b, i: (b, i, 0))
    mod_spec = pl.BlockSpec((None, None, 1, D), lambda b, i: (b, jnp.where(i < n_ctx_tiles, 1, 0), 0, 0))
    return pl.pallas_call(
        functools.partial(_residual_kernel, final=final),
        grid=(B, T // ROW_TILE),
        in_specs=[tile, tile, mod_spec, pl.BlockSpec((1, D), lambda b, i: (0, 0))],
        out_specs=tile,
        out_shape=jax.ShapeDtypeStruct((B, T, D), F32),
        compiler_params=pltpu.CompilerParams(dimension_semantics=("parallel", "parallel")),
        name="moe_residual_final_norm" if final else "moe_residual",
    )(x1, moe, g2, gain.reshape(1, D))


def _to_col_major(t):
    b, n = t.shape[:2]
    rows = n // GRID_W
    return t.reshape((b, rows, GRID_W) + t.shape[2:]).swapaxes(1, 2).reshape(t.shape)


def _to_row_major(t):
    b, n = t.shape[:2]
    rows = n // GRID_W
    return t.reshape((b, GRID_W, rows) + t.shape[2:]).swapaxes(1, 2).reshape(t.shape)


def _pad_cols(w, width):
    return jnp.pad(w, ((0, 0), (0, width - w.shape[1])))


def kernel(x, c, ctx, c_ctx, w_ada, b_ada, norm1_g, norm2_g, w_in, ssd_conv_w, ssd_conv_b, ssd_dt_bias, ssd_a_log, ssd_d, ssd_norm_g, hy_conv_w, hy_conv_b, hy_w1, hy_b1, hy_w2, hy_b2, hy_w3, hy_freq, hy_skip, ml_conv_w, ml_conv_b, ml_i_bias, ml_f_bias, ml_norm_g, hg_lb_logits, hg_norm_g, w_branch, w_out, w_router, w_gate, w_up, w_down, final_g):
    B, n, D = x.shape
    n_ctx = ctx.shape[1]
    T = n_ctx + n
    assert n_ctx % ROW_TILE == 0 and n % ROW_TILE == 0 and n % GRID_W == 0
    n_ctx_tiles = n_ctx // ROW_TILE
    n_ctx_chunks = n_ctx // SCAN_CHUNK
    w = BRANCH_W

    p_lb = jax.nn.softmax(hg_lb_logits.astype(F32), axis=0)
    lower_bounds = jnp.maximum(jnp.cumsum(p_lb, axis=0) - p_lb[0], 0.0)
    xc = jnp.concatenate([ctx, x], axis=1)
    depth = w_in.shape[0]
    for l in range(depth):
        last_layer = l == depth - 1
        mod = jax.nn.silu(c) @ w_ada[l] + b_ada[l]
        mod_c = jax.nn.silu(c_ctx) @ w_ada[l] + b_ada[l]
        mods = jnp.stack([mod, jnp.broadcast_to(mod_c, mod.shape)], axis=1).reshape(B, 2, 6, 1, D)
        sh1, sc1, g1, sh2, sc2, g2 = (mods[:, :, i] for i in range(6))

        wl = w_in[l].astype(BF16)
        o_ssd, o_hy, o_ml, o_hg, o_gate = 0, SSD_COLS, SSD_COLS + HY_COLS, SSD_COLS + HY_COLS + ML_COLS, \
            SSD_COLS + HY_COLS + ML_COLS + HG_COLS
        cols = lambda a, b_: wl[:, a:b_]
        z, xs, bc, dt, qk, v, o, gts = _proj(xc, sh1, sc1, norm1_g[l], [
            cols(o_ssd, o_ssd + w), cols(o_ssd + w, o_ssd + 2 * w), cols(o_ssd + 2 * w, o_ssd + w + SSD_CONV_CH),
            _pad_cols(cols(o_ssd + w + SSD_CONV_CH, o_hy), LANES),
            cols(o_ml, o_ml + 2 * w), cols(o_ml + 2 * w, o_ml + 3 * w), cols(o_ml + 3 * w, o_ml + 4 * w),
            _pad_cols(cols(o_ml + 4 * w, o_hg), LANES)], n_ctx_tiles)
        p_hy, gate_pre = _proj(xc, sh1, sc1, norm1_g[l], [cols(o_hy, o_ml), cols(o_gate, o_gate + N_BRANCHES * D)],
                               n_ctx_tiles)
        xc_cm = jnp.concatenate([xc[:, :n_ctx], _to_col_major(xc[:, n_ctx:])], axis=1)
        hq, hi, hff, hfb, hgt = _proj(xc_cm, sh1, sc1, norm1_g[l],
                                      [cols(o_hg + i * w, o_hg + (i + 1) * w) for i in range(5)], n_ctx_tiles)

        ya = _ssd_branch(z, xs, bc, dt, ssd_conv_w[l], ssd_conv_b[l], ssd_dt_bias[l], ssd_a_log[l], ssd_d[l],
                         ssd_norm_g[l], n_ctx_chunks)
        yc = _mlstm_branch(qk, v, o, gts, ml_conv_w[l], ml_conv_b[l], ml_i_bias[l], ml_f_bias[l], ml_norm_g[l],
                           n_ctx_chunks)
        yd_cm = _hgrn_branch(hq, hi, hff, hfb, hgt, lower_bounds[l], hg_norm_g[l], n_ctx_chunks)
        yd = jnp.concatenate([yd_cm[:, :n_ctx], _to_row_major(yd_cm[:, n_ctx:])], axis=1)
        hv, hx1, hx2 = _hyena_conv(p_hy, hy_conv_w[l], hy_conv_b[l], n_ctx_tiles)
        hy_params = tuple(p_[l].astype(F32) for p_ in (hy_w1, hy_b1, hy_w2, hy_b2, hy_w3, hy_freq))
        lat = lambda t: t[:, n_ctx:]
        sides, side_scale = _hyena_filter_sides(n, *hy_params)
        yb_lat = _hyena_latent(lat(hv), lat(hx1), lat(hx2), sides, side_scale, hy_skip[l])
        if last_layer:
            yb_ctx = jnp.zeros((B, n_ctx, w), F32)
        else:
            head = lambda t: t[:, :n_ctx]
            yb_ctx = _hyena_context(head(hv), head(hx1), head(hx2), _hyena_filter_time(n_ctx, *hy_params), hy_skip[l])
        yb = jnp.concatenate([yb_ctx, yb_lat], axis=1).astype(BF16)

        wr = jnp.stack(_split_bf16(_pad_cols(w_router[l].astype(F32), LANES)))
        x1, h2, aff = _merge((ya, yb, yc, yd), gate_pre, xc, g1, sh2, sc2, norm2_g[l],
                             w_branch[l].astype(BF16), w_out[l].astype(BF16), wr, n_ctx_tiles)
        wg, wu, wd = w_gate[l].astype(BF16), w_up[l].astype(BF16), w_down[l].astype(BF16)
        rows, vals = _expert_choice(h2, aff, n_ctx, n, wg, wu, wd)
        moe = jnp.zeros((B * T, D), F32).at[rows].add(vals)
        if not last_layer:
            rows_c, vals_c = _expert_choice(h2, aff, 0, n_ctx, wg, wu, wd)
            moe = moe.at[rows_c].add(vals_c)
        xc = _residual(x1, moe.reshape(B, T, D), g2, final_g, last_layer, n_ctx_tiles)
    return xc[:, n_ctx:]
```

```python
import functools
import math
import jax
import jax.numpy as jnp
from jax import lax
import numpy as np
from jax.experimental import pallas as pl
from jax.experimental.pallas import tpu as pltpu


D_MODEL = 1024
DEPTH = 2
GRID_W = 64
BRANCH_W = D_MODEL // 2
N_BRANCHES = 4
EPS = 1e-6
MASK_NEG = -1e30
LB_FLOOR = 1e-30

SSD_HEAD_DIM = 64
SSD_HEADS = BRANCH_W // SSD_HEAD_DIM
SSD_GROUPS = 2
SSD_STATE = 64
SSD_CONV_CH = BRANCH_W + 2 * SSD_GROUPS * SSD_STATE
SSD_COLS = BRANCH_W + SSD_CONV_CH + 2 * SSD_HEADS

HY_WIDTH = BRANCH_W
HY_ORDER = 2
HY_BANDS = 8
HY_DECAY_TARGET = 1e-2
HY_SHORT_PCT = 0.3
HY_LONG_PCT = 1.5
HY_COLS = (HY_ORDER + 1) * HY_WIDTH

ML_HEADS = 4
ML_HEAD_DIM = BRANCH_W // ML_HEADS
ML_COLS = 4 * BRANCH_W + 4 * ML_HEADS

HG_HEADS = 4
HG_HEAD_DIM = BRANCH_W // HG_HEADS
HG_COLS = 5 * BRANCH_W
HG_LEAF = 8

N_EXPERTS = 16
EC_CAPACITY = 2

LANES = 128
SUBLANES = 8
ROW_TILE = 256
SCAN_CHUNK = 128
VMEM_LIMIT = 56 * 1024 * 1024

F32 = jnp.float32
BF16 = jnp.bfloat16


def _dot(a, b):
    return jnp.dot(a, b, preferred_element_type=F32)


def _dot_nt(a, b):
    return lax.dot_general(a, b, (((1,), (1,)), ((), ())), preferred_element_type=F32)


def _dot_tn(a, b):
    return lax.dot_general(a, b, (((0,), (0,)), ((), ())), preferred_element_type=F32)


def _dot_mask(mask, x):
    hi = x.astype(BF16)
    rem = x - hi.astype(F32)
    mid = rem.astype(BF16)
    lo = (rem - mid.astype(F32)).astype(BF16)
    return _dot(mask, hi) + _dot(mask, mid) + _dot(mask, lo)


def _sigmoid(x):
    return 0.5 * jnp.tanh(0.5 * x) + 0.5


def _silu(x):
    return x * _sigmoid(x)


def _softplus(x):
    return jnp.maximum(x, 0.0) + jnp.log1p(jnp.exp(-jnp.abs(x)))


def _log_sigmoid(x):
    return jnp.minimum(x, 0.0) - jnp.log1p(jnp.exp(-jnp.abs(x)))


def _rms(x):
    return x * lax.rsqrt(jnp.mean(x * x, axis=-1, keepdims=True) + EPS)


def _proj_kernel(x_ref, sh_ref, sc_ref, g_ref, *refs, n_out):
    h = (_rms(x_ref[...]) * g_ref[...] * (1.0 + sc_ref[...]) + sh_ref[...]).astype(BF16)
    for w_ref, o_ref in zip(refs[:n_out], refs[n_out:]):
        o_ref[...] = _dot(h, w_ref[...])


def _proj(xc, shift, scale, gain, weights, n_ctx_tiles):
    B, T, D = xc.shape
    mod_spec = pl.BlockSpec((None, None, 1, D), lambda b, i: (b, jnp.where(i < n_ctx_tiles, 1, 0), 0, 0))
    return pl.pallas_call(
        functools.partial(_proj_kernel, n_out=len(weights)),
        grid=(B, T // ROW_TILE),
        in_specs=[pl.BlockSpec((None, ROW_TILE, D), lambda b, i: (b, i, 0)), mod_spec, mod_spec,
                  pl.BlockSpec((1, D), lambda b, i: (0, 0))]
                 + [pl.BlockSpec(w.shape, lambda b, i: (0, 0)) for w in weights],
        out_specs=[pl.BlockSpec((None, ROW_TILE, w.shape[1]), lambda b, i: (b, i, 0)) for w in weights],
        out_shape=[jax.ShapeDtypeStruct((B, T, w.shape[1]), F32) for w in weights],
        compiler_params=pltpu.CompilerParams(dimension_semantics=("parallel", "parallel"),
                                             vmem_limit_bytes=VMEM_LIMIT),
        name="norm_mod_proj",
    )(xc, shift, scale, gain.reshape(1, D), *weights)


def _scan_chunk(step, reverse, n_ctx_chunks, n_chunks):
    if not reverse:
        return step
    return jnp.where(step < n_ctx_chunks, n_ctx_chunks - 1 - step, n_chunks - 1 - (step - n_ctx_chunks))


def _scan_specs(reverse, n_ctx_chunks, n_chunks, chunk, batch):
    cidx = lambda s: _scan_chunk(s, reverse, n_ctx_chunks, n_chunks)
    per_tile = chunk // SUBLANES
    n_tiles = n_chunks * per_tile

    def main(width):
        return pl.BlockSpec((batch, chunk, width), lambda s: (0, cidx(s), 0))

    def prev(width):
        return pl.BlockSpec((batch, SUBLANES, width), lambda s: (0, jnp.maximum(cidx(s) * per_tile - 1, 0), 0))

    def nxt(width):
        return pl.BlockSpec((batch, SUBLANES, width),
                            lambda s: (0, jnp.minimum((cidx(s) + 1) * per_tile, n_tiles - 1), 0))

    def const(shape):
        return pl.BlockSpec(shape, lambda s: (0,) * len(shape))

    return main, prev, nxt, const


def _per_sample(body, n_inputs, n_shared):
    def kern(*refs):
        for b in range(refs[0].shape[0]):
            body(*[r if n_inputs <= i < n_inputs + n_shared else r.at[b] for i, r in enumerate(refs)])
    return kern


def _segment_edges(chunk_idx, n_ctx_chunks, n_chunks):
    first = jnp.logical_or(chunk_idx == 0, chunk_idx == n_ctx_chunks)
    last = jnp.logical_or(chunk_idx == n_ctx_chunks - 1, chunk_idx == n_chunks - 1)
    return first, last


def _conv3(u, u_prev_tile, u_next_tile, w, bias, first, last):
    q = u.shape[0]
    rows = lax.broadcasted_iota(jnp.int32, u.shape, 0)
    before = jnp.where(first, 0.0, u_prev_tile[SUBLANES - 1:SUBLANES, :])
    after = jnp.where(last, 0.0, u_next_tile[0:1, :])
    u_m1 = jnp.where(rows == 0, before, pltpu.roll(u, 1, 0))
    u_p1 = jnp.where(rows == q - 1, after, pltpu.roll(u, q - 1, 0))
    return w[0:1, :] * u_m1 + w[1:2, :] * u + w[2:3, :] * u_p1 + bias


def _order_masks(q, reverse):
    r = lax.broadcasted_iota(jnp.int32, (q, q), 0)
    c = lax.broadcasted_iota(jnp.int32, (q, q), 1)
    return (r <= c) if reverse else (r >= c)


def _ssd_kernel(z_ref, x_ref, xp_ref, xn_ref, bc_ref, bcp_ref, bcn_ref, dt_ref, yf_ref,
                cwx_ref, cbx_ref, cwbc_ref, cbbc_ref, dtb_ref, nega_ref, dskip_ref, ng_ref,
                out_ref, st_ref, *, reverse, n_ctx_chunks, n_chunks):
    q = x_ref.shape[0]
    step = pl.program_id(0)
    cidx = _scan_chunk(step, reverse, n_ctx_chunks, n_chunks)
    first, last = _segment_edges(cidx, n_ctx_chunks, n_chunks)

    @pl.when(step == 0)
    def _():
        st_ref[...] = jnp.zeros_like(st_ref)

    xs = _silu(_conv3(x_ref[...], xp_ref[...], xn_ref[...], cwx_ref[...], cbx_ref[...], first, last))
    bc = _silu(_conv3(bc_ref[...], bcp_ref[...], bcn_ref[...], cwbc_ref[...], cbbc_ref[...], first, last))
    dt_all = _softplus(dt_ref[...] + dtb_ref[...])
    da_all = dt_all * nega_ref[...]
    mask = _order_masks(q, reverse)
    acs = _dot_mask(mask.astype(BF16), da_all)
    acs_t = acs.T
    dt_t = dt_all.T
    edge = 0 if reverse else q - 1
    gn = SSD_STATE
    ys = []
    for g in range(SSD_GROUPS):
        bg = bc[:, g * gn:(g + 1) * gn].astype(BF16)
        cg = bc[:, (SSD_GROUPS + g) * gn:(SSD_GROUPS + g + 1) * gn].astype(BF16)
        cb = _dot_nt(cg, bg)
        for hh in range(SSD_HEADS // SSD_GROUPS):
            h = g * (SSD_HEADS // SSD_GROUPS) + hh
            col = (SSD_HEADS if reverse else 0) + h
            a_col = acs[:, col:col + 1]
            a_row = acs_t[col:col + 1, :]
            a_end = a_row[:, edge:edge + 1]
            decay = jnp.where(mask, jnp.exp(jnp.minimum(a_col - a_row, 0.0)), 0.0)
            scores = (cb * decay * dt_t[col:col + 1, :]).astype(BF16)
            xh = xs[:, h * SSD_HEAD_DIM:(h + 1) * SSD_HEAD_DIM]
            st = st_ref[h]
            y = _dot(scores, xh.astype(BF16)) + _dot(cg, st.astype(BF16)) * jnp.exp(a_col)
            to_end = jnp.exp(a_end - a_col) * dt_all[:, col:col + 1]
            st_ref[h] = jnp.exp(a_end) * st + _dot_tn(bg, (xh * to_end).astype(BF16))
            ys.append(y)
    y = jnp.concatenate(ys, axis=1)
    if not reverse:
        out_ref[...] = y
    else:
        y = (y + yf_ref[...] + dskip_ref[...] * xs) * _silu(z_ref[...])
        out_ref[...] = (_rms(y) * ng_ref[...]).astype(out_ref.dtype)


def _ssd_pass(z, x, bc, dt, y_fwd, consts, reverse, n_ctx_chunks):
    B, T, _ = x.shape
    n_chunks = T // SCAN_CHUNK
    main, prev, nxt, const = _scan_specs(reverse, n_ctx_chunks, n_chunks, SCAN_CHUNK, B)
    w = BRANCH_W
    wbc = 2 * SSD_GROUPS * SSD_STATE
    in_specs = [main(w), main(w), prev(w), nxt(w), main(wbc), prev(wbc), nxt(wbc), main(LANES), main(w),
                const((3, w)), const((1, w)), const((3, wbc)), const((1, wbc)), const((1, LANES)),
                const((1, LANES)), const((1, w)), const((1, w))]
    body = functools.partial(_ssd_kernel, reverse=reverse, n_ctx_chunks=n_ctx_chunks, n_chunks=n_chunks)
    return pl.pallas_call(
        _per_sample(body, 9, 8),
        grid=(n_chunks,),
        in_specs=in_specs,
        out_specs=main(w),
        out_shape=jax.ShapeDtypeStruct((B, T, w), BF16 if reverse else F32),
        scratch_shapes=[pltpu.VMEM((B, SSD_HEADS, SSD_STATE, SSD_HEAD_DIM), F32)],
        compiler_params=pltpu.CompilerParams(dimension_semantics=("arbitrary",),
                                             vmem_limit_bytes=VMEM_LIMIT),
        name="ssd_bwd_finish" if reverse else "ssd_fwd",
    )(z, x, x, x, bc, bc, bc, dt, y_fwd, *consts)


def _ssd_branch(z, x, bc, dt, conv_w, conv_b, dt_bias, a_log, d_skip, norm_g, n_ctx_chunks):
    w = BRANCH_W
    pad = LANES - 2 * SSD_HEADS
    dtb = jnp.pad(dt_bias.astype(F32).reshape(1, -1), ((0, 0), (0, pad)))
    nega = jnp.pad(-jnp.exp(a_log.astype(F32)).reshape(1, -1), ((0, 0), (0, pad)))
    consts = (conv_w[:, :w], conv_b[:w].reshape(1, w), conv_w[:, w:], conv_b[w:].reshape(1, -1), dtb, nega,
              jnp.repeat(d_skip.astype(F32), SSD_HEAD_DIM).reshape(1, w), norm_g.reshape(1, w))
    y_f = _ssd_pass(z, x, bc, dt, x, consts, False, n_ctx_chunks)
    return _ssd_pass(z, x, bc, dt, y_f, consts, True, n_ctx_chunks)


def _mlstm_kernel(qk_ref, qkp_ref, qkn_ref, v_ref, o_ref, gt_ref, hf_ref, cw_ref, cb_ref, gb_ref, ng_ref,
                  out_ref, c_ref, m_ref, *, reverse, n_ctx_chunks, n_chunks):
    q = qk_ref.shape[0]
    dh = ML_HEAD_DIM
    step = pl.program_id(0)
    cidx = _scan_chunk(step, reverse, n_ctx_chunks, n_chunks)
    first, last = _segment_edges(cidx, n_ctx_chunks, n_chunks)

    @pl.when(step == 0)
    def _():
        c_ref[...] = jnp.zeros_like(c_ref)
        m_ref[...] = jnp.zeros_like(m_ref)

    qk = _silu(_conv3(qk_ref[...], qkp_ref[...], qkn_ref[...], cw_ref[...], cb_ref[...], first, last))
    gates = gt_ref[...] + gb_ref[...]
    logf_all = _log_sigmoid(gates)
    mask = _order_masks(q, reverse)
    bcum = _dot_mask(mask.astype(BF16), logf_all)
    bcum_t = bcum.T
    gates_t = gates.T
    edge = 0 if reverse else q - 1
    lane = lax.broadcasted_iota(jnp.int32, (q, dh), 1)
    ones_col = jnp.where(lane == 0, 1.0, 0.0).astype(BF16)
    hs = []
    for h in range(ML_HEADS):
        li = (ML_HEADS if reverse else 0) + h
        lf = 2 * ML_HEADS + li
        qh = qk[:, h * dh:(h + 1) * dh].astype(BF16)
        kh = qk[:, BRANCH_W + h * dh:BRANCH_W + (h + 1) * dh] * (dh ** -0.5)
        v_aug = jnp.concatenate([v_ref[:, h * dh:(h + 1) * dh].astype(BF16), ones_col], axis=1)
        b_col = bcum[:, lf:lf + 1]
        b_row = bcum_t[lf:lf + 1, :]
        i_col = gates[:, li:li + 1]
        i_row = gates_t[li:li + 1, :]
        b_tot = b_row[:, edge:edge + 1]
        m_prev = m_ref[h:h + 1, 0:1]
        c_prev = c_ref[h]
        dmat = jnp.where(mask, b_col - b_row + i_row, MASK_NEG)
        inter = b_col + m_prev
        m_t = jnp.maximum(inter, jnp.max(dmat, axis=1, keepdims=True))
        wgt = jnp.where(mask, jnp.exp(jnp.minimum(dmat - m_t, 0.0)), 0.0) * _dot_nt(qh, kh.astype(BF16))
        w_int = jnp.exp(inter - m_t)
        cross = _dot(qh, c_prev.astype(BF16))
        num = _dot(wgt.astype(BF16), v_aug[:, :dh]) + w_int * cross[:, :dh]
        den = jnp.sum(wgt, axis=1, keepdims=True) + w_int * cross[:, dh:dh + 1]
        hs.append(num / jnp.maximum(jnp.abs(den), jnp.exp(-m_t)))
        a = b_tot - b_col + i_col
        m_new = jnp.maximum(b_tot + m_prev, jnp.max(a, axis=0, keepdims=True))
        kw = (kh * jnp.exp(a - m_new)).astype(BF16)
        c_ref[h] = jnp.exp(b_tot + m_prev - m_new) * c_prev + _dot_tn(kw, v_aug)
        m_ref[h:h + 1, :] = jnp.broadcast_to(m_new, (1, LANES))
    if not reverse:
        out_ref[...] = jnp.concatenate(hs, axis=1)
    else:
        hf = hf_ref[...]
        outs = [_rms(hs[h] + hf[:, h * dh:(h + 1) * dh]) for h in range(ML_HEADS)]
        out_ref[...] = (_sigmoid(o_ref[...]) * jnp.concatenate(outs, axis=1) * ng_ref[...]).astype(out_ref.dtype)


def _mlstm_pass(qk, v, o, gates, h_fwd, consts, reverse, n_ctx_chunks):
    B, T, _ = v.shape
    n_chunks = T // SCAN_CHUNK
    main, prev, nxt, const = _scan_specs(reverse, n_ctx_chunks, n_chunks, SCAN_CHUNK, B)
    w = BRANCH_W
    in_specs = [main(2 * w), prev(2 * w), nxt(2 * w), main(w), main(w), main(LANES), main(w),
                const((3, 2 * w)), const((1, 2 * w)), const((1, LANES)), const((1, w))]
    body = functools.partial(_mlstm_kernel, reverse=reverse, n_ctx_chunks=n_ctx_chunks, n_chunks=n_chunks)
    return pl.pallas_call(
        _per_sample(body, 7, 4),
        grid=(n_chunks,),
        in_specs=in_specs,
        out_specs=main(w),
        out_shape=jax.ShapeDtypeStruct((B, T, w), BF16 if reverse else F32),
        scratch_shapes=[pltpu.VMEM((B, ML_HEADS, ML_HEAD_DIM, 2 * ML_HEAD_DIM), F32),
                        pltpu.VMEM((B, SUBLANES, LANES), F32)],
        compiler_params=pltpu.CompilerParams(dimension_semantics=("arbitrary",),
                                             vmem_limit_bytes=VMEM_LIMIT),
        name="mlstm_bwd_finish" if reverse else "mlstm_fwd",
    )(qk, qk, qk, v, o, gates, h_fwd, *consts)


def _mlstm_branch(qk, v, o, gates, conv_w, conv_b, i_bias, f_bias, norm_g, n_ctx_chunks):
    w = BRANCH_W
    gb = jnp.concatenate([i_bias.astype(F32).reshape(-1), f_bias.astype(F32).reshape(-1)])
    gb = jnp.pad(gb.reshape(1, -1), ((0, 0), (0, LANES - 4 * ML_HEADS)))
    consts = (conv_w, conv_b.reshape(1, 2 * w), gb, norm_g.reshape(1, w))
    h_f = _mlstm_pass(qk, v, o, gates, v, consts, False, n_ctx_chunks)
    return _mlstm_pass(qk, v, o, gates, h_f, consts, True, n_ctx_chunks)


def _hgrn_level_matrix(q, block, reverse):
    t = lax.broadcasted_iota(jnp.int32, (q, q), 0)
    r = lax.broadcasted_iota(jnp.int32, (q, q), 1)
    blk = t // block
    start = blk * block
    end = start + block - 1
    odd = (blk % 2) == 1
    if not reverse:
        lo = jnp.where(odd, start, t + 1)
        hi = jnp.where(odd, t, end)
    else:
        lo = jnp.where(odd, start, t)
        hi = jnp.where(odd, t - 1, end)
    return jnp.where((r >= lo) & (r <= hi), 1.0, 0.0)


def _hgrn_select_matrix(q, reverse):
    mats = [_order_masks(q, reverse).astype(F32)]
    block = HG_LEAF
    while block < q:
        mats.append(_hgrn_level_matrix(q, block, reverse))
        block *= 2
    return jnp.concatenate(mats, axis=0).astype(BF16)


def _hgrn_kernel(q_ref, v_ref, f_ref, g_ref, of_ref, lb_ref, ng_ref, sel_ref, hsum_ref, out_ref, st_ref, *, reverse):
    q, width = q_ref.shape
    dh = HG_HEAD_DIM
    step = pl.program_id(0)

    @pl.when(step == 0)
    def _():
        st_ref[...] = jnp.zeros_like(st_ref)

    lb = lb_ref[...]
    pre = f_ref[...]
    log_lb = jnp.log(jnp.maximum(lb, LB_FLOOR))
    log_ub = jnp.log1p(-lb)
    e_pre = jnp.exp(-jnp.abs(pre))
    lo = log_ub + jnp.minimum(pre, 0.0) - jnp.log1p(e_pre)
    logf = jnp.maximum(log_lb, lo) + jnp.log1p(jnp.exp(-jnp.abs(log_lb - lo)))
    key = (1.0 - lb) * (jnp.where(pre >= 0.0, e_pre, 1.0) / (1.0 + e_pre))
    qv = _silu(q_ref[...])
    sums = _dot_mask(sel_ref[...], logf)
    bq = sums[:q]
    vv = v_ref[...]

    leaves = (q // HG_LEAF, HG_LEAF, width)
    b3, q3, k3, v3 = bq.reshape(leaves), qv.reshape(leaves), key.reshape(leaves), vv.reshape(leaves)
    tt = lax.broadcasted_iota(jnp.int32, leaves, 1)
    o_leaf = jnp.zeros(leaves, F32)
    for s in range(HG_LEAF):
        ok = (tt <= s) if reverse else (tt >= s)
        e = jnp.exp(jnp.where(ok, b3 - b3[:, s:s + 1, :], MASK_NEG))
        prod = (q3 * e * k3[:, s:s + 1, :]).reshape(q, width).astype(BF16)
        att = _dot(prod, hsum_ref[...]).reshape(leaves)
        o_leaf = o_leaf + att * v3[:, s:s + 1, :]
    o_leaf = o_leaf.reshape(q, width)

    t_idx = lax.broadcasted_iota(jnp.int32, (q, q), 0)
    s_idx = lax.broadcasted_iota(jnp.int32, (q, q), 1)
    row_t = lax.broadcasted_iota(jnp.int32, (q, dh), 0)
    levels = []
    block = HG_LEAF
    while block < q:
        fac = jnp.exp(sums[(len(levels) + 1) * q:(len(levels) + 2) * q])
        tb, sb = t_idx // block, s_idx // block
        if not reverse:
            pair = ((tb % 2) == 1) & (sb == tb - 1)
            is_query = ((row_t // block) % 2) == 1
        else:
            pair = ((tb % 2) == 0) & (sb == tb + 1)
            is_query = ((row_t // block) % 2) == 0
        levels.append((fac, pair, is_query))
        block *= 2
    edge = 0 if reverse else q - 1
    b_end = bq[edge:edge + 1, :]
    q_in = qv * jnp.exp(bq)
    k_out = key * jnp.exp(b_end - bq)
    outs = []
    for h in range(HG_HEADS):
        sl = slice(h * dh, (h + 1) * dh)
        att = jnp.zeros((q, q), F32)
        for fac, pair, is_query in levels:
            qt = jnp.where(is_query, qv[:, sl] * fac[:, sl], 0.0).astype(BF16)
            kt = jnp.where(is_query, 0.0, key[:, sl] * fac[:, sl]).astype(BF16)
            att = att + jnp.where(pair, _dot_nt(qt, kt), 0.0)
        vh = vv[:, sl].astype(BF16)
        st = st_ref[h]
        o = o_leaf[:, sl] + _dot(att.astype(BF16), vh) + _dot_nt(q_in[:, sl].astype(BF16), st.astype(BF16))
        st_ref[h] = jnp.exp(b_end[:, sl]) * st + _dot_tn(vh, k_out[:, sl].astype(BF16))
        outs.append(o)
    if not reverse:
        out_ref[...] = jnp.concatenate(outs, axis=1)
    else:
        of = of_ref[...]
        fin = [_rms(outs[h] + of[:, h * dh:(h + 1) * dh]) for h in range(HG_HEADS)]
        out_ref[...] = (jnp.concatenate(fin, axis=1) * ng_ref[...] * _sigmoid(g_ref[...])).astype(out_ref.dtype)


def _hgrn_pass(qr, v, f, g, o_fwd, lb, norm_g, reverse, n_ctx_chunks):
    B, T, w = v.shape
    n_chunks = T // SCAN_CHUNK
    main, _, _, const = _scan_specs(reverse, n_ctx_chunks, n_chunks, SCAN_CHUNK, B)
    sel = _hgrn_select_matrix(SCAN_CHUNK, reverse)
    head_of = jnp.arange(w) // HG_HEAD_DIM
    head_sum = (head_of[:, None] == head_of[None, :]).astype(BF16)
    return pl.pallas_call(
        _per_sample(functools.partial(_hgrn_kernel, reverse=reverse), 5, 4),
        grid=(n_chunks,),
        in_specs=[main(w), main(w), main(w), main(w), main(w), const((1, w)), const((1, w)), const(sel.shape),
                  const((w, w))],
        out_specs=main(w),
        out_shape=jax.ShapeDtypeStruct((B, T, w), BF16 if reverse else F32),
        scratch_shapes=[pltpu.VMEM((B, HG_HEADS, HG_HEAD_DIM, HG_HEAD_DIM), F32)],
        compiler_params=pltpu.CompilerParams(dimension_semantics=("arbitrary",),
                                             vmem_limit_bytes=VMEM_LIMIT),
        name="hgrn_bwd_finish" if reverse else "hgrn_fwd",
    )(qr, v, f, g, o_fwd, lb.reshape(1, w), norm_g.reshape(1, w), sel, head_sum)


def _hgrn_branch(qr, v, f_fwd, f_bwd, g, lb, norm_g, n_ctx_chunks):
    o_f = _hgrn_pass(qr, v, f_fwd, g, v, lb, norm_g, False, n_ctx_chunks)
    return _hgrn_pass(qr, v, f_bwd, g, o_f, lb, norm_g, True, n_ctx_chunks)


def _hyena_filter_hidden(n, w1, b1, w2, b2, freq):
    pos = jnp.arange(n, dtype=F32)
    t = pos / max(n - 1, 1)
    bands = jnp.arange(1, HY_BANDS + 1, dtype=F32)
    ang = (2.0 * math.pi / n) * pos[:, None] * bands[None, :]
    feats = jnp.concatenate([t[:, None], jnp.cos(ang), jnp.sin(ang)], axis=-1)
    hid = jnp.sin(freq[0] * (feats @ w1 + b1))
    return jnp.sin(freq[1] * (hid @ w2 + b2)), t


def _hyena_deltas():
    return jnp.abs(jnp.linspace(math.log(HY_DECAY_TARGET) / HY_LONG_PCT,
                                math.log(HY_DECAY_TARGET) / HY_SHORT_PCT, HY_WIDTH, dtype=F32))


def _hyena_filter_time(n, w1, b1, w2, b2, w3, freq):
    hid, t = _hyena_filter_hidden(n, w1, b1, w2, b2, freq)
    h = (hid @ w3).reshape(n, HY_ORDER, 2, HY_WIDTH)
    h = h * jnp.exp(-t[:, None] * _hyena_deltas()[None, :])[:, None, None, :]
    two_sided = jnp.concatenate([h[:, :, 0], jnp.zeros((1, HY_ORDER, HY_WIDTH), F32),
                                 jnp.flip(h[1:, :, 1], axis=0)], axis=0)
    return two_sided / (jnp.sum(jnp.abs(two_sided), axis=0, keepdims=True) + EPS)


def _filter_kernel(hid_ref, t_ref, wh_ref, wl_ref, delta_ref, o_ref):
    hid = hid_ref[...]
    hid_hi, hid_lo = _split_bf16(hid)
    h = _dot(hid_hi, wh_ref[...]) + _dot(hid_lo, wh_ref[...]) + _dot(hid_hi, wl_ref[...])
    decay = jnp.exp(-t_ref[...] * delta_ref[...])
    wd = HY_WIDTH
    lag0 = jnp.logical_and(pl.program_id(0) == 0, lax.broadcasted_iota(jnp.int32, decay.shape, 0) == 0)
    for g in range(2 * HY_ORDER):
        val = h[:, g * wd:(g + 1) * wd] * decay
        o_ref[g] = jnp.where(lag0, 0.0, val) if g % 2 == 1 else val


def _hyena_filter_sides(n, w1, b1, w2, b2, w3, freq):
    hid, t = _hyena_filter_hidden(n, w1, b1, w2, b2, freq)
    wh, wl = _split_bf16(w3)
    g4, wd = 2 * HY_ORDER, HY_WIDTH
    sides = pl.pallas_call(
        _filter_kernel,
        grid=(n // ROW_TILE,),
        in_specs=[pl.BlockSpec((ROW_TILE, hid.shape[1]), lambda i: (i, 0)), pl.BlockSpec((ROW_TILE, 1), lambda i: (i, 0)),
                  pl.BlockSpec(wh.shape, lambda i: (0, 0)), pl.BlockSpec(wl.shape, lambda i: (0, 0)),
                  pl.BlockSpec((1, wd), lambda i: (0, 0))],
        out_specs=pl.BlockSpec((g4, ROW_TILE, wd), lambda i: (0, i, 0)),
        out_shape=jax.ShapeDtypeStruct((g4, n, wd), F32),
        compiler_params=pltpu.CompilerParams(dimension_semantics=("parallel",)),
        name="hyena_filter_sides",
    )(hid, t.reshape(n, 1), wh, wl, _hyena_deltas().reshape(1, wd))
    norm = jnp.sum(jnp.abs(sides), axis=1).reshape(HY_ORDER, 2, wd).sum(axis=1)
    return sides, 1.0 / (norm + EPS)


def _hyena_conv_kernel(p_ref, pp_ref, pn_ref, w_ref, b_ref, v_ref, x1_ref, x2_ref, *, n_ctx_tiles, n_tiles):
    first, last = _segment_edges(pl.program_id(0), n_ctx_tiles, n_tiles)
    u = _conv3(p_ref[...], pp_ref[...], pn_ref[...], w_ref[...], b_ref[...], first, last)
    wd = HY_WIDTH
    v_ref[...] = u[:, :wd]
    x1_ref[...] = u[:, wd:2 * wd]
    x2_ref[...] = u[:, 2 * wd:]


def _hyena_conv(p, conv_w, conv_b, n_ctx_tiles):
    B, T, C = p.shape
    n_tiles = T // ROW_TILE
    main, prev, nxt, const = _scan_specs(False, n_ctx_tiles, n_tiles, ROW_TILE, B)
    body = functools.partial(_hyena_conv_kernel, n_ctx_tiles=n_ctx_tiles, n_tiles=n_tiles)
    return pl.pallas_call(
        _per_sample(body, 3, 2),
        grid=(n_tiles,),
        in_specs=[main(C), prev(C), nxt(C), const((3, C)), const((1, C))],
        out_specs=[main(HY_WIDTH)] * 3,
        out_shape=[jax.ShapeDtypeStruct((B, T, HY_WIDTH), F32)] * 3,
        compiler_params=pltpu.CompilerParams(dimension_semantics=("parallel",), vmem_limit_bytes=VMEM_LIMIT),
        name="hyena_short_conv",
    )(p, p, p, conv_w, conv_b.reshape(1, C))


DFT_N2 = 256
DFT_ROW_TILE = SUBLANES


def _split_bf16(x):
    hi = x.astype(BF16)
    return hi, (x - hi.astype(F32)).astype(BF16)


def _dft_tables(n):
    L = 2 * n
    n1_full = L // DFT_N2
    nh = n1_full // 2
    nk = nh + 1
    nkp = -(-nk // SUBLANES) * SUBLANES
    k1 = jnp.arange(nkp, dtype=jnp.int32)
    valid = (k1 < nk)[:, None]

    def stage1(rows):
        n1 = jnp.arange(rows, dtype=jnp.int32)
        ang = (2.0 * math.pi / n1_full) * ((k1[:, None] * n1[None, :]) % n1_full).astype(F32)
        return jnp.concatenate([jnp.where(valid, jnp.cos(ang), 0.0), jnp.where(valid, -jnp.sin(ang), 0.0)], axis=0)

    n1 = jnp.arange(nh, dtype=jnp.int32)
    ang = (2.0 * math.pi / n1_full) * ((n1[:, None] * k1[None, :]) % n1_full).astype(F32)
    ck = jnp.where((k1 == 0) | (k1 == nh), 1.0, 2.0) * jnp.where(k1 < nk, 1.0, 0.0) / L
    stage3 = jnp.concatenate([jnp.cos(ang) * ck[None, :], -jnp.sin(ang) * ck[None, :]], axis=1)
    n2 = jnp.arange(DFT_N2, dtype=jnp.int32)
    idx = (n2[None, :, None] * (k1[:, None, None] + n1_full * n2[None, None, :])) % L
    ang2 = (2.0 * math.pi / L) * idx.astype(F32)
    gr, gi = jnp.cos(ang2), -jnp.sin(ang2)
    grt, git = jnp.swapaxes(gr, 1, 2), jnp.swapaxes(gi, 1, 2)
    m_fwd = jnp.concatenate([jnp.concatenate([grt, -git], axis=2), jnp.concatenate([git, grt], axis=2)], axis=1)
    m_inv = jnp.swapaxes(m_fwd, 1, 2)
    eye = jnp.eye(DFT_ROW_TILE, dtype=F32)
    return dict(nh=nh, nk=nk, nkp=nkp, f1=jnp.kron(stage1(nh), eye).astype(BF16),
                f3=jnp.kron(stage3, eye).astype(BF16), m_fwd=m_fwd.astype(BF16), m_inv=m_inv.astype(BF16))


def _dft_in_kernel(u_ref, f_ref, o_ref):
    rows, t2, wd = u_ref.shape
    nkp = o_ref.shape[1]
    r = _dot(f_ref[...], u_ref[...].reshape(rows * t2, wd).astype(BF16))
    o_ref[0] = r[:nkp * t2].reshape(nkp, t2, wd)
    o_ref[1] = r[nkp * t2:].reshape(nkp, t2, wd)


def _dft_in(u, f1):
    G, rows, n2, wd = u.shape
    t2 = DFT_ROW_TILE
    nkp = f1.shape[0] // (2 * t2)
    return pl.pallas_call(
        _dft_in_kernel,
        grid=(G, n2 // t2),
        in_specs=[pl.BlockSpec((None, rows, t2, wd), lambda g, j: (g, 0, j, 0)),
                  pl.BlockSpec(f1.shape, lambda g, j: (0, 0))],
        out_specs=pl.BlockSpec((None, 2, nkp, t2, wd), lambda g, j: (g, 0, 0, j, 0)),
        out_shape=jax.ShapeDtypeStruct((G, 2, nkp, n2, wd), F32),
        compiler_params=pltpu.CompilerParams(dimension_semantics=("parallel", "parallel"),
                                             vmem_limit_bytes=VMEM_LIMIT),
        name="hyena_dft_rows",
    )(u, f1)


def _spectrum_kernel(af_ref, ab_ref, sc_ref, m_ref, o_ref, *, nk):
    k1 = pl.program_id(0)
    half = DFT_N2

    @pl.when(k1 < nk)
    def _():
        m = m_ref[...]
        uf = _dot(m, jnp.concatenate([af_ref[0], af_ref[1]], axis=0).astype(BF16))
        ub = _dot(m, jnp.concatenate([ab_ref[0], ab_ref[1]], axis=0).astype(BF16))
        o_ref[0] = (uf[:half] + ub[:half]) * sc_ref[...]
        o_ref[1] = (uf[half:] - ub[half:]) * sc_ref[...]

    @pl.when(k1 >= nk)
    def _():
        o_ref[...] = jnp.zeros_like(o_ref)


def _spectrum(a, scale, tables):
    _, _, nkp, n2, wd = a.shape
    slab = lambda g_of: pl.BlockSpec((None, 2, None, n2, wd), g_of)
    return pl.pallas_call(
        functools.partial(_spectrum_kernel, nk=tables["nk"]),
        grid=(nkp, HY_ORDER),
        in_specs=[slab(lambda k, o: (2 * o, 0, k, 0, 0)), slab(lambda k, o: (2 * o + 1, 0, k, 0, 0)),
                  pl.BlockSpec((None, 1, wd), lambda k, o: (o, 0, 0)),
                  pl.BlockSpec((None, 2 * n2, 2 * n2), lambda k, o: (k, 0, 0))],
        out_specs=slab(lambda k, o: (o, 0, k, 0, 0)),
        out_shape=jax.ShapeDtypeStruct((HY_ORDER, 2, nkp, n2, wd), F32),
        compiler_params=pltpu.CompilerParams(dimension_semantics=("parallel", "parallel"),
                                             vmem_limit_bytes=VMEM_LIMIT),
        name="hyena_spectrum",
    )(a, a, scale, tables["m_fwd"])


def _dft_mid_kernel(a_ref, h_ref, mf_ref, mi_ref, o_ref, *, nk):
    k1 = pl.program_id(0)
    half = DFT_N2

    @pl.when(k1 < nk)
    def _():
        u = _dot(mf_ref[...], jnp.concatenate([a_ref[0], a_ref[1]], axis=0).astype(BF16))
        ur, ui = u[:half], u[half:]
        hr, hi = h_ref[0], h_ref[1]
        v = jnp.concatenate([ur * hr - ui * hi, ur * hi + ui * hr], axis=0)
        y = _dot(mi_ref[...], v.astype(BF16))
        o_ref[0] = y[:half]
        o_ref[1] = y[half:]

    @pl.when(k1 >= nk)
    def _():
        o_ref[...] = jnp.zeros_like(o_ref)


def _dft_mid(a, spec, order, tables):
    G, _, nkp, n2, wd = a.shape
    slab = lambda g_of: pl.BlockSpec((None, 2, None, n2, wd), g_of)
    mat = pl.BlockSpec((None, 2 * n2, 2 * n2), lambda k, g: (k, 0, 0))
    return pl.pallas_call(
        functools.partial(_dft_mid_kernel, nk=tables["nk"]),
        grid=(nkp, G),
        in_specs=[slab(lambda k, g: (g, 0, k, 0, 0)), slab(lambda k, g: (order, 0, k, 0, 0)), mat, mat],
        out_specs=slab(lambda k, g: (g, 0, k, 0, 0)),
        out_shape=jax.ShapeDtypeStruct(a.shape, F32),
        compiler_params=pltpu.CompilerParams(dimension_semantics=("parallel", "parallel"),
                                             vmem_limit_bytes=VMEM_LIMIT),
        name="hyena_dft_mid",
    )(a, spec, tables["m_fwd"], tables["m_inv"])


def _dft_out_kernel(b_ref, u_ref, x_ref, skip_ref, f_ref, o_ref):
    rows, t2, wd = u_ref.shape
    nkp = b_ref.shape[1]
    bb = jnp.concatenate([b_ref[0].reshape(nkp * t2, wd), b_ref[1].reshape(nkp * t2, wd)], axis=0)
    y = _dot(f_ref[...], bb.astype(BF16)).reshape(rows, t2, wd)
    o_ref[...] = x_ref[...] * (y + u_ref[...] * skip_ref[...])


def _dft_out(bq, u, xg, skip_row, f3):
    G, rows, n2, wd = u.shape
    nkp = bq.shape[2]
    t2 = DFT_ROW_TILE
    tile = pl.BlockSpec((None, rows, t2, wd), lambda g, j: (g, 0, j, 0))
    return pl.pallas_call(
        _dft_out_kernel,
        grid=(G, n2 // t2),
        in_specs=[pl.BlockSpec((None, 2, nkp, t2, wd), lambda g, j: (g, 0, 0, j, 0)), tile, tile,
                  pl.BlockSpec((1, wd), lambda g, j: (0, 0)), pl.BlockSpec(f3.shape, lambda g, j: (0, 0))],
        out_specs=tile,
        out_shape=jax.ShapeDtypeStruct(u.shape, F32),
        compiler_params=pltpu.CompilerParams(dimension_semantics=("parallel", "parallel"),
                                             vmem_limit_bytes=VMEM_LIMIT),
        name="hyena_dft_rows_inverse",
    )(bq, u, xg, skip_row, f3)


def _hyena_latent(v, x1, x2, sides, scale, skip):
    B, n, wd = v.shape
    tb = _dft_tables(n)
    view = lambda t: t.reshape(t.shape[0], tb["nh"], DFT_N2, wd)
    spec = _spectrum(_dft_in(view(sides), tb["f1"]), scale.reshape(HY_ORDER, 1, wd), tb)
    z = view(v)
    for o, xg in enumerate((x1, x2)):
        bq = _dft_mid(_dft_in(z, tb["f1"]), spec, o, tb)
        z = _dft_out(bq, z, view(xg), skip[o].astype(F32).reshape(1, wd), tb["f3"])
    return z.reshape(B, n, wd)


def _hyena_context(v, x1, x2, filt, skip):
    n = v.shape[1]
    spec = jnp.fft.rfft(filt, axis=0)
    z = v
    for o, xg in enumerate((x1, x2)):
        zf = jnp.fft.rfft(z, n=2 * n, axis=1)
        y = jnp.fft.irfft(zf * spec[:, o], n=2 * n, axis=1)[:, :n]
        z = xg * (y + z * skip[o].astype(F32))
    return z


def _merge_kernel(ya_ref, yb_ref, yc_ref, yd_ref, gate_ref, x_ref, g1_ref, sh_ref, sc_ref, ng_ref,
                  wb_ref, wo_ref, wr_ref, x1_ref, h2_ref, aff_ref):
    d = x_ref.shape[1]
    merged = None
    for i, y_ref in enumerate((ya_ref, yb_ref, yc_ref, yd_ref)):
        term = _sigmoid(gate_ref[:, i * d:(i + 1) * d]) * _dot(y_ref[...], wb_ref[i])
        merged = term if merged is None else merged + term
    x1 = x_ref[...] + g1_ref[...] * _dot(merged.astype(BF16), wo_ref[...])
    x1_ref[...] = x1
    h2 = _rms(x1) * ng_ref[...] * (1.0 + sc_ref[...]) + sh_ref[...]
    h2_ref[...] = h2.astype(h2_ref.dtype)
    h2_hi, h2_lo = _split_bf16(h2)
    logits = _dot(h2_hi, wr_ref[0]) + _dot(h2_lo, wr_ref[0]) + _dot(h2_hi, wr_ref[1])
    lane = lax.broadcasted_iota(jnp.int32, logits.shape, 1)
    logits = jnp.where(lane < N_EXPERTS, logits, MASK_NEG)
    e = jnp.exp(logits - jnp.max(logits, axis=1, keepdims=True))
    aff_ref[...] = e / jnp.sum(e, axis=1, keepdims=True)


def _merge(ys, gates, xc, g1, shift, scale, gain, wb, wo, wr, n_ctx_tiles):
    B, T, D = xc.shape
    w = BRANCH_W
    tile = lambda width: pl.BlockSpec((None, ROW_TILE, width), lambda b, i: (b, i, 0))
    mod_spec = pl.BlockSpec((None, None, 1, D), lambda b, i: (b, jnp.where(i < n_ctx_tiles, 1, 0), 0, 0))
    const = lambda shape: pl.BlockSpec(shape, lambda b, i: (0,) * len(shape))
    return pl.pallas_call(
        _merge_kernel,
        grid=(B, T // ROW_TILE),
        in_specs=[tile(w), tile(w), tile(w), tile(w), tile(N_BRANCHES * D), tile(D), mod_spec, mod_spec, mod_spec,
                  const((1, D)), const(wb.shape), const(wo.shape), const(wr.shape)],
        out_specs=[tile(D), tile(D), tile(LANES)],
        out_shape=[jax.ShapeDtypeStruct((B, T, D), F32), jax.ShapeDtypeStruct((B, T, D), BF16),
                   jax.ShapeDtypeStruct((B, T, LANES), F32)],
        compiler_params=pltpu.CompilerParams(dimension_semantics=("parallel", "parallel"),
                                             vmem_limit_bytes=VMEM_LIMIT),
        name="merge_out_router",
    )(*ys, gates, xc, g1, shift, scale, gain.reshape(1, D), wb, wo, wr)


def _expert_kernel(x_ref, s_ref, wg_ref, wu_ref, wd_ref, o_ref):
    x = x_ref[...]
    hid = _silu(_dot(x, wg_ref[...])) * _dot(x, wu_ref[...])
    o_ref[...] = _dot(hid.astype(BF16), wd_ref[...]) * s_ref[...]


def _experts(xg, score, wg, wu, wd):
    E, R, D = xg.shape
    F = wg.shape[2]
    tm = min(R, 512)
    return pl.pallas_call(
        _expert_kernel,
        grid=(E, R // tm),
        in_specs=[pl.BlockSpec((None, tm, D), lambda e, i: (e, i, 0)),
                  pl.BlockSpec((None, tm, 1), lambda e, i: (e, i, 0)),
                  pl.BlockSpec((None, D, F), lambda e, i: (e, 0, 0)),
                  pl.BlockSpec((None, D, F), lambda e, i: (e, 0, 0)),
                  pl.BlockSpec((None, F, D), lambda e, i: (e, 0, 0))],
        out_specs=pl.BlockSpec((None, tm, D), lambda e, i: (e, i, 0)),
        out_shape=jax.ShapeDtypeStruct((E, R, D), F32),
        compiler_params=pltpu.CompilerParams(dimension_semantics=("parallel", "arbitrary"),
                                             vmem_limit_bytes=VMEM_LIMIT),
        name="expert_swiglu",
    )(xg, score, wg, wu, wd)


def _expert_choice(h2, aff, seg_start, seg_len, wg, wu, wd):
    B, T, D = h2.shape
    cap = EC_CAPACITY * seg_len // N_EXPERTS
    a = aff[:, seg_start:seg_start + seg_len, :N_EXPERTS]
    score, idx = lax.top_k(jnp.swapaxes(a, 1, 2), cap)
    rows = idx + seg_start + (jnp.arange(B, dtype=idx.dtype) * T)[:, None, None]
    rows = jnp.swapaxes(rows, 0, 1).reshape(N_EXPERTS, B * cap)
    score = jnp.swapaxes(score, 0, 1).reshape(N_EXPERTS, B * cap, 1)
    xg = jnp.take(h2.reshape(B * T, D), rows, axis=0)
    ye = _experts(xg, score, wg, wu, wd)
    return rows.reshape(-1), ye.reshape(-1, D)


def _residual_kernel(x_ref, m_ref, g2_ref, gain_ref, o_ref, *, final):
    x = x_ref[...] + g2_ref[...] * m_ref[...]
    o_ref[...] = _rms(x) * gain_ref[...] if final else x


def _residual(x1, moe, g2, gain, final, n_ctx_tiles):
    B, T, D = x1.shape
    tile = pl.BlockSpec((None, ROW_TILE, D), lambda b, i: (b, i, 0))
    mod_spec = pl.BlockSpec((None, None, 1, D), lambda b, i: (b, jnp.where(i < n_ctx_tiles, 1, 0), 0, 0))
    return pl.pallas_call(
        functools.partial(_residual_kernel, final=final),
        grid=(B, T // ROW_TILE),
        in_specs=[tile, tile, mod_spec, pl.BlockSpec((1, D), lambda b, i: (0, 0))],
        out_specs=tile,
        out_shape=jax.ShapeDtypeStruct((B, T, D), F32),
        compiler_params=pltpu.CompilerParams(dimension_semantics=("parallel", "parallel")),
        name="moe_residual_final_norm" if final else "moe_residual",
    )(x1, moe, g2, gain.reshape(1, D))


def _to_col_major(t):
    b, n = t.shape[:2]
    rows = n // GRID_W
    return t.reshape((b, rows, GRID_W) + t.shape[2:]).swapaxes(1, 2).reshape(t.shape)


def _to_row_major(t):
    b, n = t.shape[:2]
    rows = n // GRID_W
    return t.reshape((b, GRID_W, rows) + t.shape[2:]).swapaxes(1, 2).reshape(t.shape)


def _pad_cols(w, width):
    return jnp.pad(w, ((0, 0), (0, width - w.shape[1])))


def kernel(x, c, ctx, c_ctx, w_ada, b_ada, norm1_g, norm2_g, w_in, ssd_conv_w, ssd_conv_b, ssd_dt_bias, ssd_a_log, ssd_d, ssd_norm_g, hy_conv_w, hy_conv_b, hy_w1, hy_b1, hy_w2, hy_b2, hy_w3, hy_freq, hy_skip, ml_conv_w, ml_conv_b, ml_i_bias, ml_f_bias, ml_norm_g, hg_lb_logits, hg_norm_g, w_branch, w_out, w_router, w_gate, w_up, w_down, final_g):
    B, n, D = x.shape
    n_ctx = ctx.shape[1]
    T = n_ctx + n
    assert n_ctx % ROW_TILE == 0 and n % ROW_TILE == 0 and n % GRID_W == 0
    n_ctx_tiles = n_ctx // ROW_TILE
    n_ctx_chunks = n_ctx // SCAN_CHUNK
    w = BRANCH_W

    p_lb = jax.nn.softmax(hg_lb_logits.astype(F32), axis=0)
    lower_bounds = jnp.maximum(jnp.cumsum(p_lb, axis=0) - p_lb[0], 0.0)
    xc = jnp.concatenate([ctx, x], axis=1)
    depth = w_in.shape[0]
    for l in range(depth):
        last_layer = l == depth - 1
        mod = jax.nn.silu(c) @ w_ada[l] + b_ada[l]
        mod_c = jax.nn.silu(c_ctx) @ w_ada[l] + b_ada[l]
        mods = jnp.stack([mod, jnp.broadcast_to(mod_c, mod.shape)], axis=1).reshape(B, 2, 6, 1, D)
        sh1, sc1, g1, sh2, sc2, g2 = (mods[:, :, i] for i in range(6))

        wl = w_in[l].astype(BF16)
        o_ssd, o_hy, o_ml, o_hg, o_gate = 0, SSD_COLS, SSD_COLS + HY_COLS, SSD_COLS + HY_COLS + ML_COLS, \
            SSD_COLS + HY_COLS + ML_COLS + HG_COLS
        cols = lambda a, b_: wl[:, a:b_]
        z, xs, bc, dt, qk, v, o, gts = _proj(xc, sh1, sc1, norm1_g[l], [
            cols(o_ssd, o_ssd + w), cols(o_ssd + w, o_ssd + 2 * w), cols(o_ssd + 2 * w, o_ssd + w + SSD_CONV_CH),
            _pad_cols(cols(o_ssd + w + SSD_CONV_CH, o_hy), LANES),
            cols(o_ml, o_ml + 2 * w), cols(o_ml + 2 * w, o_ml + 3 * w), cols(o_ml + 3 * w, o_ml + 4 * w),
            _pad_cols(cols(o_ml + 4 * w, o_hg), LANES)], n_ctx_tiles)
        p_hy, gate_pre = _proj(xc, sh1, sc1, norm1_g[l], [cols(o_hy, o_ml), cols(o_gate, o_gate + N_BRANCHES * D)],
                               n_ctx_tiles)
        xc_cm = jnp.concatenate([xc[:, :n_ctx], _to_col_major(xc[:, n_ctx:])], axis=1)
        hq, hi, hff, hfb, hgt = _proj(xc_cm, sh1, sc1, norm1_g[l],
                                      [cols(o_hg + i * w, o_hg + (i + 1) * w) for i in range(5)], n_ctx_tiles)

        ya = _ssd_branch(z, xs, bc, dt, ssd_conv_w[l], ssd_conv_b[l], ssd_dt_bias[l], ssd_a_log[l], ssd_d[l],
                         ssd_norm_g[l], n_ctx_chunks)
        yc = _mlstm_branch(qk, v, o, gts, ml_conv_w[l], ml_conv_b[l], ml_i_bias[l], ml_f_bias[l], ml_norm_g[l],
                           n_ctx_chunks)
        yd_cm = _hgrn_branch(hq, hi, hff, hfb, hgt, lower_bounds[l], hg_norm_g[l], n_ctx_chunks)
        yd = jnp.concatenate([yd_cm[:, :n_ctx], _to_row_major(yd_cm[:, n_ctx:])], axis=1)
        hv, hx1, hx2 = _hyena_conv(p_hy, hy_conv_w[l], hy_conv_b[l], n_ctx_tiles)
        hy_params = tuple(p_[l].astype(F32) for p_ in (hy_w1, hy_b1, hy_w2, hy_b2, hy_w3, hy_freq))
        lat = lambda t: t[:, n_ctx:]
        sides, side_scale = _hyena_filter_sides(n, *hy_params)
        yb_lat = _hyena_latent(lat(hv), lat(hx1), lat(hx2), sides, side_scale, hy_skip[l])
        if last_layer:
            yb_ctx = jnp.zeros((B, n_ctx, w), F32)
        else:
            head = lambda t: t[:, :n_ctx]
            yb_ctx = _hyena_context(head(hv), head(hx1), head(hx2), _hyena_filter_time(n_ctx, *hy_params), hy_skip[l])
        yb = jnp.concatenate([yb_ctx, yb_lat], axis=1).astype(BF16)

        wr = jnp.stack(_split_bf16(_pad_cols(w_router[l].astype(F32), LANES)))
        x1, h2, aff = _merge((ya, yb, yc, yd), gate_pre, xc, g1, sh2, sc2, norm2_g[l],
                             w_branch[l].astype(BF16), w_out[l].astype(BF16), wr, n_ctx_tiles)
        wg, wu, wd = w_gate[l].astype(BF16), w_up[l].astype(BF16), w_down[l].astype(BF16)
        rows, vals = _expert_choice(h2, aff, n_ctx, n, wg, wu, wd)
        moe = jnp.zeros((B * T, D), F32).at[rows].add(vals)
        if not last_layer:
            rows_c, vals_c = _expert_choice(h2, aff, 0, n_ctx, wg, wu, wd)
            moe = moe.at[rows_c].add(vals_c)
        xc = _residual(x1, moe.reshape(B, T, D), g2, final_g, last_layer, n_ctx_tiles)
    return xc[:, n_ctx:]
```

```python
import functools
import math
import jax
import jax.numpy as jnp
from jax import lax
from jax.experimental import pallas as pl
from jax.experimental.pallas import tpu as pltpu


D_MODEL = 1024
DEPTH = 2
GRID_W = 64
BRANCH_W = D_MODEL // 2
N_BRANCHES = 4
EPS = 1e-6
MASK_NEG = -1e30
LB_FLOOR = 1e-30

SSD_HEAD_DIM = 64
SSD_HEADS = BRANCH_W // SSD_HEAD_DIM
SSD_GROUPS = 2
SSD_STATE = 64
SSD_CONV_CH = BRANCH_W + 2 * SSD_GROUPS * SSD_STATE
SSD_COLS = BRANCH_W + SSD_CONV_CH + 2 * SSD_HEADS

HY_WIDTH = BRANCH_W
HY_ORDER = 2
HY_BANDS = 8
HY_DECAY_TARGET = 1e-2
HY_SHORT_PCT = 0.3
HY_LONG_PCT = 1.5
HY_COLS = (HY_ORDER + 1) * HY_WIDTH

ML_HEADS = 4
ML_HEAD_DIM = BRANCH_W // ML_HEADS
ML_COLS = 4 * BRANCH_W + 4 * ML_HEADS

HG_HEADS = 4
HG_HEAD_DIM = BRANCH_W // HG_HEADS
HG_COLS = 5 * BRANCH_W
HG_LEAF = 8

N_EXPERTS = 16
EC_CAPACITY = 2

LANES = 128
SUBLANES = 8
ROW_TILE = 256
SCAN_CHUNK = 256
VMEM_LIMIT = 56 * 1024 * 1024

F32 = jnp.float32
BF16 = jnp.bfloat16


def _dot(a, b):
    return jnp.dot(a, b, preferred_element_type=F32)


def _dot_nt(a, b):
    return lax.dot_general(a, b, (((1,), (1,)), ((), ())), preferred_element_type=F32)


def _dot_tn(a, b):
    return lax.dot_general(a, b, (((0,), (0,)), ((), ())), preferred_element_type=F32)


def _dot_mask(mask, x):
    hi = x.astype(BF16)
    rem = x - hi.astype(F32)
    mid = rem.astype(BF16)
    lo = (rem - mid.astype(F32)).astype(BF16)
    return _dot(mask, hi) + _dot(mask, mid) + _dot(mask, lo)


def _sigmoid(x):
    return 0.5 * jnp.tanh(0.5 * x) + 0.5


def _silu(x):
    return x * _sigmoid(x)


def _softplus(x):
    return jnp.maximum(x, 0.0) + jnp.log1p(jnp.exp(-jnp.abs(x)))


def _log_sigmoid(x):
    return jnp.minimum(x, 0.0) - jnp.log1p(jnp.exp(-jnp.abs(x)))


def _rms(x):
    return x * lax.rsqrt(jnp.mean(x * x, axis=-1, keepdims=True) + EPS)


def _proj_kernel(x_ref, sh_ref, sc_ref, g_ref, *refs, n_out):
    h = (_rms(x_ref[...]) * g_ref[...] * (1.0 + sc_ref[...]) + sh_ref[...]).astype(BF16)
    for w_ref, o_ref in zip(refs[:n_out], refs[n_out:]):
        o_ref[...] = _dot(h, w_ref[...])


def _proj(xc, shift, scale, gain, weights, n_ctx_tiles):
    B, T, D = xc.shape
    mod_spec = pl.BlockSpec((None, None, 1, D), lambda b, i: (b, jnp.where(i < n_ctx_tiles, 1, 0), 0, 0))
    return pl.pallas_call(
        functools.partial(_proj_kernel, n_out=len(weights)),
        grid=(B, T // ROW_TILE),
        in_specs=[pl.BlockSpec((None, ROW_TILE, D), lambda b, i: (b, i, 0)), mod_spec, mod_spec,
                  pl.BlockSpec((1, D), lambda b, i: (0, 0))]
                 + [pl.BlockSpec(w.shape, lambda b, i: (0, 0)) for w in weights],
        out_specs=[pl.BlockSpec((None, ROW_TILE, w.shape[1]), lambda b, i: (b, i, 0)) for w in weights],
        out_shape=[jax.ShapeDtypeStruct((B, T, w.shape[1]), F32) for w in weights],
        compiler_params=pltpu.CompilerParams(dimension_semantics=("parallel", "parallel"),
                                             vmem_limit_bytes=VMEM_LIMIT),
        name="norm_mod_proj",
    )(xc, shift, scale, gain.reshape(1, D), *weights)


def _scan_chunk(step, reverse, n_ctx_chunks, n_chunks):
    if not reverse:
        return step
    return jnp.where(step < n_ctx_chunks, n_ctx_chunks - 1 - step, n_chunks - 1 - (step - n_ctx_chunks))


def _scan_specs(reverse, n_ctx_chunks, n_chunks, chunk, batch):
    cidx = lambda s: _scan_chunk(s, reverse, n_ctx_chunks, n_chunks)
    per_tile = chunk // SUBLANES
    n_tiles = n_chunks * per_tile

    def main(width):
        return pl.BlockSpec((batch, chunk, width), lambda s: (0, cidx(s), 0))

    def prev(width):
        return pl.BlockSpec((batch, SUBLANES, width), lambda s: (0, jnp.maximum(cidx(s) * per_tile - 1, 0), 0))

    def nxt(width):
        return pl.BlockSpec((batch, SUBLANES, width),
                            lambda s: (0, jnp.minimum((cidx(s) + 1) * per_tile, n_tiles - 1), 0))

    def const(shape):
        return pl.BlockSpec(shape, lambda s: (0,) * len(shape))

    return main, prev, nxt, const


def _per_sample(body, n_inputs, n_shared):
    def kern(*refs):
        for b in range(refs[0].shape[0]):
            body(*[r if n_inputs <= i < n_inputs + n_shared else r.at[b] for i, r in enumerate(refs)])
    return kern


def _segment_edges(chunk_idx, n_ctx_chunks, n_chunks):
    first = jnp.logical_or(chunk_idx == 0, chunk_idx == n_ctx_chunks)
    last = jnp.logical_or(chunk_idx == n_ctx_chunks - 1, chunk_idx == n_chunks - 1)
    return first, last


def _conv3(u, u_prev_tile, u_next_tile, w, bias, first, last):
    q = u.shape[0]
    rows = lax.broadcasted_iota(jnp.int32, u.shape, 0)
    before = jnp.where(first, 0.0, u_prev_tile[SUBLANES - 1:SUBLANES, :])
    after = jnp.where(last, 0.0, u_next_tile[0:1, :])
    u_m1 = jnp.where(rows == 0, before, pltpu.roll(u, 1, 0))
    u_p1 = jnp.where(rows == q - 1, after, pltpu.roll(u, q - 1, 0))
    return w[0:1, :] * u_m1 + w[1:2, :] * u + w[2:3, :] * u_p1 + bias


def _order_masks(q, reverse):
    r = lax.broadcasted_iota(jnp.int32, (q, q), 0)
    c = lax.broadcasted_iota(jnp.int32, (q, q), 1)
    return (r <= c) if reverse else (r >= c)


def _ssd_kernel(z_ref, x_ref, xp_ref, xn_ref, bc_ref, bcp_ref, bcn_ref, dt_ref, yf_ref,
                cwx_ref, cbx_ref, cwbc_ref, cbbc_ref, dtb_ref, nega_ref, dskip_ref, ng_ref,
                out_ref, st_ref, *, reverse, n_ctx_chunks, n_chunks):
    q = x_ref.shape[0]
    step = pl.program_id(0)
    cidx = _scan_chunk(step, reverse, n_ctx_chunks, n_chunks)
    first, last = _segment_edges(cidx, n_ctx_chunks, n_chunks)

    @pl.when(step == 0)
    def _():
        st_ref[...] = jnp.zeros_like(st_ref)

    xs = _silu(_conv3(x_ref[...], xp_ref[...], xn_ref[...], cwx_ref[...], cbx_ref[...], first, last))
    bc = _silu(_conv3(bc_ref[...], bcp_ref[...], bcn_ref[...], cwbc_ref[...], cbbc_ref[...], first, last))
    dt_all = _softplus(dt_ref[...] + dtb_ref[...])
    da_all = dt_all * nega_ref[...]
    mask = _order_masks(q, reverse)
    acs = _dot_mask(mask.astype(BF16), da_all)
    acs_t = acs.T
    dt_t = dt_all.T
    edge = 0 if reverse else q - 1
    gn = SSD_STATE
    per_group = SSD_HEADS // SSD_GROUPS
    gw = per_group * SSD_HEAD_DIM
    lane_head = lax.broadcasted_iota(jnp.int32, (q, gw), 1) // SSD_HEAD_DIM

    def per_head_lanes(cols):
        out = cols[-1]
        for i in range(per_group - 2, -1, -1):
            out = jnp.where(lane_head == i, cols[i], out)
        return jnp.broadcast_to(out, (q, gw))

    ys = []
    for g in range(SSD_GROUPS):
        bg = bc[:, g * gn:(g + 1) * gn].astype(BF16)
        cg = bc[:, (SSD_GROUPS + g) * gn:(SSD_GROUPS + g + 1) * gn].astype(BF16)
        cb = _dot_nt(cg, bg)
        xg = xs[:, g * gw:(g + 1) * gw]
        cols = [(SSD_HEADS if reverse else 0) + g * per_group + hh for hh in range(per_group)]
        a_cols = [acs[:, c:c + 1] for c in cols]
        a_rows = [acs_t[c:c + 1, :] for c in cols]
        a_ends = [r[:, edge:edge + 1] for r in a_rows]
        diag = []
        for hh, c in enumerate(cols):
            decay = jnp.where(mask, jnp.exp(jnp.minimum(a_cols[hh] - a_rows[hh], 0.0)), 0.0)
            scores = (cb * decay * dt_t[c:c + 1, :]).astype(BF16)
            diag.append(_dot(scores, xg[:, hh * SSD_HEAD_DIM:(hh + 1) * SSD_HEAD_DIM].astype(BF16)))
        a_lanes = per_head_lanes(a_cols)
        end_lanes = per_head_lanes(a_ends)
        st = st_ref[g]
        ys.append(jnp.concatenate(diag, axis=1) + _dot(cg, st.astype(BF16)) * jnp.exp(a_lanes))
        to_end = jnp.exp(end_lanes - a_lanes) * per_head_lanes([dt_all[:, c:c + 1] for c in cols])
        st_ref[g] = jnp.exp(end_lanes[0:1, :]) * st + _dot_tn(bg, (xg * to_end).astype(BF16))
    y = jnp.concatenate(ys, axis=1)
    if not reverse:
        out_ref[...] = y
    else:
        y = (y + yf_ref[...] + dskip_ref[...] * xs) * _silu(z_ref[...])
        out_ref[...] = (_rms(y) * ng_ref[...]).astype(out_ref.dtype)


def _ssd_pass(z, x, bc, dt, y_fwd, consts, reverse, n_ctx_chunks):
    B, T, _ = x.shape
    n_chunks = T // SCAN_CHUNK
    main, prev, nxt, const = _scan_specs(reverse, n_ctx_chunks, n_chunks, SCAN_CHUNK, B)
    w = BRANCH_W
    wbc = 2 * SSD_GROUPS * SSD_STATE
    in_specs = [main(w), main(w), prev(w), nxt(w), main(wbc), prev(wbc), nxt(wbc), main(LANES), main(w),
                const((3, w)), const((1, w)), const((3, wbc)), const((1, wbc)), const((1, LANES)),
                const((1, LANES)), const((1, w)), const((1, w))]
    body = functools.partial(_ssd_kernel, reverse=reverse, n_ctx_chunks=n_ctx_chunks, n_chunks=n_chunks)
    return pl.pallas_call(
        _per_sample(body, 9, 8),
        grid=(n_chunks,),
        in_specs=in_specs,
        out_specs=main(w),
        out_shape=jax.ShapeDtypeStruct((B, T, w), BF16 if reverse else F32),
        scratch_shapes=[pltpu.VMEM((B, SSD_GROUPS, SSD_STATE, BRANCH_W // SSD_GROUPS), F32)],
        compiler_params=pltpu.CompilerParams(dimension_semantics=("arbitrary",),
                                             vmem_limit_bytes=VMEM_LIMIT),
        name="ssd_bwd_finish" if reverse else "ssd_fwd",
    )(z, x, x, x, bc, bc, bc, dt, y_fwd, *consts)


def _ssd_branch(z, x, bc, dt, conv_w, conv_b, dt_bias, a_log, d_skip, norm_g, n_ctx_chunks):
    w = BRANCH_W
    pad = LANES - 2 * SSD_HEADS
    dtb = jnp.pad(dt_bias.astype(F32).reshape(1, -1), ((0, 0), (0, pad)))
    nega = jnp.pad(-jnp.exp(a_log.astype(F32)).reshape(1, -1), ((0, 0), (0, pad)))
    consts = (conv_w[:, :w], conv_b[:w].reshape(1, w), conv_w[:, w:], conv_b[w:].reshape(1, -1), dtb, nega,
              jnp.repeat(d_skip.astype(F32), SSD_HEAD_DIM).reshape(1, w), norm_g.reshape(1, w))
    y_f = _ssd_pass(z, x, bc, dt, x, consts, False, n_ctx_chunks)
    return _ssd_pass(z, x, bc, dt, y_f, consts, True, n_ctx_chunks)


def _mlstm_kernel(qk_ref, qkp_ref, qkn_ref, v_ref, o_ref, gt_ref, hf_ref, cw_ref, cb_ref, gb_ref, ng_ref,
                  out_ref, c_ref, m_ref, *, reverse, n_ctx_chunks, n_chunks):
    q = qk_ref.shape[0]
    dh = ML_HEAD_DIM
    step = pl.program_id(0)
    cidx = _scan_chunk(step, reverse, n_ctx_chunks, n_chunks)
    first, last = _segment_edges(cidx, n_ctx_chunks, n_chunks)

    @pl.when(step == 0)
    def _():
        c_ref[...] = jnp.zeros_like(c_ref)
        m_ref[...] = jnp.zeros_like(m_ref)

    qk = _silu(_conv3(qk_ref[...], qkp_ref[...], qkn_ref[...], cw_ref[...], cb_ref[...], first, last))
    gates = gt_ref[...] + gb_ref[...]
    logf_all = _log_sigmoid(gates)
    mask = _order_masks(q, reverse)
    bcum = _dot_mask(mask.astype(BF16), logf_all)
    bcum_t = bcum.T
    gates_t = gates.T
    edge = 0 if reverse else q - 1
    lane = lax.broadcasted_iota(jnp.int32, (q, dh), 1)
    ones_col = jnp.where(lane == 0, 1.0, 0.0).astype(BF16)
    hs = []
    for h in range(ML_HEADS):
        li = (ML_HEADS if reverse else 0) + h
        lf = 2 * ML_HEADS + li
        qh = qk[:, h * dh:(h + 1) * dh].astype(BF16)
        kh = qk[:, BRANCH_W + h * dh:BRANCH_W + (h + 1) * dh] * (dh ** -0.5)
        v_aug = jnp.concatenate([v_ref[:, h * dh:(h + 1) * dh].astype(BF16), ones_col], axis=1)
        b_col = bcum[:, lf:lf + 1]
        b_row = bcum_t[lf:lf + 1, :]
        i_col = gates[:, li:li + 1]
        i_row = gates_t[li:li + 1, :]
        b_tot = b_row[:, edge:edge + 1]
        m_prev = m_ref[h:h + 1, 0:1]
        c_prev = c_ref[h]
        dmat = jnp.where(mask, b_col - b_row + i_row, MASK_NEG)
        inter = b_col + m_prev
        m_t = jnp.maximum(inter, jnp.max(dmat, axis=1, keepdims=True))
        wgt = jnp.where(mask, jnp.exp(jnp.minimum(dmat - m_t, 0.0)), 0.0) * _dot_nt(qh, kh.astype(BF16))
        w_int = jnp.exp(inter - m_t)
        cross = _dot(qh, c_prev.astype(BF16))
        num = _dot(wgt.astype(BF16), v_aug[:, :dh]) + w_int * cross[:, :dh]
        den = jnp.sum(wgt, axis=1, keepdims=True) + w_int * cross[:, dh:dh + 1]
        hs.append(num / jnp.maximum(jnp.abs(den), jnp.exp(-m_t)))
        a = b_tot - b_col + i_col
        m_new = jnp.maximum(b_tot + m_prev, jnp.max(a, axis=0, keepdims=True))
        kw = (kh * jnp.exp(a - m_new)).astype(BF16)
        c_ref[h] = jnp.exp(b_tot + m_prev - m_new) * c_prev + _dot_tn(kw, v_aug)
        m_ref[h:h + 1, :] = jnp.broadcast_to(m_new, (1, LANES))
    if not reverse:
        out_ref[...] = jnp.concatenate(hs, axis=1)
    else:
        hf = hf_ref[...]
        outs = [_rms(hs[h] + hf[:, h * dh:(h + 1) * dh]) for h in range(ML_HEADS)]
        out_ref[...] = (_sigmoid(o_ref[...]) * jnp.concatenate(outs, axis=1) * ng_ref[...]).astype(out_ref.dtype)


def _mlstm_pass(qk, v, o, gates, h_fwd, consts, reverse, n_ctx_chunks):
    B, T, _ = v.shape
    n_chunks = T // SCAN_CHUNK
    main, prev, nxt, const = _scan_specs(reverse, n_ctx_chunks, n_chunks, SCAN_CHUNK, B)
    w = BRANCH_W
    in_specs = [main(2 * w), prev(2 * w), nxt(2 * w), main(w), main(w), main(LANES), main(w),
                const((3, 2 * w)), const((1, 2 * w)), const((1, LANES)), const((1, w))]
    body = functools.partial(_mlstm_kernel, reverse=reverse, n_ctx_chunks=n_ctx_chunks, n_chunks=n_chunks)
    return pl.pallas_call(
        _per_sample(body, 7, 4),
        grid=(n_chunks,),
        in_specs=in_specs,
        out_specs=main(w),
        out_shape=jax.ShapeDtypeStruct((B, T, w), BF16 if reverse else F32),
        scratch_shapes=[pltpu.VMEM((B, ML_HEADS, ML_HEAD_DIM, 2 * ML_HEAD_DIM), F32),
                        pltpu.VMEM((B, SUBLANES, LANES), F32)],
        compiler_params=pltpu.CompilerParams(dimension_semantics=("arbitrary",),
                                             vmem_limit_bytes=VMEM_LIMIT),
        name="mlstm_bwd_finish" if reverse else "mlstm_fwd",
    )(qk, qk, qk, v, o, gates, h_fwd, *consts)


def _mlstm_branch(qk, v, o, gates, conv_w, conv_b, i_bias, f_bias, norm_g, n_ctx_chunks):
    w = BRANCH_W
    gb = jnp.concatenate([i_bias.astype(F32).reshape(-1), f_bias.astype(F32).reshape(-1)])
    gb = jnp.pad(gb.reshape(1, -1), ((0, 0), (0, LANES - 4 * ML_HEADS)))
    consts = (conv_w, conv_b.reshape(1, 2 * w), gb, norm_g.reshape(1, w))
    h_f = _mlstm_pass(qk, v, o, gates, v, consts, False, n_ctx_chunks)
    return _mlstm_pass(qk, v, o, gates, h_f, consts, True, n_ctx_chunks)


def _hgrn_level_matrix(q, block, reverse):
    t = lax.broadcasted_iota(jnp.int32, (q, q), 0)
    r = lax.broadcasted_iota(jnp.int32, (q, q), 1)
    blk = t // block
    start = blk * block
    end = start + block - 1
    odd = (blk % 2) == 1
    if not reverse:
        lo = jnp.where(odd, start, t + 1)
        hi = jnp.where(odd, t, end)
    else:
        lo = jnp.where(odd, start, t)
        hi = jnp.where(odd, t - 1, end)
    return jnp.where((r >= lo) & (r <= hi), 1.0, 0.0)


def _hgrn_select_matrix(q, reverse):
    mats = [_order_masks(q, reverse).astype(F32)]
    block = HG_LEAF
    while block < q:
        mats.append(_hgrn_level_matrix(q, block, reverse))
        block *= 2
    return jnp.concatenate(mats, axis=0).astype(BF16)


def _hgrn_kernel(q_ref, v_ref, f_ref, g_ref, of_ref, lb_ref, ng_ref, sel_ref, hsum_ref, out_ref, st_ref, *, reverse):
    q, width = q_ref.shape
    dh = HG_HEAD_DIM
    step = pl.program_id(0)

    @pl.when(step == 0)
    def _():
        st_ref[...] = jnp.zeros_like(st_ref)

    lb = lb_ref[...]
    pre = f_ref[...]
    log_lb = jnp.log(jnp.maximum(lb, LB_FLOOR))
    log_ub = jnp.log1p(-lb)
    e_pre = jnp.exp(-jnp.abs(pre))
    lo = log_ub + jnp.minimum(pre, 0.0) - jnp.log1p(e_pre)
    logf = jnp.maximum(log_lb, lo) + jnp.log1p(jnp.exp(-jnp.abs(log_lb - lo)))
    key = (1.0 - lb) * (jnp.where(pre >= 0.0, e_pre, 1.0) / (1.0 + e_pre))
    qv = _silu(q_ref[...])
    sums = _dot_mask(sel_ref[...], logf)
    bq = sums[:q]
    vv = v_ref[...]

    leaves = (q // HG_LEAF, HG_LEAF, width)
    b3, q3, k3, v3 = bq.reshape(leaves), qv.reshape(leaves), key.reshape(leaves), vv.reshape(leaves)
    tt = lax.broadcasted_iota(jnp.int32, leaves, 1)
    o_leaf = jnp.zeros(leaves, F32)
    for s in range(HG_LEAF):
        ok = (tt <= s) if reverse else (tt >= s)
        e = jnp.exp(jnp.where(ok, b3 - b3[:, s:s + 1, :], MASK_NEG))
        prod = (q3 * e * k3[:, s:s + 1, :]).reshape(q, width).astype(BF16)
        att = _dot(prod, hsum_ref[...]).reshape(leaves)
        o_leaf = o_leaf + att * v3[:, s:s + 1, :]
    o_leaf = o_leaf.reshape(q, width)

    t_idx = lax.broadcasted_iota(jnp.int32, (q, q), 0)
    s_idx = lax.broadcasted_iota(jnp.int32, (q, q), 1)
    row_t = lax.broadcasted_iota(jnp.int32, (q, dh), 0)
    levels = []
    block = HG_LEAF
    while block < q:
        fac = jnp.exp(sums[(len(levels) + 1) * q:(len(levels) + 2) * q])
        tb, sb = t_idx // block, s_idx // block
        if not reverse:
            pair = ((tb % 2) == 1) & (sb == tb - 1)
            is_query = ((row_t // block) % 2) == 1
        else:
            pair = ((tb % 2) == 0) & (sb == tb + 1)
            is_query = ((row_t // block) % 2) == 0
        levels.append((fac, pair, is_query))
        block *= 2
    edge = 0 if reverse else q - 1
    b_end = bq[edge:edge + 1, :]
    q_in = qv * jnp.exp(bq)
    k_out = key * jnp.exp(b_end - bq)
    outs = []
    for h in range(HG_HEADS):
        sl = slice(h * dh, (h + 1) * dh)
        att = jnp.zeros((q, q), F32)
        for fac, pair, is_query in levels:
            qt = jnp.where(is_query, qv[:, sl] * fac[:, sl], 0.0).astype(BF16)
            kt = jnp.where(is_query, 0.0, key[:, sl] * fac[:, sl]).astype(BF16)
            att = att + jnp.where(pair, _dot_nt(qt, kt), 0.0)
        vh = vv[:, sl].astype(BF16)
        st = st_ref[h]
        o = o_leaf[:, sl] + _dot(att.astype(BF16), vh) + _dot_nt(q_in[:, sl].astype(BF16), st.astype(BF16))
        st_ref[h] = jnp.exp(b_end[:, sl]) * st + _dot_tn(vh, k_out[:, sl].astype(BF16))
        outs.append(o)
    if not reverse:
        out_ref[...] = jnp.concatenate(outs, axis=1)
    else:
        of = of_ref[...]
        fin = [_rms(outs[h] + of[:, h * dh:(h + 1) * dh]) for h in range(HG_HEADS)]
        out_ref[...] = (jnp.concatenate(fin, axis=1) * ng_ref[...] * _sigmoid(g_ref[...])).astype(out_ref.dtype)


def _hgrn_pass(qr, v, f, g, o_fwd, lb, norm_g, reverse, n_ctx_chunks):
    B, T, w = v.shape
    n_chunks = T // SCAN_CHUNK
    main, _, _, const = _scan_specs(reverse, n_ctx_chunks, n_chunks, SCAN_CHUNK, B)
    sel = _hgrn_select_matrix(SCAN_CHUNK, reverse)
    head_of = jnp.arange(w) // HG_HEAD_DIM
    head_sum = (head_of[:, None] == head_of[None, :]).astype(BF16)
    return pl.pallas_call(
        _per_sample(functools.partial(_hgrn_kernel, reverse=reverse), 5, 4),
        grid=(n_chunks,),
        in_specs=[main(w), main(w), main(w), main(w), main(w), const((1, w)), const((1, w)), const(sel.shape),
                  const((w, w))],
        out_specs=main(w),
        out_shape=jax.ShapeDtypeStruct((B, T, w), BF16 if reverse else F32),
        scratch_shapes=[pltpu.VMEM((B, HG_HEADS, HG_HEAD_DIM, HG_HEAD_DIM), F32)],
        compiler_params=pltpu.CompilerParams(dimension_semantics=("arbitrary",),
                                             vmem_limit_bytes=VMEM_LIMIT),
        name="hgrn_bwd_finish" if reverse else "hgrn_fwd",
    )(qr, v, f, g, o_fwd, lb.reshape(1, w), norm_g.reshape(1, w), sel, head_sum)


def _hgrn_branch(qr, v, f_fwd, f_bwd, g, lb, norm_g, n_ctx_chunks):
    o_f = _hgrn_pass(qr, v, f_fwd, g, v, lb, norm_g, False, n_ctx_chunks)
    return _hgrn_pass(qr, v, f_bwd, g, o_f, lb, norm_g, True, n_ctx_chunks)


def _hyena_filter_hidden(n, w1, b1, w2, b2, freq):
    pos = jnp.arange(n, dtype=F32)
    t = pos / max(n - 1, 1)
    bands = jnp.arange(1, HY_BANDS + 1, dtype=F32)
    ang = (2.0 * math.pi / n) * pos[:, None] * bands[None, :]
    feats = jnp.concatenate([t[:, None], jnp.cos(ang), jnp.sin(ang)], axis=-1)
    hid = jnp.sin(freq[0] * (feats @ w1 + b1))
    return jnp.sin(freq[1] * (hid @ w2 + b2)), t


def _hyena_deltas():
    return jnp.abs(jnp.linspace(math.log(HY_DECAY_TARGET) / HY_LONG_PCT,
                                math.log(HY_DECAY_TARGET) / HY_SHORT_PCT, HY_WIDTH, dtype=F32))


def _hyena_filter_time(n, w1, b1, w2, b2, w3, freq):
    hid, t = _hyena_filter_hidden(n, w1, b1, w2, b2, freq)
    h = (hid @ w3).reshape(n, HY_ORDER, 2, HY_WIDTH)
    h = h * jnp.exp(-t[:, None] * _hyena_deltas()[None, :])[:, None, None, :]
    two_sided = jnp.concatenate([h[:, :, 0], jnp.zeros((1, HY_ORDER, HY_WIDTH), F32),
                                 jnp.flip(h[1:, :, 1], axis=0)], axis=0)
    return two_sided / (jnp.sum(jnp.abs(two_sided), axis=0, keepdims=True) + EPS)


def _filter_kernel(hid_ref, t_ref, wh_ref, wl_ref, delta_ref, o_ref):
    hid = hid_ref[...]
    hid_hi, hid_lo = _split_bf16(hid)
    h = _dot(hid_hi, wh_ref[...]) + _dot(hid_lo, wh_ref[...]) + _dot(hid_hi, wl_ref[...])
    decay = jnp.exp(-t_ref[...] * delta_ref[...])
    wd = HY_WIDTH
    lag0 = jnp.logical_and(pl.program_id(0) == 0, lax.broadcasted_iota(jnp.int32, decay.shape, 0) == 0)
    for g in range(2 * HY_ORDER):
        val = h[:, g * wd:(g + 1) * wd] * decay
        o_ref[g] = jnp.where(lag0, 0.0, val) if g % 2 == 1 else val


def _hyena_filter_sides(n, w1, b1, w2, b2, w3, freq):
    hid, t = _hyena_filter_hidden(n, w1, b1, w2, b2, freq)
    wh, wl = _split_bf16(w3)
    g4, wd = 2 * HY_ORDER, HY_WIDTH
    sides = pl.pallas_call(
        _filter_kernel,
        grid=(n // ROW_TILE,),
        in_specs=[pl.BlockSpec((ROW_TILE, hid.shape[1]), lambda i: (i, 0)), pl.BlockSpec((ROW_TILE, 1), lambda i: (i, 0)),
                  pl.BlockSpec(wh.shape, lambda i: (0, 0)), pl.BlockSpec(wl.shape, lambda i: (0, 0)),
                  pl.BlockSpec((1, wd), lambda i: (0, 0))],
        out_specs=pl.BlockSpec((g4, ROW_TILE, wd), lambda i: (0, i, 0)),
        out_shape=jax.ShapeDtypeStruct((g4, n, wd), F32),
        compiler_params=pltpu.CompilerParams(dimension_semantics=("parallel",)),
        name="hyena_filter_sides",
    )(hid, t.reshape(n, 1), wh, wl, _hyena_deltas().reshape(1, wd))
    norm = jnp.sum(jnp.abs(sides), axis=1).reshape(HY_ORDER, 2, wd).sum(axis=1)
    return sides, 1.0 / (norm + EPS)


def _hyena_conv_kernel(p_ref, pp_ref, pn_ref, w_ref, b_ref, v_ref, x1_ref, x2_ref, *, n_ctx_tiles, n_tiles):
    first, last = _segment_edges(pl.program_id(0), n_ctx_tiles, n_tiles)
    u = _conv3(p_ref[...], pp_ref[...], pn_ref[...], w_ref[...], b_ref[...], first, last)
    wd = HY_WIDTH
    v_ref[...] = u[:, :wd]
    x1_ref[...] = u[:, wd:2 * wd]
    x2_ref[...] = u[:, 2 * wd:]


def _hyena_conv(p, conv_w, conv_b, n_ctx_tiles):
    B, T, C = p.shape
    n_tiles = T // ROW_TILE
    main, prev, nxt, const = _scan_specs(False, n_ctx_tiles, n_tiles, ROW_TILE, B)
    body = functools.partial(_hyena_conv_kernel, n_ctx_tiles=n_ctx_tiles, n_tiles=n_tiles)
    return pl.pallas_call(
        _per_sample(body, 3, 2),
        grid=(n_tiles,),
        in_specs=[main(C), prev(C), nxt(C), const((3, C)), const((1, C))],
        out_specs=[main(HY_WIDTH)] * 3,
        out_shape=[jax.ShapeDtypeStruct((B, T, HY_WIDTH), F32)] * 3,
        compiler_params=pltpu.CompilerParams(dimension_semantics=("parallel",), vmem_limit_bytes=VMEM_LIMIT),
        name="hyena_short_conv",
    )(p, p, p, conv_w, conv_b.reshape(1, C))


DFT_N2 = 256
DFT_ROW_TILE = SUBLANES


def _split_bf16(x):
    hi = x.astype(BF16)
    return hi, (x - hi.astype(F32)).astype(BF16)


def _dft_tables(n):
    L = 2 * n
    n1_full = L // DFT_N2
    nh = n1_full // 2
    nk = nh + 1
    nkp = -(-nk // SUBLANES) * SUBLANES
    k1 = jnp.arange(nkp, dtype=jnp.int32)
    valid = (k1 < nk)[:, None]

    def stage1(rows):
        n1 = jnp.arange(rows, dtype=jnp.int32)
        ang = (2.0 * math.pi / n1_full) * ((k1[:, None] * n1[None, :]) % n1_full).astype(F32)
        return jnp.concatenate([jnp.where(valid, jnp.cos(ang), 0.0), jnp.where(valid, -jnp.sin(ang), 0.0)], axis=0)

    n1 = jnp.arange(nh, dtype=jnp.int32)
    ang = (2.0 * math.pi / n1_full) * ((n1[:, None] * k1[None, :]) % n1_full).astype(F32)
    ck = jnp.where((k1 == 0) | (k1 == nh), 1.0, 2.0) * jnp.where(k1 < nk, 1.0, 0.0) / L
    stage3 = jnp.concatenate([jnp.cos(ang) * ck[None, :], -jnp.sin(ang) * ck[None, :]], axis=1)
    n2 = jnp.arange(DFT_N2, dtype=jnp.int32)
    idx = (n2[None, :, None] * (k1[:, None, None] + n1_full * n2[None, None, :])) % L
    ang2 = (2.0 * math.pi / L) * idx.astype(F32)
    gr, gi = jnp.cos(ang2), -jnp.sin(ang2)
    grt, git = jnp.swapaxes(gr, 1, 2), jnp.swapaxes(gi, 1, 2)
    m_fwd = jnp.concatenate([jnp.concatenate([grt, -git], axis=2), jnp.concatenate([git, grt], axis=2)], axis=1)
    m_inv = jnp.swapaxes(m_fwd, 1, 2)
    eye = jnp.eye(DFT_ROW_TILE, dtype=F32)
    return dict(nh=nh, nk=nk, nkp=nkp, f1=jnp.kron(stage1(nh), eye).astype(BF16),
                f3=jnp.kron(stage3, eye).astype(BF16), m_fwd=m_fwd.astype(BF16), m_inv=m_inv.astype(BF16))


def _dft_in_kernel(u_ref, f_ref, o_ref):
    rows, t2, wd = u_ref.shape
    nkp = o_ref.shape[1]
    r = _dot(f_ref[...], u_ref[...].reshape(rows * t2, wd).astype(BF16))
    o_ref[0] = r[:nkp * t2].reshape(nkp, t2, wd)
    o_ref[1] = r[nkp * t2:].reshape(nkp, t2, wd)


def _dft_in(u, f1):
    G, rows, n2, wd = u.shape
    t2 = DFT_ROW_TILE
    nkp = f1.shape[0] // (2 * t2)
    return pl.pallas_call(
        _dft_in_kernel,
        grid=(G, n2 // t2),
        in_specs=[pl.BlockSpec((None, rows, t2, wd), lambda g, j: (g, 0, j, 0)),
                  pl.BlockSpec(f1.shape, lambda g, j: (0, 0))],
        out_specs=pl.BlockSpec((None, 2, nkp, t2, wd), lambda g, j: (g, 0, 0, j, 0)),
        out_shape=jax.ShapeDtypeStruct((G, 2, nkp, n2, wd), F32),
        compiler_params=pltpu.CompilerParams(dimension_semantics=("parallel", "parallel"),
                                             vmem_limit_bytes=VMEM_LIMIT),
        name="hyena_dft_rows",
    )(u, f1)


def _spectrum_kernel(af_ref, ab_ref, sc_ref, m_ref, o_ref, *, nk):
    k1 = pl.program_id(0)
    half = DFT_N2

    @pl.when(k1 < nk)
    def _():
        m = m_ref[...]
        uf = _dot(m, jnp.concatenate([af_ref[0], af_ref[1]], axis=0).astype(BF16))
        ub = _dot(m, jnp.concatenate([ab_ref[0], ab_ref[1]], axis=0).astype(BF16))
        o_ref[0] = (uf[:half] + ub[:half]) * sc_ref[...]
        o_ref[1] = (uf[half:] - ub[half:]) * sc_ref[...]

    @pl.when(k1 >= nk)
    def _():
        o_ref[...] = jnp.zeros_like(o_ref)


def _spectrum(a, scale, tables):
    _, _, nkp, n2, wd = a.shape
    slab = lambda g_of: pl.BlockSpec((None, 2, None, n2, wd), g_of)
    return pl.pallas_call(
        functools.partial(_spectrum_kernel, nk=tables["nk"]),
        grid=(nkp, HY_ORDER),
        in_specs=[slab(lambda k, o: (2 * o, 0, k, 0, 0)), slab(lambda k, o: (2 * o + 1, 0, k, 0, 0)),
                  pl.BlockSpec((None, 1, wd), lambda k, o: (o, 0, 0)),
                  pl.BlockSpec((None, 2 * n2, 2 * n2), lambda k, o: (k, 0, 0))],
        out_specs=slab(lambda k, o: (o, 0, k, 0, 0)),
        out_shape=jax.ShapeDtypeStruct((HY_ORDER, 2, nkp, n2, wd), F32),
        compiler_params=pltpu.CompilerParams(dimension_semantics=("parallel", "parallel"),
                                             vmem_limit_bytes=VMEM_LIMIT),
        name="hyena_spectrum",
    )(a, a, scale, tables["m_fwd"])


def _dft_mid_kernel(a_ref, h_ref, mf_ref, mi_ref, o_ref, *, nk):
    k1 = pl.program_id(0)
    half = DFT_N2

    @pl.when(k1 < nk)
    def _():
        u = _dot(mf_ref[...], jnp.concatenate([a_ref[0], a_ref[1]], axis=0).astype(BF16))
        ur, ui = u[:half], u[half:]
        hr, hi = h_ref[0], h_ref[1]
        v = jnp.concatenate([ur * hr - ui * hi, ur * hi + ui * hr], axis=0)
        y = _dot(mi_ref[...], v.astype(BF16))
        o_ref[0] = y[:half]
        o_ref[1] = y[half:]

    @pl.when(k1 >= nk)
    def _():
        o_ref[...] = jnp.zeros_like(o_ref)


def _dft_mid(a, spec, order, tables):
    G, _, nkp, n2, wd = a.shape
    slab = lambda g_of: pl.BlockSpec((None, 2, None, n2, wd), g_of)
    mat = pl.BlockSpec((None, 2 * n2, 2 * n2), lambda k, g: (k, 0, 0))
    return pl.pallas_call(
        functools.partial(_dft_mid_kernel, nk=tables["nk"]),
        grid=(nkp, G),
        in_specs=[slab(lambda k, g: (g, 0, k, 0, 0)), slab(lambda k, g: (order, 0, k, 0, 0)), mat, mat],
        out_specs=slab(lambda k, g: (g, 0, k, 0, 0)),
        out_shape=jax.ShapeDtypeStruct(a.shape, F32),
        compiler_params=pltpu.CompilerParams(dimension_semantics=("parallel", "parallel"),
                                             vmem_limit_bytes=VMEM_LIMIT),
        name="hyena_dft_mid",
    )(a, spec, tables["m_fwd"], tables["m_inv"])


def _dft_out_kernel(b_ref, u_ref, x_ref, skip_ref, f_ref, o_ref):
    rows, t2, wd = u_ref.shape
    nkp = b_ref.shape[1]
    bb = jnp.concatenate([b_ref[0].reshape(nkp * t2, wd), b_ref[1].reshape(nkp * t2, wd)], axis=0)
    y = _dot(f_ref[...], bb.astype(BF16)).reshape(rows, t2, wd)
    o_ref[...] = x_ref[...] * (y + u_ref[...] * skip_ref[...])


def _dft_out(bq, u, xg, skip_row, f3):
    G, rows, n2, wd = u.shape
    nkp = bq.shape[2]
    t2 = DFT_ROW_TILE
    tile = pl.BlockSpec((None, rows, t2, wd), lambda g, j: (g, 0, j, 0))
    return pl.pallas_call(
        _dft_out_kernel,
        grid=(G, n2 // t2),
        in_specs=[pl.BlockSpec((None, 2, nkp, t2, wd), lambda g, j: (g, 0, 0, j, 0)), tile, tile,
                  pl.BlockSpec((1, wd), lambda g, j: (0, 0)), pl.BlockSpec(f3.shape, lambda g, j: (0, 0))],
        out_specs=tile,
        out_shape=jax.ShapeDtypeStruct(u.shape, F32),
        compiler_params=pltpu.CompilerParams(dimension_semantics=("parallel", "parallel"),
                                             vmem_limit_bytes=VMEM_LIMIT),
        name="hyena_dft_rows_inverse",
    )(bq, u, xg, skip_row, f3)


def _hyena_latent(v, x1, x2, sides, scale, skip):
    B, n, wd = v.shape
    tb = _dft_tables(n)
    view = lambda t: t.reshape(t.shape[0], tb["nh"], DFT_N2, wd)
    spec = _spectrum(_dft_in(view(sides), tb["f1"]), scale.reshape(HY_ORDER, 1, wd), tb)
    z = view(v)
    for o, xg in enumerate((x1, x2)):
        bq = _dft_mid(_dft_in(z, tb["f1"]), spec, o, tb)
        z = _dft_out(bq, z, view(xg), skip[o].astype(F32).reshape(1, wd), tb["f3"])
    return z.reshape(B, n, wd)


def _hyena_context(v, x1, x2, filt, skip):
    n = v.shape[1]
    spec = jnp.fft.rfft(filt, axis=0)
    z = v
    for o, xg in enumerate((x1, x2)):
        zf = jnp.fft.rfft(z, n=2 * n, axis=1)
        y = jnp.fft.irfft(zf * spec[:, o], n=2 * n, axis=1)[:, :n]
        z = xg * (y + z * skip[o].astype(F32))
    return z


def _merge_kernel(ya_ref, yb_ref, yc_ref, yd_ref, gate_ref, x_ref, g1_ref, sh_ref, sc_ref, ng_ref,
                  wb_ref, wo_ref, wr_ref, x1_ref, h2_ref, aff_ref):
    d = x_ref.shape[1]
    merged = None
    for i, y_ref in enumerate((ya_ref, yb_ref, yc_ref, yd_ref)):
        term = _sigmoid(gate_ref[:, i * d:(i + 1) * d]) * _dot(y_ref[...], wb_ref[i])
        merged = term if merged is None else merged + term
    x1 = x_ref[...] + g1_ref[...] * _dot(merged.astype(BF16), wo_ref[...])
    x1_ref[...] = x1
    h2 = _rms(x1) * ng_ref[...] * (1.0 + sc_ref[...]) + sh_ref[...]
    h2_ref[...] = h2.astype(h2_ref.dtype)
    h2_hi, h2_lo = _split_bf16(h2)
    logits = _dot(h2_hi, wr_ref[0]) + _dot(h2_lo, wr_ref[0]) + _dot(h2_hi, wr_ref[1])
    lane = lax.broadcasted_iota(jnp.int32, logits.shape, 1)
    logits = jnp.where(lane < N_EXPERTS, logits, MASK_NEG)
    e = jnp.exp(logits - jnp.max(logits, axis=1, keepdims=True))
    aff_ref[...] = e / jnp.sum(e, axis=1, keepdims=True)


def _merge(ys, gates, xc, g1, shift, scale, gain, wb, wo, wr, n_ctx_tiles):
    B, T, D = xc.shape
    w = BRANCH_W
    tile = lambda width: pl.BlockSpec((None, ROW_TILE, width), lambda b, i: (b, i, 0))
    mod_spec = pl.BlockSpec((None, None, 1, D), lambda b, i: (b, jnp.where(i < n_ctx_tiles, 1, 0), 0, 0))
    const = lambda shape: pl.BlockSpec(shape, lambda b, i: (0,) * len(shape))
    return pl.pallas_call(
        _merge_kernel,
        grid=(B, T // ROW_TILE),
        in_specs=[tile(w), tile(w), tile(w), tile(w), tile(N_BRANCHES * D), tile(D), mod_spec, mod_spec, mod_spec,
                  const((1, D)), const(wb.shape), const(wo.shape), const(wr.shape)],
        out_specs=[tile(D), tile(D), tile(LANES)],
        out_shape=[jax.ShapeDtypeStruct((B, T, D), F32), jax.ShapeDtypeStruct((B, T, D), BF16),
                   jax.ShapeDtypeStruct((B, T, LANES), F32)],
        compiler_params=pltpu.CompilerParams(dimension_semantics=("parallel", "parallel"),
                                             vmem_limit_bytes=VMEM_LIMIT),
        name="merge_out_router",
    )(*ys, gates, xc, g1, shift, scale, gain.reshape(1, D), wb, wo, wr)


def _expert_kernel(x_ref, s_ref, wg_ref, wu_ref, wd_ref, o_ref):
    x = x_ref[...]
    hid = _silu(_dot(x, wg_ref[...])) * _dot(x, wu_ref[...])
    o_ref[...] = _dot(hid.astype(BF16), wd_ref[...]) * s_ref[...]


def _experts(xg, score, wg, wu, wd):
    E, R, D = xg.shape
    F = wg.shape[2]
    tm = min(R, 512)
    return pl.pallas_call(
        _expert_kernel,
        grid=(E, R // tm),
        in_specs=[pl.BlockSpec((None, tm, D), lambda e, i: (e, i, 0)),
                  pl.BlockSpec((None, tm, 1), lambda e, i: (e, i, 0)),
                  pl.BlockSpec((None, D, F), lambda e, i: (e, 0, 0)),
                  pl.BlockSpec((None, D, F), lambda e, i: (e, 0, 0)),
                  pl.BlockSpec((None, F, D), lambda e, i: (e, 0, 0))],
        out_specs=pl.BlockSpec((None, tm, D), lambda e, i: (e, i, 0)),
        out_shape=jax.ShapeDtypeStruct((E, R, D), F32),
        compiler_params=pltpu.CompilerParams(dimension_semantics=("parallel", "arbitrary"),
                                             vmem_limit_bytes=VMEM_LIMIT),
        name="expert_swiglu",
    )(xg, score, wg, wu, wd)


def _expert_choice(h2, aff, seg_start, seg_len, wg, wu, wd):
    B, T, D = h2.shape
    cap = EC_CAPACITY * seg_len // N_EXPERTS
    a = aff[:, seg_start:seg_start + seg_len, :N_EXPERTS]
    score, idx = lax.top_k(jnp.swapaxes(a, 1, 2), cap)
    rows = idx + seg_start + (jnp.arange(B, dtype=idx.dtype) * T)[:, None, None]
    rows = jnp.swapaxes(rows, 0, 1).reshape(N_EXPERTS, B * cap)
    score = jnp.swapaxes(score, 0, 1).reshape(N_EXPERTS, B * cap, 1)
    xg = jnp.take(h2.reshape(B * T, D), rows, axis=0)
    ye = _experts(xg, score, wg, wu, wd)
    return rows.reshape(-1), ye.reshape(-1, D)


def _residual_kernel(x_ref, m_ref, g2_ref, gain_ref, o_ref, *, final):
    x = x_ref[...] + g2_ref[...] * m_ref[...]
    o_ref[...] = _rms(x) * gain_ref[...] if final else x


def _residual(x1, moe, g2, gain, final, n_ctx_tiles):
    B, T, D = x1.shape
    tile = pl.BlockSpec((None, ROW_TILE, D), lambda b, i: (b, i, 0))
    mod_spec = pl.BlockSpec((None, None, 1, D), lambda b, i: (b, jnp.where(i < n_ctx_tiles, 1, 0), 0, 0))
    return pl.pallas_call(
        functools.partial(_residual_kernel, final=final),
        grid=(B, T // ROW_TILE),
        in_specs=[tile, tile, mod_spec, pl.BlockSpec((1, D), lambda b, i: (0, 0))],
        out_specs=tile,
        out_shape=jax.ShapeDtypeStruct((B, T, D), F32),
        compiler_params=pltpu.CompilerParams(dimension_semantics=("parallel", "parallel")),
        name="moe_residual_final_norm" if final else "moe_residual",
    )(x1, moe, g2, gain.reshape(1, D))


def _to_col_major(t):
    b, n = t.shape[:2]
    rows = n // GRID_W
    return t.reshape((b, rows, GRID_W) + t.shape[2:]).swapaxes(1, 2).reshape(t.shape)


def _to_row_major(t):
    b, n = t.shape[:2]
    rows = n // GRID_W
    return t.reshape((b, GRID_W, rows) + t.shape[2:]).swapaxes(1, 2).reshape(t.shape)


def _pad_cols(w, width):
    return jnp.pad(w, ((0, 0), (0, width - w.shape[1])))


def kernel(x, c, ctx, c_ctx, w_ada, b_ada, norm1_g, norm2_g, w_in, ssd_conv_w, ssd_conv_b, ssd_dt_bias, ssd_a_log, ssd_d, ssd_norm_g, hy_conv_w, hy_conv_b, hy_w1, hy_b1, hy_w2, hy_b2, hy_w3, hy_freq, hy_skip, ml_conv_w, ml_conv_b, ml_i_bias, ml_f_bias, ml_norm_g, hg_lb_logits, hg_norm_g, w_branch, w_out, w_router, w_gate, w_up, w_down, final_g):
    B, n, D = x.shape
    n_ctx = ctx.shape[1]
    T = n_ctx + n
    assert n_ctx % ROW_TILE == 0 and n % ROW_TILE == 0 and n % GRID_W == 0
    assert n_ctx % SCAN_CHUNK == 0 and n % SCAN_CHUNK == 0
    n_ctx_tiles = n_ctx // ROW_TILE
    n_ctx_chunks = n_ctx // SCAN_CHUNK
    w = BRANCH_W

    p_lb = jax.nn.softmax(hg_lb_logits.astype(F32), axis=0)
    lower_bounds = jnp.maximum(jnp.cumsum(p_lb, axis=0) - p_lb[0], 0.0)
    xc = jnp.concatenate([ctx, x], axis=1)
    depth = w_in.shape[0]
    for l in range(depth):
        last_layer = l == depth - 1
        mod = jax.nn.silu(c) @ w_ada[l] + b_ada[l]
        mod_c = jax.nn.silu(c_ctx) @ w_ada[l] + b_ada[l]
        mods = jnp.stack([mod, jnp.broadcast_to(mod_c, mod.shape)], axis=1).reshape(B, 2, 6, 1, D)
        sh1, sc1, g1, sh2, sc2, g2 = (mods[:, :, i] for i in range(6))

        wl = w_in[l].astype(BF16)
        o_ssd, o_hy, o_ml, o_hg, o_gate = 0, SSD_COLS, SSD_COLS + HY_COLS, SSD_COLS + HY_COLS + ML_COLS, \
            SSD_COLS + HY_COLS + ML_COLS + HG_COLS
        cols = lambda a, b_: wl[:, a:b_]
        z, xs, bc, dt, qk, v, o, gts = _proj(xc, sh1, sc1, norm1_g[l], [
            cols(o_ssd, o_ssd + w), cols(o_ssd + w, o_ssd + 2 * w), cols(o_ssd + 2 * w, o_ssd + w + SSD_CONV_CH),
            _pad_cols(cols(o_ssd + w + SSD_CONV_CH, o_hy), LANES),
            cols(o_ml, o_ml + 2 * w), cols(o_ml + 2 * w, o_ml + 3 * w), cols(o_ml + 3 * w, o_ml + 4 * w),
            _pad_cols(cols(o_ml + 4 * w, o_hg), LANES)], n_ctx_tiles)
        p_hy, gate_pre = _proj(xc, sh1, sc1, norm1_g[l], [cols(o_hy, o_ml), cols(o_gate, o_gate + N_BRANCHES * D)],
                               n_ctx_tiles)
        xc_cm = jnp.concatenate([xc[:, :n_ctx], _to_col_major(xc[:, n_ctx:])], axis=1)
        hq, hi, hff, hfb, hgt = _proj(xc_cm, sh1, sc1, norm1_g[l],
                                      [cols(o_hg + i * w, o_hg + (i + 1) * w) for i in range(5)], n_ctx_tiles)

        ya = _ssd_branch(z, xs, bc, dt, ssd_conv_w[l], ssd_conv_b[l], ssd_dt_bias[l], ssd_a_log[l], ssd_d[l],
                         ssd_norm_g[l], n_ctx_chunks)
        yc = _mlstm_branch(qk, v, o, gts, ml_conv_w[l], ml_conv_b[l], ml_i_bias[l], ml_f_bias[l], ml_norm_g[l],
                           n_ctx_chunks)
        yd_cm = _hgrn_branch(hq, hi, hff, hfb, hgt, lower_bounds[l], hg_norm_g[l], n_ctx_chunks)
        yd = jnp.concatenate([yd_cm[:, :n_ctx], _to_row_major(yd_cm[:, n_ctx:])], axis=1)
        hv, hx1, hx2 = _hyena_conv(p_hy, hy_conv_w[l], hy_conv_b[l], n_ctx_tiles)
        hy_params = tuple(p_[l].astype(F32) for p_ in (hy_w1, hy_b1, hy_w2, hy_b2, hy_w3, hy_freq))
        lat = lambda t: t[:, n_ctx:]
        sides, side_scale = _hyena_filter_sides(n, *hy_params)
        yb_lat = _hyena_latent(lat(hv), lat(hx1), lat(hx2), sides, side_scale, hy_skip[l])
        if last_layer:
            yb_ctx = jnp.zeros((B, n_ctx, w), F32)
        else:
            head = lambda t: t[:, :n_ctx]
            yb_ctx = _hyena_context(head(hv), head(hx1), head(hx2), _hyena_filter_time(n_ctx, *hy_params), hy_skip[l])
        yb = jnp.concatenate([yb_ctx, yb_lat], axis=1).astype(BF16)

        wr = jnp.stack(_split_bf16(_pad_cols(w_router[l].astype(F32), LANES)))
        x1, h2, aff = _merge((ya, yb, yc, yd), gate_pre, xc, g1, sh2, sc2, norm2_g[l],
                             w_branch[l].astype(BF16), w_out[l].astype(BF16), wr, n_ctx_tiles)
        wg, wu, wd = w_gate[l].astype(BF16), w_up[l].astype(BF16), w_down[l].astype(BF16)
        rows, vals = _expert_choice(h2, aff, n_ctx, n, wg, wu, wd)
        moe = jnp.zeros((B * T, D), F32).at[rows].add(vals)
        if not last_layer:
            rows_c, vals_c = _expert_choice(h2, aff, 0, n_ctx, wg, wu, wd)
            moe = moe.at[rows_c].add(vals_c)
        xc = _residual(x1, moe.reshape(B, T, D), g2, final_g, last_layer, n_ctx_tiles)
    return xc[:, n_ctx:]
```

```python
import functools
import math
import jax
import jax.numpy as jnp
from jax import lax
from jax.experimental import pallas as pl
from jax.experimental.pallas import tpu as pltpu


D_MODEL = 1024
DEPTH = 2
GRID_W = 64
BRANCH_W = D_MODEL // 2
N_BRANCHES = 4
EPS = 1e-6
MASK_NEG = -1e30
LB_FLOOR = 1e-30

SSD_HEAD_DIM = 64
SSD_HEADS = BRANCH_W // SSD_HEAD_DIM
SSD_GROUPS = 2
SSD_STATE = 64
SSD_CONV_CH = BRANCH_W + 2 * SSD_GROUPS * SSD_STATE
SSD_COLS = BRANCH_W + SSD_CONV_CH + 2 * SSD_HEADS

HY_WIDTH = BRANCH_W
HY_ORDER = 2
HY_BANDS = 8
HY_DECAY_TARGET = 1e-2
HY_SHORT_PCT = 0.3
HY_LONG_PCT = 1.5
HY_COLS = (HY_ORDER + 1) * HY_WIDTH

ML_HEADS = 4
ML_HEAD_DIM = BRANCH_W // ML_HEADS
ML_COLS = 4 * BRANCH_W + 4 * ML_HEADS

HG_HEADS = 4
HG_HEAD_DIM = BRANCH_W // HG_HEADS
HG_COLS = 5 * BRANCH_W
HG_LEAF = 8

N_EXPERTS = 16
EC_CAPACITY = 2

LANES = 128
SUBLANES = 8
ROW_TILE = 256
SCAN_CHUNK = 256
VMEM_LIMIT = 56 * 1024 * 1024

F32 = jnp.float32
BF16 = jnp.bfloat16


def _dot(a, b):
    return jnp.dot(a, b, preferred_element_type=F32)


def _dot_nt(a, b):
    return lax.dot_general(a, b, (((1,), (1,)), ((), ())), preferred_element_type=F32)


def _dot_tn(a, b):
    return lax.dot_general(a, b, (((0,), (0,)), ((), ())), preferred_element_type=F32)


def _dot_mask(mask, x):
    hi = x.astype(BF16)
    rem = x - hi.astype(F32)
    mid = rem.astype(BF16)
    lo = (rem - mid.astype(F32)).astype(BF16)
    return _dot(mask, hi) + _dot(mask, mid) + _dot(mask, lo)


def _sigmoid(x):
    return 0.5 * jnp.tanh(0.5 * x) + 0.5


def _silu(x):
    return x * _sigmoid(x)


def _softplus(x):
    return jnp.maximum(x, 0.0) + jnp.log1p(jnp.exp(-jnp.abs(x)))


def _log_sigmoid(x):
    return jnp.minimum(x, 0.0) - jnp.log1p(jnp.exp(-jnp.abs(x)))


def _rms(x):
    return x * lax.rsqrt(jnp.mean(x * x, axis=-1, keepdims=True) + EPS)


def _proj_kernel(x_ref, sh_ref, sc_ref, g_ref, *refs, n_out):
    h = (_rms(x_ref[...]) * g_ref[...] * (1.0 + sc_ref[...]) + sh_ref[...]).astype(BF16)
    for w_ref, o_ref in zip(refs[:n_out], refs[n_out:]):
        o_ref[...] = _dot(h, w_ref[...])


def _proj(xc, shift, scale, gain, weights, n_ctx_tiles):
    B, T, D = xc.shape
    mod_spec = pl.BlockSpec((None, None, 1, D), lambda b, i: (b, jnp.where(i < n_ctx_tiles, 1, 0), 0, 0))
    return pl.pallas_call(
        functools.partial(_proj_kernel, n_out=len(weights)),
        grid=(B, T // ROW_TILE),
        in_specs=[pl.BlockSpec((None, ROW_TILE, D), lambda b, i: (b, i, 0)), mod_spec, mod_spec,
                  pl.BlockSpec((1, D), lambda b, i: (0, 0))]
                 + [pl.BlockSpec(w.shape, lambda b, i: (0, 0)) for w in weights],
        out_specs=[pl.BlockSpec((None, ROW_TILE, w.shape[1]), lambda b, i: (b, i, 0)) for w in weights],
        out_shape=[jax.ShapeDtypeStruct((B, T, w.shape[1]), F32) for w in weights],
        compiler_params=pltpu.CompilerParams(dimension_semantics=("parallel", "parallel"),
                                             vmem_limit_bytes=VMEM_LIMIT),
        name="norm_mod_proj",
    )(xc, shift, scale, gain.reshape(1, D), *weights)


def _scan_chunk(step, reverse, n_ctx_chunks, n_chunks):
    if not reverse:
        return step
    return jnp.where(step < n_ctx_chunks, n_ctx_chunks - 1 - step, n_chunks - 1 - (step - n_ctx_chunks))


def _scan_specs(reverse, n_ctx_chunks, n_chunks, chunk, batch):
    cidx = lambda s: _scan_chunk(s, reverse, n_ctx_chunks, n_chunks)
    per_tile = chunk // SUBLANES
    n_tiles = n_chunks * per_tile

    def main(width):
        return pl.BlockSpec((batch, chunk, width), lambda s: (0, cidx(s), 0))

    def prev(width):
        return pl.BlockSpec((batch, SUBLANES, width), lambda s: (0, jnp.maximum(cidx(s) * per_tile - 1, 0), 0))

    def nxt(width):
        return pl.BlockSpec((batch, SUBLANES, width),
                            lambda s: (0, jnp.minimum((cidx(s) + 1) * per_tile, n_tiles - 1), 0))

    def const(shape):
        return pl.BlockSpec(shape, lambda s: (0,) * len(shape))

    return main, prev, nxt, const


def _per_sample(body, n_inputs, n_shared):
    def kern(*refs):
        for b in range(refs[0].shape[0]):
            body(*[r if n_inputs <= i < n_inputs + n_shared else r.at[b] for i, r in enumerate(refs)])
    return kern


def _segment_edges(chunk_idx, n_ctx_chunks, n_chunks):
    first = jnp.logical_or(chunk_idx == 0, chunk_idx == n_ctx_chunks)
    last = jnp.logical_or(chunk_idx == n_ctx_chunks - 1, chunk_idx == n_chunks - 1)
    return first, last


def _conv3(u, u_prev_tile, u_next_tile, w, bias, first, last):
    q = u.shape[0]
    rows = lax.broadcasted_iota(jnp.int32, u.shape, 0)
    before = jnp.where(first, 0.0, u_prev_tile[SUBLANES - 1:SUBLANES, :])
    after = jnp.where(last, 0.0, u_next_tile[0:1, :])
    u_m1 = jnp.where(rows == 0, before, pltpu.roll(u, 1, 0))
    u_p1 = jnp.where(rows == q - 1, after, pltpu.roll(u, q - 1, 0))
    return w[0:1, :] * u_m1 + w[1:2, :] * u + w[2:3, :] * u_p1 + bias


def _order_masks(q, reverse):
    r = lax.broadcasted_iota(jnp.int32, (q, q), 0)
    c = lax.broadcasted_iota(jnp.int32, (q, q), 1)
    return (r <= c) if reverse else (r >= c)


def _ssd_kernel(z_ref, x_ref, xp_ref, xn_ref, bc_ref, bcp_ref, bcn_ref, dt_ref, yf_ref,
                cwx_ref, cbx_ref, cwbc_ref, cbbc_ref, dtb_ref, nega_ref, dskip_ref, ng_ref,
                out_ref, st_ref, *, reverse, n_ctx_chunks, n_chunks):
    q = x_ref.shape[0]
    step = pl.program_id(0)
    cidx = _scan_chunk(step, reverse, n_ctx_chunks, n_chunks)
    first, last = _segment_edges(cidx, n_ctx_chunks, n_chunks)

    @pl.when(step == 0)
    def _():
        st_ref[...] = jnp.zeros_like(st_ref)

    xs = _silu(_conv3(x_ref[...], xp_ref[...], xn_ref[...], cwx_ref[...], cbx_ref[...], first, last))
    bc = _silu(_conv3(bc_ref[...], bcp_ref[...], bcn_ref[...], cwbc_ref[...], cbbc_ref[...], first, last))
    dt_all = _softplus(dt_ref[...] + dtb_ref[...])
    da_all = dt_all * nega_ref[...]
    mask = _order_masks(q, reverse)
    acs = _dot_mask(mask.astype(BF16), da_all)
    acs_t = acs.T
    dt_t = dt_all.T
    edge = 0 if reverse else q - 1
    gn = SSD_STATE
    per_group = SSD_HEADS // SSD_GROUPS
    gw = per_group * SSD_HEAD_DIM
    lane_head = lax.broadcasted_iota(jnp.int32, (q, gw), 1) // SSD_HEAD_DIM

    def per_head_lanes(cols):
        out = cols[-1]
        for i in range(per_group - 2, -1, -1):
            out = jnp.where(lane_head == i, cols[i], out)
        return jnp.broadcast_to(out, (q, gw))

    ys = []
    for g in range(SSD_GROUPS):
        bg = bc[:, g * gn:(g + 1) * gn].astype(BF16)
        cg = bc[:, (SSD_GROUPS + g) * gn:(SSD_GROUPS + g + 1) * gn].astype(BF16)
        cb = _dot_nt(cg, bg)
        xg = xs[:, g * gw:(g + 1) * gw]
        cols = [(SSD_HEADS if reverse else 0) + g * per_group + hh for hh in range(per_group)]
        a_cols = [acs[:, c:c + 1] for c in cols]
        a_rows = [acs_t[c:c + 1, :] for c in cols]
        a_ends = [r[:, edge:edge + 1] for r in a_rows]
        diag = []
        for hh, c in enumerate(cols):
            decay = jnp.where(mask, jnp.exp(jnp.minimum(a_cols[hh] - a_rows[hh], 0.0)), 0.0)
            scores = (cb * decay * dt_t[c:c + 1, :]).astype(BF16)
            diag.append(_dot(scores, xg[:, hh * SSD_HEAD_DIM:(hh + 1) * SSD_HEAD_DIM].astype(BF16)))
        a_lanes = per_head_lanes(a_cols)
        end_lanes = per_head_lanes(a_ends)
        st = st_ref[g]
        ys.append(jnp.concatenate(diag, axis=1) + _dot(cg, st.astype(BF16)) * jnp.exp(a_lanes))
        to_end = jnp.exp(end_lanes - a_lanes) * per_head_lanes([dt_all[:, c:c + 1] for c in cols])
        st_ref[g] = jnp.exp(end_lanes[0:1, :]) * st + _dot_tn(bg, (xg * to_end).astype(BF16))
    y = jnp.concatenate(ys, axis=1)
    if not reverse:
        out_ref[...] = y
    else:
        y = (y + yf_ref[...] + dskip_ref[...] * xs) * _silu(z_ref[...])
        out_ref[...] = (_rms(y) * ng_ref[...]).astype(out_ref.dtype)


def _ssd_pass(z, x, bc, dt, y_fwd, consts, reverse, n_ctx_chunks):
    B, T, _ = x.shape
    n_chunks = T // SCAN_CHUNK
    main, prev, nxt, const = _scan_specs(reverse, n_ctx_chunks, n_chunks, SCAN_CHUNK, B)
    w = BRANCH_W
    wbc = 2 * SSD_GROUPS * SSD_STATE
    in_specs = [main(w), main(w), prev(w), nxt(w), main(wbc), prev(wbc), nxt(wbc), main(LANES), main(w),
                const((3, w)), const((1, w)), const((3, wbc)), const((1, wbc)), const((1, LANES)),
                const((1, LANES)), const((1, w)), const((1, w))]
    body = functools.partial(_ssd_kernel, reverse=reverse, n_ctx_chunks=n_ctx_chunks, n_chunks=n_chunks)
    return pl.pallas_call(
        _per_sample(body, 9, 8),
        grid=(n_chunks,),
        in_specs=in_specs,
        out_specs=main(w),
        out_shape=jax.ShapeDtypeStruct((B, T, w), BF16 if reverse else F32),
        scratch_shapes=[pltpu.VMEM((B, SSD_GROUPS, SSD_STATE, BRANCH_W // SSD_GROUPS), F32)],
        compiler_params=pltpu.CompilerParams(dimension_semantics=("arbitrary",),
                                             vmem_limit_bytes=VMEM_LIMIT),
        name="ssd_bwd_finish" if reverse else "ssd_fwd",
    )(z, x, x, x, bc, bc, bc, dt, y_fwd, *consts)


def _ssd_branch(z, x, bc, dt, conv_w, conv_b, dt_bias, a_log, d_skip, norm_g, n_ctx_chunks):
    w = BRANCH_W
    pad = LANES - 2 * SSD_HEADS
    dtb = jnp.pad(dt_bias.astype(F32).reshape(1, -1), ((0, 0), (0, pad)))
    nega = jnp.pad(-jnp.exp(a_log.astype(F32)).reshape(1, -1), ((0, 0), (0, pad)))
    consts = (conv_w[:, :w], conv_b[:w].reshape(1, w), conv_w[:, w:], conv_b[w:].reshape(1, -1), dtb, nega,
              jnp.repeat(d_skip.astype(F32), SSD_HEAD_DIM).reshape(1, w), norm_g.reshape(1, w))
    y_f = _ssd_pass(z, x, bc, dt, x, consts, False, n_ctx_chunks)
    return _ssd_pass(z, x, bc, dt, y_f, consts, True, n_ctx_chunks)


def _mlstm_kernel(qk_ref, qkp_ref, qkn_ref, v_ref, o_ref, gt_ref, hf_ref, cw_ref, cb_ref, gb_ref, ng_ref,
                  out_ref, c_ref, m_ref, *, reverse, n_ctx_chunks, n_chunks):
    q = qk_ref.shape[0]
    dh = ML_HEAD_DIM
    step = pl.program_id(0)
    cidx = _scan_chunk(step, reverse, n_ctx_chunks, n_chunks)
    first, last = _segment_edges(cidx, n_ctx_chunks, n_chunks)

    @pl.when(step == 0)
    def _():
        c_ref[...] = jnp.zeros_like(c_ref)
        m_ref[...] = jnp.zeros_like(m_ref)

    qk = _silu(_conv3(qk_ref[...], qkp_ref[...], qkn_ref[...], cw_ref[...], cb_ref[...], first, last))
    gates = gt_ref[...] + gb_ref[...]
    logf_all = _log_sigmoid(gates)
    mask = _order_masks(q, reverse)
    bcum = _dot_mask(mask.astype(BF16), logf_all)
    bcum_t = bcum.T
    gates_t = gates.T
    edge = 0 if reverse else q - 1
    lane = lax.broadcasted_iota(jnp.int32, (q, dh), 1)
    ones_col = jnp.where(lane == 0, 1.0, 0.0).astype(BF16)
    hs = []
    for h in range(ML_HEADS):
        li = (ML_HEADS if reverse else 0) + h
        lf = 2 * ML_HEADS + li
        q32 = qk[:, h * dh:(h + 1) * dh]
        qh = q32.astype(BF16)
        kh = qk[:, BRANCH_W + h * dh:BRANCH_W + (h + 1) * dh] * (dh ** -0.5)
        v_aug = jnp.concatenate([v_ref[:, h * dh:(h + 1) * dh].astype(BF16), ones_col], axis=1)
        b_col = bcum[:, lf:lf + 1]
        b_row = bcum_t[lf:lf + 1, :]
        i_col = gates[:, li:li + 1]
        i_row = gates_t[li:li + 1, :]
        b_tot = b_row[:, edge:edge + 1]
        m_prev = m_ref[h:h + 1, 0:1]
        c_prev = c_ref[h]
        dmat = jnp.where(mask, b_col - b_row + i_row, MASK_NEG)
        inter = b_col + m_prev
        m_t = jnp.maximum(inter, jnp.max(dmat, axis=1, keepdims=True))
        wgt = jnp.where(mask, jnp.exp(jnp.minimum(dmat - m_t, 0.0)), 0.0) * _dot_nt(qh, kh.astype(BF16))
        w_int = jnp.exp(inter - m_t)
        lhs = jnp.concatenate([wgt.astype(BF16), (w_int * q32).astype(BF16)], axis=1)
        both = _dot(lhs, jnp.concatenate([v_aug, c_prev.astype(BF16)], axis=0))
        den = both[:, dh:dh + 1]
        hs.append(both[:, :dh] / jnp.maximum(jnp.abs(den), jnp.exp(-m_t)))
        a = b_tot - b_col + i_col
        m_new = jnp.maximum(b_tot + m_prev, jnp.max(a, axis=0, keepdims=True))
        kw = (kh * jnp.exp(a - m_new)).astype(BF16)
        c_ref[h] = jnp.exp(b_tot + m_prev - m_new) * c_prev + _dot_tn(kw, v_aug)
        m_ref[h:h + 1, :] = jnp.broadcast_to(m_new, (1, LANES))
    if not reverse:
        out_ref[...] = jnp.concatenate(hs, axis=1)
    else:
        hf = hf_ref[...]
        outs = [_rms(hs[h] + hf[:, h * dh:(h + 1) * dh]) for h in range(ML_HEADS)]
        out_ref[...] = (_sigmoid(o_ref[...]) * jnp.concatenate(outs, axis=1) * ng_ref[...]).astype(out_ref.dtype)


def _mlstm_pass(qk, v, o, gates, h_fwd, consts, reverse, n_ctx_chunks):
    B, T, _ = v.shape
    n_chunks = T // SCAN_CHUNK
    main, prev, nxt, const = _scan_specs(reverse, n_ctx_chunks, n_chunks, SCAN_CHUNK, B)
    w = BRANCH_W
    in_specs = [main(2 * w), prev(2 * w), nxt(2 * w), main(w), main(w), main(LANES), main(w),
                const((3, 2 * w)), const((1, 2 * w)), const((1, LANES)), const((1, w))]
    body = functools.partial(_mlstm_kernel, reverse=reverse, n_ctx_chunks=n_ctx_chunks, n_chunks=n_chunks)
    return pl.pallas_call(
        _per_sample(body, 7, 4),
        grid=(n_chunks,),
        in_specs=in_specs,
        out_specs=main(w),
        out_shape=jax.ShapeDtypeStruct((B, T, w), BF16 if reverse else F32),
        scratch_shapes=[pltpu.VMEM((B, ML_HEADS, ML_HEAD_DIM, 2 * ML_HEAD_DIM), F32),
                        pltpu.VMEM((B, SUBLANES, LANES), F32)],
        compiler_params=pltpu.CompilerParams(dimension_semantics=("arbitrary",),
                                             vmem_limit_bytes=VMEM_LIMIT),
        name="mlstm_bwd_finish" if reverse else "mlstm_fwd",
    )(qk, qk, qk, v, o, gates, h_fwd, *consts)


def _mlstm_branch(qk, v, o, gates, conv_w, conv_b, i_bias, f_bias, norm_g, n_ctx_chunks):
    w = BRANCH_W
    gb = jnp.concatenate([i_bias.astype(F32).reshape(-1), f_bias.astype(F32).reshape(-1)])
    gb = jnp.pad(gb.reshape(1, -1), ((0, 0), (0, LANES - 4 * ML_HEADS)))
    consts = (conv_w, conv_b.reshape(1, 2 * w), gb, norm_g.reshape(1, w))
    h_f = _mlstm_pass(qk, v, o, gates, v, consts, False, n_ctx_chunks)
    return _mlstm_pass(qk, v, o, gates, h_f, consts, True, n_ctx_chunks)


def _hgrn_level_factor(bq, block, reverse):
    q, width = bq.shape
    b3 = bq.reshape(q // block, block, width)
    blk = lax.broadcasted_iota(jnp.int32, b3.shape, 0)
    if not reverse:
        edge = b3[:, block - 1:block, :]
        before = jnp.concatenate([jnp.zeros_like(edge[:1]), edge[:-1]], axis=0)
        expo = jnp.where(blk % 2 == 1, b3 - before, edge - b3)
    else:
        edge = b3[:, 0:1, :]
        after = jnp.concatenate([edge[1:], jnp.zeros_like(edge[:1])], axis=0)
        expo = jnp.where(blk % 2 == 0, b3 - after, edge - b3)
    return jnp.exp(expo).reshape(q, width)


def _hgrn_kernel(q_ref, v_ref, f_ref, g_ref, of_ref, lb_ref, ng_ref, hsum_ref, out_ref, st_ref, *, reverse):
    q, width = q_ref.shape
    dh = HG_HEAD_DIM
    step = pl.program_id(0)

    @pl.when(step == 0)
    def _():
        st_ref[...] = jnp.zeros_like(st_ref)

    lb = lb_ref[...]
    pre = f_ref[...]
    log_lb = jnp.log(jnp.maximum(lb, LB_FLOOR))
    log_ub = jnp.log1p(-lb)
    e_pre = jnp.exp(-jnp.abs(pre))
    lo = log_ub + jnp.minimum(pre, 0.0) - jnp.log1p(e_pre)
    logf = jnp.maximum(log_lb, lo) + jnp.log1p(jnp.exp(-jnp.abs(log_lb - lo)))
    key = (1.0 - lb) * (jnp.where(pre >= 0.0, e_pre, 1.0) / (1.0 + e_pre))
    qv = _silu(q_ref[...])
    bq = _dot_mask(_order_masks(q, reverse).astype(BF16), logf)
    vv = v_ref[...]

    leaves = (q // HG_LEAF, HG_LEAF, width)
    b3, q3, k3, v3 = bq.reshape(leaves), qv.reshape(leaves), key.reshape(leaves), vv.reshape(leaves)
    tt = lax.broadcasted_iota(jnp.int32, leaves, 1)
    prods = []
    for s in range(HG_LEAF):
        ok = (tt <= s) if reverse else (tt >= s)
        e = jnp.exp(jnp.where(ok, b3 - b3[:, s:s + 1, :], MASK_NEG))
        prods.append((q3 * e * k3[:, s:s + 1, :]).reshape(q, width).astype(BF16))
    att = _dot(jnp.concatenate(prods, axis=0), hsum_ref[...])
    o_leaf = jnp.zeros(leaves, F32)
    for s in range(HG_LEAF):
        o_leaf = o_leaf + att[s * q:(s + 1) * q].reshape(leaves) * v3[:, s:s + 1, :]
    o_leaf = o_leaf.reshape(q, width)

    t_idx = lax.broadcasted_iota(jnp.int32, (q, q), 0)
    s_idx = lax.broadcasted_iota(jnp.int32, (q, q), 1)
    row_t = lax.broadcasted_iota(jnp.int32, (q, dh), 0)
    levels = []
    block = HG_LEAF
    while block < q:
        fac = _hgrn_level_factor(bq, block, reverse)
        tb, sb = t_idx // block, s_idx // block
        if not reverse:
            pair = ((tb % 2) == 1) & (sb == tb - 1)
            is_query = ((row_t // block) % 2) == 1
        else:
            pair = ((tb % 2) == 0) & (sb == tb + 1)
            is_query = ((row_t // block) % 2) == 0
        levels.append((fac, pair, is_query))
        block *= 2
    edge = 0 if reverse else q - 1
    b_end = bq[edge:edge + 1, :]
    q_in = qv * jnp.exp(bq)
    k_out = key * jnp.exp(b_end - bq)
    outs = []
    for h in range(HG_HEADS):
        sl = slice(h * dh, (h + 1) * dh)
        att = jnp.zeros((q, q), F32)
        for fac, pair, is_query in levels:
            qt = jnp.where(is_query, qv[:, sl] * fac[:, sl], 0.0).astype(BF16)
            kt = jnp.where(is_query, 0.0, key[:, sl] * fac[:, sl]).astype(BF16)
            att = att + jnp.where(pair, _dot_nt(qt, kt), 0.0)
        vh = vv[:, sl].astype(BF16)
        st = st_ref[h]
        o = o_leaf[:, sl] + _dot(att.astype(BF16), vh) + _dot_nt(q_in[:, sl].astype(BF16), st.astype(BF16))
        st_ref[h] = jnp.exp(b_end[:, sl]) * st + _dot_tn(vh, k_out[:, sl].astype(BF16))
        outs.append(o)
    if not reverse:
        out_ref[...] = jnp.concatenate(outs, axis=1)
    else:
        of = of_ref[...]
        fin = [_rms(outs[h] + of[:, h * dh:(h + 1) * dh]) for h in range(HG_HEADS)]
        out_ref[...] = (jnp.concatenate(fin, axis=1) * ng_ref[...] * _sigmoid(g_ref[...])).astype(out_ref.dtype)


def _hgrn_pass(qr, v, f, g, o_fwd, lb, norm_g, reverse, n_ctx_chunks):
    B, T, w = v.shape
    n_chunks = T // SCAN_CHUNK
    main, _, _, const = _scan_specs(reverse, n_ctx_chunks, n_chunks, SCAN_CHUNK, B)
    head_of = jnp.arange(w) // HG_HEAD_DIM
    head_sum = (head_of[:, None] == head_of[None, :]).astype(BF16)
    return pl.pallas_call(
        _per_sample(functools.partial(_hgrn_kernel, reverse=reverse), 5, 3),
        grid=(n_chunks,),
        in_specs=[main(w), main(w), main(w), main(w), main(w), const((1, w)), const((1, w)), const((w, w))],
        out_specs=main(w),
        out_shape=jax.ShapeDtypeStruct((B, T, w), BF16 if reverse else F32),
        scratch_shapes=[pltpu.VMEM((B, HG_HEADS, HG_HEAD_DIM, HG_HEAD_DIM), F32)],
        compiler_params=pltpu.CompilerParams(dimension_semantics=("arbitrary",),
                                             vmem_limit_bytes=VMEM_LIMIT),
        name="hgrn_bwd_finish" if reverse else "hgrn_fwd",
    )(qr, v, f, g, o_fwd, lb.reshape(1, w), norm_g.reshape(1, w), head_sum)


def _hgrn_branch(qr, v, f_fwd, f_bwd, g, lb, norm_g, n_ctx_chunks):
    o_f = _hgrn_pass(qr, v, f_fwd, g, v, lb, norm_g, False, n_ctx_chunks)
    return _hgrn_pass(qr, v, f_bwd, g, o_f, lb, norm_g, True, n_ctx_chunks)


def _hyena_filter_hidden(n, w1, b1, w2, b2, freq):
    pos = jnp.arange(n, dtype=F32)
    t = pos / max(n - 1, 1)
    bands = jnp.arange(1, HY_BANDS + 1, dtype=F32)
    ang = (2.0 * math.pi / n) * pos[:, None] * bands[None, :]
    feats = jnp.concatenate([t[:, None], jnp.cos(ang), jnp.sin(ang)], axis=-1)
    hid = jnp.sin(freq[0] * (feats @ w1 + b1))
    return jnp.sin(freq[1] * (hid @ w2 + b2)), t


def _hyena_deltas():
    return jnp.abs(jnp.linspace(math.log(HY_DECAY_TARGET) / HY_LONG_PCT,
                                math.log(HY_DECAY_TARGET) / HY_SHORT_PCT, HY_WIDTH, dtype=F32))


def _hyena_filter_time(n, w1, b1, w2, b2, w3, freq):
    hid, t = _hyena_filter_hidden(n, w1, b1, w2, b2, freq)
    h = (hid @ w3).reshape(n, HY_ORDER, 2, HY_WIDTH)
    h = h * jnp.exp(-t[:, None] * _hyena_deltas()[None, :])[:, None, None, :]
    two_sided = jnp.concatenate([h[:, :, 0], jnp.zeros((1, HY_ORDER, HY_WIDTH), F32),
                                 jnp.flip(h[1:, :, 1], axis=0)], axis=0)
    return two_sided / (jnp.sum(jnp.abs(two_sided), axis=0, keepdims=True) + EPS)


def _filter_kernel(hid_ref, t_ref, wh_ref, wl_ref, delta_ref, o_ref):
    hid = hid_ref[...]
    hid_hi, hid_lo = _split_bf16(hid)
    h = _dot(hid_hi, wh_ref[...]) + _dot(hid_lo, wh_ref[...]) + _dot(hid_hi, wl_ref[...])
    decay = jnp.exp(-t_ref[...] * delta_ref[...])
    wd = HY_WIDTH
    lag0 = jnp.logical_and(pl.program_id(0) == 0, lax.broadcasted_iota(jnp.int32, decay.shape, 0) == 0)
    for g in range(2 * HY_ORDER):
        val = h[:, g * wd:(g + 1) * wd] * decay
        o_ref[g] = jnp.where(lag0, 0.0, val) if g % 2 == 1 else val


def _hyena_filter_sides(n, w1, b1, w2, b2, w3, freq):
    hid, t = _hyena_filter_hidden(n, w1, b1, w2, b2, freq)
    wh, wl = _split_bf16(w3)
    g4, wd = 2 * HY_ORDER, HY_WIDTH
    sides = pl.pallas_call(
        _filter_kernel,
        grid=(n // ROW_TILE,),
        in_specs=[pl.BlockSpec((ROW_TILE, hid.shape[1]), lambda i: (i, 0)), pl.BlockSpec((ROW_TILE, 1), lambda i: (i, 0)),
                  pl.BlockSpec(wh.shape, lambda i: (0, 0)), pl.BlockSpec(wl.shape, lambda i: (0, 0)),
                  pl.BlockSpec((1, wd), lambda i: (0, 0))],
        out_specs=pl.BlockSpec((g4, ROW_TILE, wd), lambda i: (0, i, 0)),
        out_shape=jax.ShapeDtypeStruct((g4, n, wd), F32),
        compiler_params=pltpu.CompilerParams(dimension_semantics=("parallel",)),
        name="hyena_filter_sides",
    )(hid, t.reshape(n, 1), wh, wl, _hyena_deltas().reshape(1, wd))
    norm = jnp.sum(jnp.abs(sides), axis=1).reshape(HY_ORDER, 2, wd).sum(axis=1)
    return sides, 1.0 / (norm + EPS)


def _hyena_conv_kernel(p_ref, pp_ref, pn_ref, w_ref, b_ref, v_ref, x1_ref, x2_ref, *, n_ctx_tiles, n_tiles):
    first, last = _segment_edges(pl.program_id(0), n_ctx_tiles, n_tiles)
    u = _conv3(p_ref[...], pp_ref[...], pn_ref[...], w_ref[...], b_ref[...], first, last)
    wd = HY_WIDTH
    v_ref[...] = u[:, :wd]
    x1_ref[...] = u[:, wd:2 * wd]
    x2_ref[...] = u[:, 2 * wd:]


def _hyena_conv(p, conv_w, conv_b, n_ctx_tiles):
    B, T, C = p.shape
    n_tiles = T // ROW_TILE
    main, prev, nxt, const = _scan_specs(False, n_ctx_tiles, n_tiles, ROW_TILE, B)
    body = functools.partial(_hyena_conv_kernel, n_ctx_tiles=n_ctx_tiles, n_tiles=n_tiles)
    return pl.pallas_call(
        _per_sample(body, 3, 2),
        grid=(n_tiles,),
        in_specs=[main(C), prev(C), nxt(C), const((3, C)), const((1, C))],
        out_specs=[main(HY_WIDTH)] * 3,
        out_shape=[jax.ShapeDtypeStruct((B, T, HY_WIDTH), F32)] * 3,
        compiler_params=pltpu.CompilerParams(dimension_semantics=("parallel",), vmem_limit_bytes=VMEM_LIMIT),
        name="hyena_short_conv",
    )(p, p, p, conv_w, conv_b.reshape(1, C))


DFT_N2 = 256
DFT_ROW_TILE = SUBLANES


def _split_bf16(x):
    hi = x.astype(BF16)
    return hi, (x - hi.astype(F32)).astype(BF16)


def _dft_tables(n):
    L = 2 * n
    n1_full = L // DFT_N2
    nh = n1_full // 2
    nk = nh + 1
    nkp = -(-nk // SUBLANES) * SUBLANES
    k1 = jnp.arange(nkp, dtype=jnp.int32)
    valid = (k1 < nk)[:, None]

    def stage1(rows):
        n1 = jnp.arange(rows, dtype=jnp.int32)
        ang = (2.0 * math.pi / n1_full) * ((k1[:, None] * n1[None, :]) % n1_full).astype(F32)
        return jnp.concatenate([jnp.where(valid, jnp.cos(ang), 0.0), jnp.where(valid, -jnp.sin(ang), 0.0)], axis=0)

    n1 = jnp.arange(nh, dtype=jnp.int32)
    ang = (2.0 * math.pi / n1_full) * ((n1[:, None] * k1[None, :]) % n1_full).astype(F32)
    ck = jnp.where((k1 == 0) | (k1 == nh), 1.0, 2.0) * jnp.where(k1 < nk, 1.0, 0.0) / L
    stage3 = jnp.concatenate([jnp.cos(ang) * ck[None, :], -jnp.sin(ang) * ck[None, :]], axis=1)
    n2 = jnp.arange(DFT_N2, dtype=jnp.int32)
    idx = (n2[None, :, None] * (k1[:, None, None] + n1_full * n2[None, None, :])) % L
    ang2 = (2.0 * math.pi / L) * idx.astype(F32)
    gr, gi = jnp.cos(ang2), -jnp.sin(ang2)
    grt, git = jnp.swapaxes(gr, 1, 2), jnp.swapaxes(gi, 1, 2)
    m_fwd = jnp.concatenate([jnp.concatenate([grt, -git], axis=2), jnp.concatenate([git, grt], axis=2)], axis=1)
    m_inv = jnp.swapaxes(m_fwd, 1, 2)
    eye = jnp.eye(DFT_ROW_TILE, dtype=F32)
    return dict(nh=nh, nk=nk, nkp=nkp, f1=jnp.kron(stage1(nh), eye).astype(BF16),
                f3=jnp.kron(stage3, eye).astype(BF16), m_fwd=m_fwd.astype(BF16), m_inv=m_inv.astype(BF16))


def _dft_in_kernel(u_ref, f_ref, o_ref):
    rows, t2, wd = u_ref.shape
    nkp = o_ref.shape[1]
    r = _dot(f_ref[...], u_ref[...].reshape(rows * t2, wd).astype(BF16))
    o_ref[0] = r[:nkp * t2].reshape(nkp, t2, wd)
    o_ref[1] = r[nkp * t2:].reshape(nkp, t2, wd)


def _dft_in(u, f1):
    G, rows, n2, wd = u.shape
    t2 = DFT_ROW_TILE
    nkp = f1.shape[0] // (2 * t2)
    return pl.pallas_call(
        _dft_in_kernel,
        grid=(G, n2 // t2),
        in_specs=[pl.BlockSpec((None, rows, t2, wd), lambda g, j: (g, 0, j, 0)),
                  pl.BlockSpec(f1.shape, lambda g, j: (0, 0))],
        out_specs=pl.BlockSpec((None, 2, nkp, t2, wd), lambda g, j: (g, 0, 0, j, 0)),
        out_shape=jax.ShapeDtypeStruct((G, 2, nkp, n2, wd), F32),
        compiler_params=pltpu.CompilerParams(dimension_semantics=("parallel", "parallel"),
                                             vmem_limit_bytes=VMEM_LIMIT),
        name="hyena_dft_rows",
    )(u, f1)


def _spectrum_kernel(af_ref, ab_ref, sc_ref, m_ref, o_ref, *, nk):
    k1 = pl.program_id(0)
    half = DFT_N2

    @pl.when(k1 < nk)
    def _():
        m = m_ref[...]
        uf = _dot(m, jnp.concatenate([af_ref[0], af_ref[1]], axis=0).astype(BF16))
        ub = _dot(m, jnp.concatenate([ab_ref[0], ab_ref[1]], axis=0).astype(BF16))
        o_ref[0] = (uf[:half] + ub[:half]) * sc_ref[...]
        o_ref[1] = (uf[half:] - ub[half:]) * sc_ref[...]

    @pl.when(k1 >= nk)
    def _():
        o_ref[...] = jnp.zeros_like(o_ref)


def _spectrum(a, scale, tables):
    _, _, nkp, n2, wd = a.shape
    slab = lambda g_of: pl.BlockSpec((None, 2, None, n2, wd), g_of)
    return pl.pallas_call(
        functools.partial(_spectrum_kernel, nk=tables["nk"]),
        grid=(nkp, HY_ORDER),
        in_specs=[slab(lambda k, o: (2 * o, 0, k, 0, 0)), slab(lambda k, o: (2 * o + 1, 0, k, 0, 0)),
                  pl.BlockSpec((None, 1, wd), lambda k, o: (o, 0, 0)),
                  pl.BlockSpec((None, 2 * n2, 2 * n2), lambda k, o: (k, 0, 0))],
        out_specs=slab(lambda k, o: (o, 0, k, 0, 0)),
        out_shape=jax.ShapeDtypeStruct((HY_ORDER, 2, nkp, n2, wd), F32),
        compiler_params=pltpu.CompilerParams(dimension_semantics=("parallel", "parallel"),
                                             vmem_limit_bytes=VMEM_LIMIT),
        name="hyena_spectrum",
    )(a, a, scale, tables["m_fwd"])


def _dft_mid_kernel(a_ref, h_ref, mf_ref, mi_ref, o_ref, *, nk):
    k1 = pl.program_id(0)
    half = DFT_N2

    @pl.when(k1 < nk)
    def _():
        u = _dot(mf_ref[...], jnp.concatenate([a_ref[0], a_ref[1]], axis=0).astype(BF16))
        ur, ui = u[:half], u[half:]
        hr, hi = h_ref[0], h_ref[1]
        v = jnp.concatenate([ur * hr - ui * hi, ur * hi + ui * hr], axis=0)
        y = _dot(mi_ref[...], v.astype(BF16))
        o_ref[0] = y[:half]
        o_ref[1] = y[half:]

    @pl.when(k1 >= nk)
    def _():
        o_ref[...] = jnp.zeros_like(o_ref)


def _dft_mid(a, spec, order, tables):
    G, _, nkp, n2, wd = a.shape
    slab = lambda g_of: pl.BlockSpec((None, 2, None, n2, wd), g_of)
    mat = pl.BlockSpec((None, 2 * n2, 2 * n2), lambda k, g: (k, 0, 0))
    return pl.pallas_call(
        functools.partial(_dft_mid_kernel, nk=tables["nk"]),
        grid=(nkp, G),
        in_specs=[slab(lambda k, g: (g, 0, k, 0, 0)), slab(lambda k, g: (order, 0, k, 0, 0)), mat, mat],
        out_specs=slab(lambda k, g: (g, 0, k, 0, 0)),
        out_shape=jax.ShapeDtypeStruct(a.shape, F32),
        compiler_params=pltpu.CompilerParams(dimension_semantics=("parallel", "parallel"),
                                             vmem_limit_bytes=VMEM_LIMIT),
        name="hyena_dft_mid",
    )(a, spec, tables["m_fwd"], tables["m_inv"])


def _dft_out_kernel(b_ref, u_ref, x_ref, skip_ref, f_ref, o_ref):
    rows, t2, wd = u_ref.shape
    nkp = b_ref.shape[1]
    bb = jnp.concatenate([b_ref[0].reshape(nkp * t2, wd), b_ref[1].reshape(nkp * t2, wd)], axis=0)
    y = _dot(f_ref[...], bb.astype(BF16)).reshape(rows, t2, wd)
    o_ref[...] = x_ref[...] * (y + u_ref[...] * skip_ref[...])


def _dft_out(bq, u, xg, skip_row, f3):
    G, rows, n2, wd = u.shape
    nkp = bq.shape[2]
    t2 = DFT_ROW_TILE
    tile = pl.BlockSpec((None, rows, t2, wd), lambda g, j: (g, 0, j, 0))
    return pl.pallas_call(
        _dft_out_kernel,
        grid=(G, n2 // t2),
        in_specs=[pl.BlockSpec((None, 2, nkp, t2, wd), lambda g, j: (g, 0, 0, j, 0)), tile, tile,
                  pl.BlockSpec((1, wd), lambda g, j: (0, 0)), pl.BlockSpec(f3.shape, lambda g, j: (0, 0))],
        out_specs=tile,
        out_shape=jax.ShapeDtypeStruct(u.shape, F32),
        compiler_params=pltpu.CompilerParams(dimension_semantics=("parallel", "parallel"),
                                             vmem_limit_bytes=VMEM_LIMIT),
        name="hyena_dft_rows_inverse",
    )(bq, u, xg, skip_row, f3)


def _hyena_latent(v, x1, x2, sides, scale, skip):
    B, n, wd = v.shape
    tb = _dft_tables(n)
    view = lambda t: t.reshape(t.shape[0], tb["nh"], DFT_N2, wd)
    spec = _spectrum(_dft_in(view(sides), tb["f1"]), scale.reshape(HY_ORDER, 1, wd), tb)
    z = view(v)
    for o, xg in enumerate((x1, x2)):
        bq = _dft_mid(_dft_in(z, tb["f1"]), spec, o, tb)
        z = _dft_out(bq, z, view(xg), skip[o].astype(F32).reshape(1, wd), tb["f3"])
    return z.reshape(B, n, wd)


def _hyena_context(v, x1, x2, filt, skip):
    n = v.shape[1]
    spec = jnp.fft.rfft(filt, axis=0)
    z = v
    for o, xg in enumerate((x1, x2)):
        zf = jnp.fft.rfft(z, n=2 * n, axis=1)
        y = jnp.fft.irfft(zf * spec[:, o], n=2 * n, axis=1)[:, :n]
        z = xg * (y + z * skip[o].astype(F32))
    return z


def _merge_kernel(ya_ref, yb_ref, yc_ref, yd_ref, gate_ref, x_ref, g1_ref, sh_ref, sc_ref, ng_ref,
                  wb_ref, wo_ref, wr_ref, x1_ref, h2_ref, aff_ref):
    d = x_ref.shape[1]
    merged = None
    for i, y_ref in enumerate((ya_ref, yb_ref, yc_ref, yd_ref)):
        term = _sigmoid(gate_ref[:, i * d:(i + 1) * d]) * _dot(y_ref[...], wb_ref[i])
        merged = term if merged is None else merged + term
    x1 = x_ref[...] + g1_ref[...] * _dot(merged.astype(BF16), wo_ref[...])
    x1_ref[...] = x1
    h2 = _rms(x1) * ng_ref[...] * (1.0 + sc_ref[...]) + sh_ref[...]
    h2_ref[...] = h2.astype(h2_ref.dtype)
    h2_hi, h2_lo = _split_bf16(h2)
    logits = _dot(h2_hi, wr_ref[0]) + _dot(h2_lo, wr_ref[0]) + _dot(h2_hi, wr_ref[1])
    lane = lax.broadcasted_iota(jnp.int32, logits.shape, 1)
    logits = jnp.where(lane < N_EXPERTS, logits, MASK_NEG)
    e = jnp.exp(logits - jnp.max(logits, axis=1, keepdims=True))
    aff_ref[...] = e / jnp.sum(e, axis=1, keepdims=True)


def _merge(ys, gates, xc, g1, shift, scale, gain, wb, wo, wr, n_ctx_tiles):
    B, T, D = xc.shape
    w = BRANCH_W
    tile = lambda width: pl.BlockSpec((None, ROW_TILE, width), lambda b, i: (b, i, 0))
    mod_spec = pl.BlockSpec((None, None, 1, D), lambda b, i: (b, jnp.where(i < n_ctx_tiles, 1, 0), 0, 0))
    const = lambda shape: pl.BlockSpec(shape, lambda b, i: (0,) * len(shape))
    return pl.pallas_call(
        _merge_kernel,
        grid=(B, T // ROW_TILE),
        in_specs=[tile(w), tile(w), tile(w), tile(w), tile(N_BRANCHES * D), tile(D), mod_spec, mod_spec, mod_spec,
                  const((1, D)), const(wb.shape), const(wo.shape), const(wr.shape)],
        out_specs=[tile(D), tile(D), tile(LANES)],
        out_shape=[jax.ShapeDtypeStruct((B, T, D), F32), jax.ShapeDtypeStruct((B, T, D), BF16),
                   jax.ShapeDtypeStruct((B, T, LANES), F32)],
        compiler_params=pltpu.CompilerParams(dimension_semantics=("parallel", "parallel"),
                                             vmem_limit_bytes=VMEM_LIMIT),
        name="merge_out_router",
    )(*ys, gates, xc, g1, shift, scale, gain.reshape(1, D), wb, wo, wr)


def _expert_kernel(x_ref, s_ref, wg_ref, wu_ref, wd_ref, o_ref):
    x = x_ref[...]
    hid = _silu(_dot(x, wg_ref[...])) * _dot(x, wu_ref[...])
    o_ref[...] = _dot(hid.astype(BF16), wd_ref[...]) * s_ref[...]


def _experts(xg, score, wg, wu, wd):
    E, R, D = xg.shape
    F = wg.shape[2]
    tm = min(R, 512)
    return pl.pallas_call(
        _expert_kernel,
        grid=(E, R // tm),
        in_specs=[pl.BlockSpec((None, tm, D), lambda e, i: (e, i, 0)),
                  pl.BlockSpec((None, tm, 1), lambda e, i: (e, i, 0)),
                  pl.BlockSpec((None, D, F), lambda e, i: (e, 0, 0)),
                  pl.BlockSpec((None, D, F), lambda e, i: (e, 0, 0)),
                  pl.BlockSpec((None, F, D), lambda e, i: (e, 0, 0))],
        out_specs=pl.BlockSpec((None, tm, D), lambda e, i: (e, i, 0)),
        out_shape=jax.ShapeDtypeStruct((E, R, D), F32),
        compiler_params=pltpu.CompilerParams(dimension_semantics=("parallel", "arbitrary"),
                                             vmem_limit_bytes=VMEM_LIMIT),
        name="expert_swiglu",
    )(xg, score, wg, wu, wd)


def _expert_choice(h2, aff, seg_start, seg_len, wg, wu, wd):
    B, T, D = h2.shape
    cap = EC_CAPACITY * seg_len // N_EXPERTS
    a = aff[:, seg_start:seg_start + seg_len, :N_EXPERTS]
    score, idx = lax.top_k(jnp.swapaxes(a, 1, 2), cap)
    rows = idx + seg_start + (jnp.arange(B, dtype=idx.dtype) * T)[:, None, None]
    rows = jnp.swapaxes(rows, 0, 1).reshape(N_EXPERTS, B * cap)
    score = jnp.swapaxes(score, 0, 1).reshape(N_EXPERTS, B * cap, 1)
    xg = jnp.take(h2.reshape(B * T, D), rows, axis=0)
    ye = _experts(xg, score, wg, wu, wd)
    return rows.reshape(-1), ye.reshape(-1, D)


def _residual_kernel(x_ref, m_ref, g2_ref, gain_ref, o_ref, *, final):
    x = x_ref[...] + g2_ref[...] * m_ref[...]
    o_ref[...] = _rms(x) * gain_ref[...] if final else x


def _residual(x1, moe, g2, gain, final, n_ctx_tiles):
    B, T, D = x1.shape
    tile = pl.BlockSpec((None, ROW_TILE, D), lambda b, i: (b, i, 0))
    mod_spec = pl.BlockSpec((None, None, 1, D), lambda b, i: (b, jnp.where(i < n_ctx_tiles, 1, 0), 0, 0))
    return pl.pallas_call(
        functools.partial(_residual_kernel, final=final),
        grid=(B, T // ROW_TILE),
        in_specs=[tile, tile, mod_spec, pl.BlockSpec((1, D), lambda b, i: (0, 0))],
        out_specs=tile,
        out_shape=jax.ShapeDtypeStruct((B, T, D), F32),
        compiler_params=pltpu.CompilerParams(dimension_semantics=("parallel", "parallel")),
        name="moe_residual_final_norm" if final else "moe_residual",
    )(x1, moe, g2, gain.reshape(1, D))


def _to_col_major(t):
    b, n = t.shape[:2]
    rows = n // GRID_W
    return t.reshape((b, rows, GRID_W) + t.shape[2:]).swapaxes(1, 2).reshape(t.shape)


def _to_row_major(t):
    b, n = t.shape[:2]
    rows = n // GRID_W
    return t.reshape((b, GRID_W, rows) + t.shape[2:]).swapaxes(1, 2).reshape(t.shape)


def _pad_cols(w, width):
    return jnp.pad(w, ((0, 0), (0, width - w.shape[1])))


def kernel(x, c, ctx, c_ctx, w_ada, b_ada, norm1_g, norm2_g, w_in, ssd_conv_w, ssd_conv_b, ssd_dt_bias, ssd_a_log, ssd_d, ssd_norm_g, hy_conv_w, hy_conv_b, hy_w1, hy_b1, hy_w2, hy_b2, hy_w3, hy_freq, hy_skip, ml_conv_w, ml_conv_b, ml_i_bias, ml_f_bias, ml_norm_g, hg_lb_logits, hg_norm_g, w_branch, w_out, w_router, w_gate, w_up, w_down, final_g):
    B, n, D = x.shape
    n_ctx = ctx.shape[1]
    T = n_ctx + n
    assert n_ctx % ROW_TILE == 0 and n % ROW_TILE == 0 and n % GRID_W == 0
    assert n_ctx % SCAN_CHUNK == 0 and n % SCAN_CHUNK == 0
    n_ctx_tiles = n_ctx // ROW_TILE
    n_ctx_chunks = n_ctx // SCAN_CHUNK
    w = BRANCH_W

    p_lb = jax.nn.softmax(hg_lb_logits.astype(F32), axis=0)
    lower_bounds = jnp.maximum(jnp.cumsum(p_lb, axis=0) - p_lb[0], 0.0)
    xc = jnp.concatenate([ctx, x], axis=1)
    depth = w_in.shape[0]
    for l in range(depth):
        last_layer = l == depth - 1
        mod = jax.nn.silu(c) @ w_ada[l] + b_ada[l]
        mod_c = jax.nn.silu(c_ctx) @ w_ada[l] + b_ada[l]
        mods = jnp.stack([mod, jnp.broadcast_to(mod_c, mod.shape)], axis=1).reshape(B, 2, 6, 1, D)
        sh1, sc1, g1, sh2, sc2, g2 = (mods[:, :, i] for i in range(6))

        wl = w_in[l].astype(BF16)
        o_ssd, o_hy, o_ml, o_hg, o_gate = 0, SSD_COLS, SSD_COLS + HY_COLS, SSD_COLS + HY_COLS + ML_COLS, \
            SSD_COLS + HY_COLS + ML_COLS + HG_COLS
        cols = lambda a, b_: wl[:, a:b_]
        z, xs, bc, dt, qk, v, o, gts = _proj(xc, sh1, sc1, norm1_g[l], [
            cols(o_ssd, o_ssd + w), cols(o_ssd + w, o_ssd + 2 * w), cols(o_ssd + 2 * w, o_ssd + w + SSD_CONV_CH),
            _pad_cols(cols(o_ssd + w + SSD_CONV_CH, o_hy), LANES),
            cols(o_ml, o_ml + 2 * w), cols(o_ml + 2 * w, o_ml + 3 * w), cols(o_ml + 3 * w, o_ml + 4 * w),
            _pad_cols(cols(o_ml + 4 * w, o_hg), LANES)], n_ctx_tiles)
        p_hy, gate_pre = _proj(xc, sh1, sc1, norm1_g[l], [cols(o_hy, o_ml), cols(o_gate, o_gate + N_BRANCHES * D)],
                               n_ctx_tiles)
        xc_cm = jnp.concatenate([xc[:, :n_ctx], _to_col_major(xc[:, n_ctx:])], axis=1)
        hq, hi, hff, hfb, hgt = _proj(xc_cm, sh1, sc1, norm1_g[l],
                                      [cols(o_hg + i * w, o_hg + (i + 1) * w) for i in range(5)], n_ctx_tiles)

        ya = _ssd_branch(z, xs, bc, dt, ssd_conv_w[l], ssd_conv_b[l], ssd_dt_bias[l], ssd_a_log[l], ssd_d[l],
                         ssd_norm_g[l], n_ctx_chunks)
        yc = _mlstm_branch(qk, v, o, gts, ml_conv_w[l], ml_conv_b[l], ml_i_bias[l], ml_f_bias[l], ml_norm_g[l],
                           n_ctx_chunks)
        yd_cm = _hgrn_branch(hq, hi, hff, hfb, hgt, lower_bounds[l], hg_norm_g[l], n_ctx_chunks)
        yd = jnp.concatenate([yd_cm[:, :n_ctx], _to_row_major(yd_cm[:, n_ctx:])], axis=1)
        hv, hx1, hx2 = _hyena_conv(p_hy, hy_conv_w[l], hy_conv_b[l], n_ctx_tiles)
        hy_params = tuple(p_[l].astype(F32) for p_ in (hy_w1, hy_b1, hy_w2, hy_b2, hy_w3, hy_freq))
        lat = lambda t: t[:, n_ctx:]
        sides, side_scale = _hyena_filter_sides(n, *hy_params)
        yb_lat = _hyena_latent(lat(hv), lat(hx1), lat(hx2), sides, side_scale, hy_skip[l])
        if last_layer:
            yb_ctx = jnp.zeros((B, n_ctx, w), F32)
        else:
            head = lambda t: t[:, :n_ctx]
            yb_ctx = _hyena_context(head(hv), head(hx1), head(hx2), _hyena_filter_time(n_ctx, *hy_params), hy_skip[l])
        yb = jnp.concatenate([yb_ctx, yb_lat], axis=1).astype(BF16)

        wr = jnp.stack(_split_bf16(_pad_cols(w_router[l].astype(F32), LANES)))
        x1, h2, aff = _merge((ya, yb, yc, yd), gate_pre, xc, g1, sh2, sc2, norm2_g[l],
                             w_branch[l].astype(BF16), w_out[l].astype(BF16), wr, n_ctx_tiles)
        wg, wu, wd = w_gate[l].astype(BF16), w_up[l].astype(BF16), w_down[l].astype(BF16)
        rows, vals = _expert_choice(h2, aff, n_ctx, n, wg, wu, wd)
        moe = jnp.zeros((B * T, D), F32).at[rows].add(vals)
        if not last_layer:
            rows_c, vals_c = _expert_choice(h2, aff, 0, n_ctx, wg, wu, wd)
            moe = moe.at[rows_c].add(vals_c)
        xc = _residual(x1, moe.reshape(B, T, D), g2, final_g, last_layer, n_ctx_tiles)
    return xc[:, n_ctx:]
```

```python
import functools
import math
import jax
import jax.numpy as jnp
from jax import lax
from jax.experimental import pallas as pl
from jax.experimental.pallas import tpu as pltpu


D_MODEL = 1024
DEPTH = 2
GRID_W = 64
BRANCH_W = D_MODEL // 2
N_BRANCHES = 4
EPS = 1e-6
MASK_NEG = -1e30
LB_FLOOR = 1e-30

SSD_HEAD_DIM = 64
SSD_HEADS = BRANCH_W // SSD_HEAD_DIM
SSD_GROUPS = 2
SSD_STATE = 64
SSD_CONV_CH = BRANCH_W + 2 * SSD_GROUPS * SSD_STATE
SSD_COLS = BRANCH_W + SSD_CONV_CH + 2 * SSD_HEADS

HY_WIDTH = BRANCH_W
HY_ORDER = 2
HY_BANDS = 8
HY_DECAY_TARGET = 1e-2
HY_SHORT_PCT = 0.3
HY_LONG_PCT = 1.5
HY_COLS = (HY_ORDER + 1) * HY_WIDTH

ML_HEADS = 4
ML_HEAD_DIM = BRANCH_W // ML_HEADS
ML_COLS = 4 * BRANCH_W + 4 * ML_HEADS

HG_HEADS = 4
HG_HEAD_DIM = BRANCH_W // HG_HEADS
HG_COLS = 5 * BRANCH_W
HG_LEAF = 8

N_EXPERTS = 16
EC_CAPACITY = 2

LANES = 128
SUBLANES = 8
ROW_TILE = 256
SCAN_CHUNK = 256
VMEM_LIMIT = 56 * 1024 * 1024

F32 = jnp.float32
BF16 = jnp.bfloat16


def _dot(a, b):
    return jnp.dot(a, b, preferred_element_type=F32)


def _dot_nt(a, b):
    return lax.dot_general(a, b, (((1,), (1,)), ((), ())), preferred_element_type=F32)


def _dot_tn(a, b):
    return lax.dot_general(a, b, (((0,), (0,)), ((), ())), preferred_element_type=F32)


def _dot_mask(mask, x):
    hi = x.astype(BF16)
    rem = x - hi.astype(F32)
    mid = rem.astype(BF16)
    lo = (rem - mid.astype(F32)).astype(BF16)
    return _dot(mask, hi) + _dot(mask, mid) + _dot(mask, lo)


def _sigmoid(x):
    return 0.5 * jnp.tanh(0.5 * x) + 0.5


def _silu(x):
    return x * _sigmoid(x)


def _softplus(x):
    return jnp.maximum(x, 0.0) + jnp.log1p(jnp.exp(-jnp.abs(x)))


def _log_sigmoid(x):
    return jnp.minimum(x, 0.0) - jnp.log1p(jnp.exp(-jnp.abs(x)))


def _rms(x):
    return x * lax.rsqrt(jnp.mean(x * x, axis=-1, keepdims=True) + EPS)


def _proj_kernel(x_ref, sh_ref, sc_ref, g_ref, *refs, n_out):
    h = (_rms(x_ref[...]) * g_ref[...] * (1.0 + sc_ref[...]) + sh_ref[...]).astype(BF16)
    for w_ref, o_ref in zip(refs[:n_out], refs[n_out:]):
        o_ref[...] = _dot(h, w_ref[...])


def _proj(xc, shift, scale, gain, weights, n_ctx_tiles):
    B, T, D = xc.shape
    mod_spec = pl.BlockSpec((None, None, 1, D), lambda b, i: (b, jnp.where(i < n_ctx_tiles, 1, 0), 0, 0))
    return pl.pallas_call(
        functools.partial(_proj_kernel, n_out=len(weights)),
        grid=(B, T // ROW_TILE),
        in_specs=[pl.BlockSpec((None, ROW_TILE, D), lambda b, i: (b, i, 0)), mod_spec, mod_spec,
                  pl.BlockSpec((1, D), lambda b, i: (0, 0))]
                 + [pl.BlockSpec(w.shape, lambda b, i: (0, 0)) for w in weights],
        out_specs=[pl.BlockSpec((None, ROW_TILE, w.shape[1]), lambda b, i: (b, i, 0)) for w in weights],
        out_shape=[jax.ShapeDtypeStruct((B, T, w.shape[1]), F32) for w in weights],
        compiler_params=pltpu.CompilerParams(dimension_semantics=("parallel", "parallel"),
                                             vmem_limit_bytes=VMEM_LIMIT),
        name="norm_mod_proj",
    )(xc, shift, scale, gain.reshape(1, D), *weights)


def _scan_chunk(step, reverse, n_ctx_chunks, n_chunks):
    if not reverse:
        return step
    return jnp.where(step < n_ctx_chunks, n_ctx_chunks - 1 - step, n_chunks - 1 - (step - n_ctx_chunks))


def _scan_specs(reverse, n_ctx_chunks, n_chunks, chunk, batch):
    cidx = lambda s: _scan_chunk(s, reverse, n_ctx_chunks, n_chunks)
    per_tile = chunk // SUBLANES
    n_tiles = n_chunks * per_tile

    def main(width):
        return pl.BlockSpec((batch, chunk, width), lambda s: (0, cidx(s), 0))

    def prev(width):
        return pl.BlockSpec((batch, SUBLANES, width), lambda s: (0, jnp.maximum(cidx(s) * per_tile - 1, 0), 0))

    def nxt(width):
        return pl.BlockSpec((batch, SUBLANES, width),
                            lambda s: (0, jnp.minimum((cidx(s) + 1) * per_tile, n_tiles - 1), 0))

    def const(shape):
        return pl.BlockSpec(shape, lambda s: (0,) * len(shape))

    return main, prev, nxt, const


def _per_sample(body, n_inputs, n_shared):
    def kern(*refs):
        for b in range(refs[0].shape[0]):
            body(*[r if n_inputs <= i < n_inputs + n_shared else r.at[b] for i, r in enumerate(refs)])
    return kern


def _segment_edges(chunk_idx, n_ctx_chunks, n_chunks):
    first = jnp.logical_or(chunk_idx == 0, chunk_idx == n_ctx_chunks)
    last = jnp.logical_or(chunk_idx == n_ctx_chunks - 1, chunk_idx == n_chunks - 1)
    return first, last


def _conv3(u, u_prev_tile, u_next_tile, w, bias, first, last):
    q = u.shape[0]
    rows = lax.broadcasted_iota(jnp.int32, u.shape, 0)
    before = jnp.where(first, 0.0, u_prev_tile[SUBLANES - 1:SUBLANES, :])
    after = jnp.where(last, 0.0, u_next_tile[0:1, :])
    u_m1 = jnp.where(rows == 0, before, pltpu.roll(u, 1, 0))
    u_p1 = jnp.where(rows == q - 1, after, pltpu.roll(u, q - 1, 0))
    return w[0:1, :] * u_m1 + w[1:2, :] * u + w[2:3, :] * u_p1 + bias


def _order_masks(q, reverse):
    r = lax.broadcasted_iota(jnp.int32, (q, q), 0)
    c = lax.broadcasted_iota(jnp.int32, (q, q), 1)
    return (r <= c) if reverse else (r >= c)


def _ssd_kernel(z_ref, x_ref, xp_ref, xn_ref, bc_ref, bcp_ref, bcn_ref, dt_ref, yf_ref,
                cwx_ref, cbx_ref, cwbc_ref, cbbc_ref, dtb_ref, nega_ref, dskip_ref, ng_ref,
                out_ref, st_ref, *, reverse, n_ctx_chunks, n_chunks):
    q = x_ref.shape[0]
    step = pl.program_id(0)
    cidx = _scan_chunk(step, reverse, n_ctx_chunks, n_chunks)
    first, last = _segment_edges(cidx, n_ctx_chunks, n_chunks)

    @pl.when(step == 0)
    def _():
        st_ref[...] = jnp.zeros_like(st_ref)

    xs = _silu(_conv3(x_ref[...], xp_ref[...], xn_ref[...], cwx_ref[...], cbx_ref[...], first, last))
    bc = _silu(_conv3(bc_ref[...], bcp_ref[...], bcn_ref[...], cwbc_ref[...], cbbc_ref[...], first, last))
    dt_all = _softplus(dt_ref[...] + dtb_ref[...])
    da_all = dt_all * nega_ref[...]
    mask = _order_masks(q, reverse)
    acs = _dot_mask(mask.astype(BF16), da_all)
    acs_t = acs.T
    dt_t = dt_all.T
    edge = 0 if reverse else q - 1
    gn = SSD_STATE
    per_group = SSD_HEADS // SSD_GROUPS
    gw = per_group * SSD_HEAD_DIM
    lane_head = lax.broadcasted_iota(jnp.int32, (q, gw), 1) // SSD_HEAD_DIM

    def per_head_lanes(cols):
        out = cols[-1]
        for i in range(per_group - 2, -1, -1):
            out = jnp.where(lane_head == i, cols[i], out)
        return jnp.broadcast_to(out, (q, gw))

    ys = []
    for g in range(SSD_GROUPS):
        bg = bc[:, g * gn:(g + 1) * gn].astype(BF16)
        cg = bc[:, (SSD_GROUPS + g) * gn:(SSD_GROUPS + g + 1) * gn].astype(BF16)
        cb = _dot_nt(cg, bg)
        xg = xs[:, g * gw:(g + 1) * gw]
        cols = [(SSD_HEADS if reverse else 0) + g * per_group + hh for hh in range(per_group)]
        a_cols = [acs[:, c:c + 1] for c in cols]
        a_rows = [acs_t[c:c + 1, :] for c in cols]
        a_ends = [r[:, edge:edge + 1] for r in a_rows]
        diag = []
        for hh, c in enumerate(cols):
            decay = jnp.where(mask, jnp.exp(jnp.minimum(a_cols[hh] - a_rows[hh], 0.0)), 0.0)
            scores = (cb * decay * dt_t[c:c + 1, :]).astype(BF16)
            diag.append(_dot(scores, xg[:, hh * SSD_HEAD_DIM:(hh + 1) * SSD_HEAD_DIM].astype(BF16)))
        a_lanes = per_head_lanes(a_cols)
        end_lanes = per_head_lanes(a_ends)
        st = st_ref[g]
        ys.append(jnp.concatenate(diag, axis=1) + _dot(cg, st.astype(BF16)) * jnp.exp(a_lanes))
        to_end = jnp.exp(end_lanes - a_lanes) * per_head_lanes([dt_all[:, c:c + 1] for c in cols])
        st_ref[g] = jnp.exp(end_lanes[0:1, :]) * st + _dot_tn(bg, (xg * to_end).astype(BF16))
    y = jnp.concatenate(ys, axis=1)
    if not reverse:
        out_ref[...] = y
    else:
        y = (y + yf_ref[...] + dskip_ref[...] * xs) * _silu(z_ref[...])
        out_ref[...] = (_rms(y) * ng_ref[...]).astype(out_ref.dtype)


def _ssd_pass(z, x, bc, dt, y_fwd, consts, reverse, n_ctx_chunks):
    B, T, _ = x.shape
    n_chunks = T // SCAN_CHUNK
    main, prev, nxt, const = _scan_specs(reverse, n_ctx_chunks, n_chunks, SCAN_CHUNK, B)
    w = BRANCH_W
    wbc = 2 * SSD_GROUPS * SSD_STATE
    in_specs = [main(w), main(w), prev(w), nxt(w), main(wbc), prev(wbc), nxt(wbc), main(LANES), main(w),
                const((3, w)), const((1, w)), const((3, wbc)), const((1, wbc)), const((1, LANES)),
                const((1, LANES)), const((1, w)), const((1, w))]
    body = functools.partial(_ssd_kernel, reverse=reverse, n_ctx_chunks=n_ctx_chunks, n_chunks=n_chunks)
    return pl.pallas_call(
        _per_sample(body, 9, 8),
        grid=(n_chunks,),
        in_specs=in_specs,
        out_specs=main(w),
        out_shape=jax.ShapeDtypeStruct((B, T, w), BF16 if reverse else F32),
        scratch_shapes=[pltpu.VMEM((B, SSD_GROUPS, SSD_STATE, BRANCH_W // SSD_GROUPS), F32)],
        compiler_params=pltpu.CompilerParams(dimension_semantics=("arbitrary",),
                                             vmem_limit_bytes=VMEM_LIMIT),
        name="ssd_bwd_finish" if reverse else "ssd_fwd",
    )(z, x, x, x, bc, bc, bc, dt, y_fwd, *consts)


def _ssd_branch(z, x, bc, dt, conv_w, conv_b, dt_bias, a_log, d_skip, norm_g, n_ctx_chunks):
    w = BRANCH_W
    pad = LANES - 2 * SSD_HEADS
    dtb = jnp.pad(dt_bias.astype(F32).reshape(1, -1), ((0, 0), (0, pad)))
    nega = jnp.pad(-jnp.exp(a_log.astype(F32)).reshape(1, -1), ((0, 0), (0, pad)))
    consts = (conv_w[:, :w], conv_b[:w].reshape(1, w), conv_w[:, w:], conv_b[w:].reshape(1, -1), dtb, nega,
              jnp.repeat(d_skip.astype(F32), SSD_HEAD_DIM).reshape(1, w), norm_g.reshape(1, w))
    y_f = _ssd_pass(z, x, bc, dt, x, consts, False, n_ctx_chunks)
    return _ssd_pass(z, x, bc, dt, y_f, consts, True, n_ctx_chunks)


def _mlstm_kernel(qk_ref, qkp_ref, qkn_ref, v_ref, o_ref, gt_ref, hf_ref, cw_ref, cb_ref, gb_ref, ng_ref,
                  out_ref, c_ref, m_ref, *, reverse, n_ctx_chunks, n_chunks):
    q = qk_ref.shape[0]
    dh = ML_HEAD_DIM
    step = pl.program_id(0)
    cidx = _scan_chunk(step, reverse, n_ctx_chunks, n_chunks)
    first, last = _segment_edges(cidx, n_ctx_chunks, n_chunks)

    @pl.when(step == 0)
    def _():
        c_ref[...] = jnp.zeros_like(c_ref)
        m_ref[...] = jnp.zeros_like(m_ref)

    qk = _silu(_conv3(qk_ref[...], qkp_ref[...], qkn_ref[...], cw_ref[...], cb_ref[...], first, last))
    gates = gt_ref[...] + gb_ref[...]
    logf_all = _log_sigmoid(gates)
    mask = _order_masks(q, reverse)
    bcum = _dot_mask(mask.astype(BF16), logf_all)
    bcum_t = bcum.T
    gates_t = gates.T
    edge = 0 if reverse else q - 1
    lane = lax.broadcasted_iota(jnp.int32, (q, dh), 1)
    ones_col = jnp.where(lane == 0, 1.0, 0.0).astype(BF16)
    hs = []
    for h in range(ML_HEADS):
        li = (ML_HEADS if reverse else 0) + h
        lf = 2 * ML_HEADS + li
        q32 = qk[:, h * dh:(h + 1) * dh]
        qh = q32.astype(BF16)
        kh = qk[:, BRANCH_W + h * dh:BRANCH_W + (h + 1) * dh] * (dh ** -0.5)
        v_aug = jnp.concatenate([v_ref[:, h * dh:(h + 1) * dh].astype(BF16), ones_col], axis=1)
        b_col = bcum[:, lf:lf + 1]
        b_row = bcum_t[lf:lf + 1, :]
        i_col = gates[:, li:li + 1]
        i_row = gates_t[li:li + 1, :]
        b_tot = b_row[:, edge:edge + 1]
        m_prev = m_ref[h:h + 1, 0:1]
        c_prev = c_ref[h]
        dmat = jnp.where(mask, b_col - b_row + i_row, MASK_NEG)
        inter = b_col + m_prev
        m_t = jnp.maximum(inter, jnp.max(dmat, axis=1, keepdims=True))
        wgt = jnp.where(mask, jnp.exp(jnp.minimum(dmat - m_t, 0.0)), 0.0) * _dot_nt(qh, kh.astype(BF16))
        w_int = jnp.exp(inter - m_t)
        lhs = jnp.concatenate([wgt.astype(BF16), (w_int * q32).astype(BF16)], axis=1)
        both = _dot(lhs, jnp.concatenate([v_aug, c_prev.astype(BF16)], axis=0))
        den = both[:, dh:dh + 1]
        hs.append(both[:, :dh] / jnp.maximum(jnp.abs(den), jnp.exp(-m_t)))
        a = b_tot - b_col + i_col
        m_new = jnp.maximum(b_tot + m_prev, jnp.max(a, axis=0, keepdims=True))
        kw = (kh * jnp.exp(a - m_new)).astype(BF16)
        c_ref[h] = jnp.exp(b_tot + m_prev - m_new) * c_prev + _dot_tn(kw, v_aug)
        m_ref[h:h + 1, :] = jnp.broadcast_to(m_new, (1, LANES))
    if not reverse:
        out_ref[...] = jnp.concatenate(hs, axis=1)
    else:
        hf = hf_ref[...]
        outs = [_rms(hs[h] + hf[:, h * dh:(h + 1) * dh]) for h in range(ML_HEADS)]
        out_ref[...] = (_sigmoid(o_ref[...]) * jnp.concatenate(outs, axis=1) * ng_ref[...]).astype(out_ref.dtype)


def _mlstm_pass(qk, v, o, gates, h_fwd, consts, reverse, n_ctx_chunks):
    B, T, _ = v.shape
    n_chunks = T // SCAN_CHUNK
    main, prev, nxt, const = _scan_specs(reverse, n_ctx_chunks, n_chunks, SCAN_CHUNK, B)
    w = BRANCH_W
    in_specs = [main(2 * w), prev(2 * w), nxt(2 * w), main(w), main(w), main(LANES), main(w),
                const((3, 2 * w)), const((1, 2 * w)), const((1, LANES)), const((1, w))]
    body = functools.partial(_mlstm_kernel, reverse=reverse, n_ctx_chunks=n_ctx_chunks, n_chunks=n_chunks)
    return pl.pallas_call(
        _per_sample(body, 7, 4),
        grid=(n_chunks,),
        in_specs=in_specs,
        out_specs=main(w),
        out_shape=jax.ShapeDtypeStruct((B, T, w), BF16 if reverse else F32),
        scratch_shapes=[pltpu.VMEM((B, ML_HEADS, ML_HEAD_DIM, 2 * ML_HEAD_DIM), F32),
                        pltpu.VMEM((B, SUBLANES, LANES), F32)],
        compiler_params=pltpu.CompilerParams(dimension_semantics=("arbitrary",),
                                             vmem_limit_bytes=VMEM_LIMIT),
        name="mlstm_bwd_finish" if reverse else "mlstm_fwd",
    )(qk, qk, qk, v, o, gates, h_fwd, *consts)


def _mlstm_branch(qk, v, o, gates, conv_w, conv_b, i_bias, f_bias, norm_g, n_ctx_chunks):
    w = BRANCH_W
    gb = jnp.concatenate([i_bias.astype(F32).reshape(-1), f_bias.astype(F32).reshape(-1)])
    gb = jnp.pad(gb.reshape(1, -1), ((0, 0), (0, LANES - 4 * ML_HEADS)))
    consts = (conv_w, conv_b.reshape(1, 2 * w), gb, norm_g.reshape(1, w))
    h_f = _mlstm_pass(qk, v, o, gates, v, consts, False, n_ctx_chunks)
    return _mlstm_pass(qk, v, o, gates, h_f, consts, True, n_ctx_chunks)


def _hgrn_level_factor(bq, block, reverse):
    q, width = bq.shape
    b3 = bq.reshape(q // block, block, width)
    blk = lax.broadcasted_iota(jnp.int32, b3.shape, 0)
    if not reverse:
        edge = b3[:, block - 1:block, :]
        before = jnp.concatenate([jnp.zeros_like(edge[:1]), edge[:-1]], axis=0)
        expo = jnp.where(blk % 2 == 1, b3 - before, edge - b3)
    else:
        edge = b3[:, 0:1, :]
        after = jnp.concatenate([edge[1:], jnp.zeros_like(edge[:1])], axis=0)
        expo = jnp.where(blk % 2 == 0, b3 - after, edge - b3)
    return jnp.exp(expo).reshape(q, width)


def _hgrn_kernel(q_ref, v_ref, f_ref, g_ref, of_ref, lb_ref, ng_ref, hsum_ref, out_ref, st_ref, *, reverse):
    q, width = q_ref.shape
    dh = HG_HEAD_DIM
    step = pl.program_id(0)

    @pl.when(step == 0)
    def _():
        st_ref[...] = jnp.zeros_like(st_ref)

    lb = lb_ref[...]
    pre = f_ref[...]
    log_lb = jnp.log(jnp.maximum(lb, LB_FLOOR))
    log_ub = jnp.log1p(-lb)
    e_pre = jnp.exp(-jnp.abs(pre))
    lo = log_ub + jnp.minimum(pre, 0.0) - jnp.log1p(e_pre)
    logf = jnp.maximum(log_lb, lo) + jnp.log1p(jnp.exp(-jnp.abs(log_lb - lo)))
    key = (1.0 - lb) * (jnp.where(pre >= 0.0, e_pre, 1.0) / (1.0 + e_pre))
    qv = _silu(q_ref[...])
    bq = _dot_mask(_order_masks(q, reverse).astype(BF16), logf)
    vv = v_ref[...]

    leaves = (q // HG_LEAF, HG_LEAF, width)
    b3, q3, k3, v3 = bq.reshape(leaves), qv.reshape(leaves), key.reshape(leaves), vv.reshape(leaves)
    tt = lax.broadcasted_iota(jnp.int32, leaves, 1)
    prods = []
    for s in range(HG_LEAF):
        ok = (tt <= s) if reverse else (tt >= s)
        e = jnp.exp(jnp.where(ok, b3 - b3[:, s:s + 1, :], MASK_NEG))
        prods.append((q3 * e * k3[:, s:s + 1, :]).reshape(q, width).astype(BF16))
    att = _dot(jnp.concatenate(prods, axis=0), hsum_ref[...])
    o_leaf = jnp.zeros(leaves, F32)
    for s in range(HG_LEAF):
        o_leaf = o_leaf + att[s * q:(s + 1) * q].reshape(leaves) * v3[:, s:s + 1, :]
    o_leaf = o_leaf.reshape(q, width)

    t_idx = lax.broadcasted_iota(jnp.int32, (q, q), 0)
    s_idx = lax.broadcasted_iota(jnp.int32, (q, q), 1)
    row_t = lax.broadcasted_iota(jnp.int32, (q, dh), 0)
    levels = []
    block = HG_LEAF
    while block < q:
        fac = _hgrn_level_factor(bq, block, reverse)
        tb, sb = t_idx // block, s_idx // block
        if not reverse:
            pair = ((tb % 2) == 1) & (sb == tb - 1)
            is_query = ((row_t // block) % 2) == 1
        else:
            pair = ((tb % 2) == 0) & (sb == tb + 1)
            is_query = ((row_t // block) % 2) == 0
        levels.append((fac, pair, is_query))
        block *= 2
    edge = 0 if reverse else q - 1
    b_end = bq[edge:edge + 1, :]
    q_in = qv * jnp.exp(bq)
    k_out = key * jnp.exp(b_end - bq)
    outs = []
    for h in range(HG_HEADS):
        sl = slice(h * dh, (h + 1) * dh)
        att = jnp.zeros((q, q), F32)
        for fac, pair, is_query in levels:
            qt = jnp.where(is_query, qv[:, sl] * fac[:, sl], 0.0).astype(BF16)
            kt = jnp.where(is_query, 0.0, key[:, sl] * fac[:, sl]).astype(BF16)
            att = att + jnp.where(pair, _dot_nt(qt, kt), 0.0)
        vh = vv[:, sl].astype(BF16)
        st = st_ref[h]
        o = o_leaf[:, sl] + _dot(att.astype(BF16), vh) + _dot_nt(q_in[:, sl].astype(BF16), st.astype(BF16))
        st_ref[h] = jnp.exp(b_end[:, sl]) * st + _dot_tn(vh, k_out[:, sl].astype(BF16))
        outs.append(o)
    if not reverse:
        out_ref[...] = jnp.concatenate(outs, axis=1)
    else:
        of = of_ref[...]
        fin = [_rms(outs[h] + of[:, h * dh:(h + 1) * dh]) for h in range(HG_HEADS)]
        out_ref[...] = (jnp.concatenate(fin, axis=1) * ng_ref[...] * _sigmoid(g_ref[...])).astype(out_ref.dtype)


def _hgrn_pass(qr, v, f, g, o_fwd, lb, norm_g, reverse, n_ctx_chunks):
    B, T, w = v.shape
    n_chunks = T // SCAN_CHUNK
    main, _, _, const = _scan_specs(reverse, n_ctx_chunks, n_chunks, SCAN_CHUNK, B)
    head_of = jnp.arange(w) // HG_HEAD_DIM
    head_sum = (head_of[:, None] == head_of[None, :]).astype(BF16)
    return pl.pallas_call(
        _per_sample(functools.partial(_hgrn_kernel, reverse=reverse), 5, 3),
        grid=(n_chunks,),
        in_specs=[main(w), main(w), main(w), main(w), main(w), const((1, w)), const((1, w)), const((w, w))],
        out_specs=main(w),
        out_shape=jax.ShapeDtypeStruct((B, T, w), BF16 if reverse else F32),
        scratch_shapes=[pltpu.VMEM((B, HG_HEADS, HG_HEAD_DIM, HG_HEAD_DIM), F32)],
        compiler_params=pltpu.CompilerParams(dimension_semantics=("arbitrary",),
                                             vmem_limit_bytes=VMEM_LIMIT),
        name="hgrn_bwd_finish" if reverse else "hgrn_fwd",
    )(qr, v, f, g, o_fwd, lb.reshape(1, w), norm_g.reshape(1, w), head_sum)


def _hgrn_branch(qr, v, f_fwd, f_bwd, g, lb, norm_g, n_ctx_chunks):
    o_f = _hgrn_pass(qr, v, f_fwd, g, v, lb, norm_g, False, n_ctx_chunks)
    return _hgrn_pass(qr, v, f_bwd, g, o_f, lb, norm_g, True, n_ctx_chunks)


def _hyena_filter_hidden(n, w1, b1, w2, b2, freq):
    pos = jnp.arange(n, dtype=F32)
    t = pos / max(n - 1, 1)
    bands = jnp.arange(1, HY_BANDS + 1, dtype=F32)
    ang = (2.0 * math.pi / n) * pos[:, None] * bands[None, :]
    feats = jnp.concatenate([t[:, None], jnp.cos(ang), jnp.sin(ang)], axis=-1)
    hid = jnp.sin(freq[0] * (feats @ w1 + b1))
    return jnp.sin(freq[1] * (hid @ w2 + b2)), t


def _hyena_deltas():
    return jnp.abs(jnp.linspace(math.log(HY_DECAY_TARGET) / HY_LONG_PCT,
                                math.log(HY_DECAY_TARGET) / HY_SHORT_PCT, HY_WIDTH, dtype=F32))


def _hyena_filter_time(n, w1, b1, w2, b2, w3, freq):
    hid, t = _hyena_filter_hidden(n, w1, b1, w2, b2, freq)
    h = (hid @ w3).reshape(n, HY_ORDER, 2, HY_WIDTH)
    h = h * jnp.exp(-t[:, None] * _hyena_deltas()[None, :])[:, None, None, :]
    two_sided = jnp.concatenate([h[:, :, 0], jnp.zeros((1, HY_ORDER, HY_WIDTH), F32),
                                 jnp.flip(h[1:, :, 1], axis=0)], axis=0)
    return two_sided / (jnp.sum(jnp.abs(two_sided), axis=0, keepdims=True) + EPS)


def _filter_kernel(hid_ref, t_ref, wh_ref, wl_ref, delta_ref, o_ref):
    hid = hid_ref[...]
    hid_hi, hid_lo = _split_bf16(hid)
    h = _dot(hid_hi, wh_ref[...]) + _dot(hid_lo, wh_ref[...]) + _dot(hid_hi, wl_ref[...])
    decay = jnp.exp(-t_ref[...] * delta_ref[...])
    wd = HY_WIDTH
    lag0 = jnp.logical_and(pl.program_id(0) == 0, lax.broadcasted_iota(jnp.int32, decay.shape, 0) == 0)
    for g in range(2 * HY_ORDER):
        val = h[:, g * wd:(g + 1) * wd] * decay
        o_ref[g] = jnp.where(lag0, 0.0, val) if g % 2 == 1 else val


def _hyena_filter_sides(n, w1, b1, w2, b2, w3, freq):
    hid, t = _hyena_filter_hidden(n, w1, b1, w2, b2, freq)
    wh, wl = _split_bf16(w3)
    g4, wd = 2 * HY_ORDER, HY_WIDTH
    sides = pl.pallas_call(
        _filter_kernel,
        grid=(n // ROW_TILE,),
        in_specs=[pl.BlockSpec((ROW_TILE, hid.shape[1]), lambda i: (i, 0)), pl.BlockSpec((ROW_TILE, 1), lambda i: (i, 0)),
                  pl.BlockSpec(wh.shape, lambda i: (0, 0)), pl.BlockSpec(wl.shape, lambda i: (0, 0)),
                  pl.BlockSpec((1, wd), lambda i: (0, 0))],
        out_specs=pl.BlockSpec((g4, ROW_TILE, wd), lambda i: (0, i, 0)),
        out_shape=jax.ShapeDtypeStruct((g4, n, wd), F32),
        compiler_params=pltpu.CompilerParams(dimension_semantics=("parallel",)),
        name="hyena_filter_sides",
    )(hid, t.reshape(n, 1), wh, wl, _hyena_deltas().reshape(1, wd))
    norm = jnp.sum(jnp.abs(sides), axis=1).reshape(HY_ORDER, 2, wd).sum(axis=1)
    return sides, 1.0 / (norm + EPS)


def _hyena_conv_kernel(p_ref, pp_ref, pn_ref, w_ref, b_ref, v_ref, x1_ref, x2_ref, *, n_ctx_tiles, n_tiles):
    first, last = _segment_edges(pl.program_id(0), n_ctx_tiles, n_tiles)
    u = _conv3(p_ref[...], pp_ref[...], pn_ref[...], w_ref[...], b_ref[...], first, last)
    wd = HY_WIDTH
    v_ref[...] = u[:, :wd]
    x1_ref[...] = u[:, wd:2 * wd]
    x2_ref[...] = u[:, 2 * wd:]


def _hyena_conv(p, conv_w, conv_b, n_ctx_tiles):
    B, T, C = p.shape
    n_tiles = T // ROW_TILE
    main, prev, nxt, const = _scan_specs(False, n_ctx_tiles, n_tiles, ROW_TILE, B)
    body = functools.partial(_hyena_conv_kernel, n_ctx_tiles=n_ctx_tiles, n_tiles=n_tiles)
    return pl.pallas_call(
        _per_sample(body, 3, 2),
        grid=(n_tiles,),
        in_specs=[main(C), prev(C), nxt(C), const((3, C)), const((1, C))],
        out_specs=[main(HY_WIDTH)] * 3,
        out_shape=[jax.ShapeDtypeStruct((B, T, HY_WIDTH), F32)] * 3,
        compiler_params=pltpu.CompilerParams(dimension_semantics=("parallel",), vmem_limit_bytes=VMEM_LIMIT),
        name="hyena_short_conv",
    )(p, p, p, conv_w, conv_b.reshape(1, C))


DFT_N2 = 256
DFT_ROW_TILE = SUBLANES


def _split_bf16(x):
    hi = x.astype(BF16)
    return hi, (x - hi.astype(F32)).astype(BF16)


def _dft_tables(n):
    L = 2 * n
    n1_full = L // DFT_N2
    nh = n1_full // 2
    nk = nh + 1
    nkp = -(-nk // SUBLANES) * SUBLANES
    k1 = jnp.arange(nkp, dtype=jnp.int32)
    valid = (k1 < nk)[:, None]

    def stage1(rows):
        n1 = jnp.arange(rows, dtype=jnp.int32)
        ang = (2.0 * math.pi / n1_full) * ((k1[:, None] * n1[None, :]) % n1_full).astype(F32)
        return jnp.concatenate([jnp.where(valid, jnp.cos(ang), 0.0), jnp.where(valid, -jnp.sin(ang), 0.0)], axis=0)

    n1 = jnp.arange(nh, dtype=jnp.int32)
    ang = (2.0 * math.pi / n1_full) * ((n1[:, None] * k1[None, :]) % n1_full).astype(F32)
    ck = jnp.where((k1 == 0) | (k1 == nh), 1.0, 2.0) * jnp.where(k1 < nk, 1.0, 0.0) / L
    stage3 = jnp.concatenate([jnp.cos(ang) * ck[None, :], -jnp.sin(ang) * ck[None, :]], axis=1)
    n2 = jnp.arange(DFT_N2, dtype=jnp.int32)
    idx = (n2[None, :, None] * (k1[:, None, None] + n1_full * n2[None, None, :])) % L
    ang2 = (2.0 * math.pi / L) * idx.astype(F32)
    gr, gi = jnp.cos(ang2), -jnp.sin(ang2)
    grt, git = jnp.swapaxes(gr, 1, 2), jnp.swapaxes(gi, 1, 2)
    m_fwd = jnp.concatenate([jnp.concatenate([grt, -git], axis=2), jnp.concatenate([git, grt], axis=2)], axis=1)
    m_inv = jnp.swapaxes(m_fwd, 1, 2)
    eye = jnp.eye(DFT_ROW_TILE, dtype=F32)
    return dict(nh=nh, nk=nk, nkp=nkp, f1=jnp.kron(stage1(nh), eye).astype(BF16),
                f3=jnp.kron(stage3, eye).astype(BF16), m_fwd=m_fwd.astype(BF16), m_inv=m_inv.astype(BF16))


def _dft_in_kernel(u_ref, f_ref, o_ref):
    rows, t2, wd = u_ref.shape
    nkp = o_ref.shape[1]
    r = _dot(f_ref[...], u_ref[...].reshape(rows * t2, wd).astype(BF16))
    o_ref[0] = r[:nkp * t2].reshape(nkp, t2, wd)
    o_ref[1] = r[nkp * t2:].reshape(nkp, t2, wd)


def _dft_in(u, f1):
    G, rows, n2, wd = u.shape
    t2 = DFT_ROW_TILE
    nkp = f1.shape[0] // (2 * t2)
    return pl.pallas_call(
        _dft_in_kernel,
        grid=(G, n2 // t2),
        in_specs=[pl.BlockSpec((None, rows, t2, wd), lambda g, j: (g, 0, j, 0)),
                  pl.BlockSpec(f1.shape, lambda g, j: (0, 0))],
        out_specs=pl.BlockSpec((None, 2, nkp, t2, wd), lambda g, j: (g, 0, 0, j, 0)),
        out_shape=jax.ShapeDtypeStruct((G, 2, nkp, n2, wd), F32),
        compiler_params=pltpu.CompilerParams(dimension_semantics=("parallel", "parallel"),
                                             vmem_limit_bytes=VMEM_LIMIT),
        name="hyena_dft_rows",
    )(u, f1)


def _spectrum_kernel(af_ref, ab_ref, sc_ref, m_ref, o_ref, *, nk):
    k1 = pl.program_id(0)
    half = DFT_N2

    @pl.when(k1 < nk)
    def _():
        m = m_ref[...]
        uf = _dot(m, jnp.concatenate([af_ref[0], af_ref[1]], axis=0).astype(BF16))
        ub = _dot(m, jnp.concatenate([ab_ref[0], ab_ref[1]], axis=0).astype(BF16))
        o_ref[0] = (uf[:half] + ub[:half]) * sc_ref[...]
        o_ref[1] = (uf[half:] - ub[half:]) * sc_ref[...]

    @pl.when(k1 >= nk)
    def _():
        o_ref[...] = jnp.zeros_like(o_ref)


def _spectrum(a, scale, tables):
    _, _, nkp, n2, wd = a.shape
    slab = lambda g_of: pl.BlockSpec((None, 2, None, n2, wd), g_of)
    return pl.pallas_call(
        functools.partial(_spectrum_kernel, nk=tables["nk"]),
        grid=(nkp, HY_ORDER),
        in_specs=[slab(lambda k, o: (2 * o, 0, k, 0, 0)), slab(lambda k, o: (2 * o + 1, 0, k, 0, 0)),
                  pl.BlockSpec((None, 1, wd), lambda k, o: (o, 0, 0)),
                  pl.BlockSpec((None, 2 * n2, 2 * n2), lambda k, o: (k, 0, 0))],
        out_specs=slab(lambda k, o: (o, 0, k, 0, 0)),
        out_shape=jax.ShapeDtypeStruct((HY_ORDER, 2, nkp, n2, wd), F32),
        compiler_params=pltpu.CompilerParams(dimension_semantics=("parallel", "parallel"),
                                             vmem_limit_bytes=VMEM_LIMIT),
        name="hyena_spectrum",
    )(a, a, scale, tables["m_fwd"])


def _dft_mid_kernel(a_ref, h_ref, mf_ref, mi_ref, o_ref, *, nk):
    k1 = pl.program_id(0)
    half = DFT_N2

    @pl.when(k1 < nk)
    def _():
        u = _dot(mf_ref[...], jnp.concatenate([a_ref[0], a_ref[1]], axis=0).astype(BF16))
        ur, ui = u[:half], u[half:]
        hr, hi = h_ref[0], h_ref[1]
        v = jnp.concatenate([ur * hr - ui * hi, ur * hi + ui * hr], axis=0)
        y = _dot(mi_ref[...], v.astype(BF16))
        o_ref[0] = y[:half]
        o_ref[1] = y[half:]

    @pl.when(k1 >= nk)
    def _():
        o_ref[...] = jnp.zeros_like(o_ref)


def _dft_mid(a, spec, order, tables):
    G, _, nkp, n2, wd = a.shape
    slab = lambda g_of: pl.BlockSpec((None, 2, None, n2, wd), g_of)
    mat = pl.BlockSpec((None, 2 * n2, 2 * n2), lambda k, g: (k, 0, 0))
    return pl.pallas_call(
        functools.partial(_dft_mid_kernel, nk=tables["nk"]),
        grid=(nkp, G),
        in_specs=[slab(lambda k, g: (g, 0, k, 0, 0)), slab(lambda k, g: (order, 0, k, 0, 0)), mat, mat],
        out_specs=slab(lambda k, g: (g, 0, k, 0, 0)),
        out_shape=jax.ShapeDtypeStruct(a.shape, F32),
        compiler_params=pltpu.CompilerParams(dimension_semantics=("parallel", "parallel"),
                                             vmem_limit_bytes=VMEM_LIMIT),
        name="hyena_dft_mid",
    )(a, spec, tables["m_fwd"], tables["m_inv"])


def _dft_out_kernel(b_ref, u_ref, x_ref, skip_ref, f_ref, o_ref):
    rows, t2, wd = u_ref.shape
    nkp = b_ref.shape[1]
    bb = jnp.concatenate([b_ref[0].reshape(nkp * t2, wd), b_ref[1].reshape(nkp * t2, wd)], axis=0)
    y = _dot(f_ref[...], bb.astype(BF16)).reshape(rows, t2, wd)
    o_ref[...] = x_ref[...] * (y + u_ref[...] * skip_ref[...])


def _dft_out(bq, u, xg, skip_row, f3):
    G, rows, n2, wd = u.shape
    nkp = bq.shape[2]
    t2 = DFT_ROW_TILE
    tile = pl.BlockSpec((None, rows, t2, wd), lambda g, j: (g, 0, j, 0))
    return pl.pallas_call(
        _dft_out_kernel,
        grid=(G, n2 // t2),
        in_specs=[pl.BlockSpec((None, 2, nkp, t2, wd), lambda g, j: (g, 0, 0, j, 0)), tile, tile,
                  pl.BlockSpec((1, wd), lambda g, j: (0, 0)), pl.BlockSpec(f3.shape, lambda g, j: (0, 0))],
        out_specs=tile,
        out_shape=jax.ShapeDtypeStruct(u.shape, F32),
        compiler_params=pltpu.CompilerParams(dimension_semantics=("parallel", "parallel"),
                                             vmem_limit_bytes=VMEM_LIMIT),
        name="hyena_dft_rows_inverse",
    )(bq, u, xg, skip_row, f3)


def _hyena_latent(v, x1, x2, sides, scale, skip):
    B, n, wd = v.shape
    tb = _dft_tables(n)
    view = lambda t: t.reshape(t.shape[0], tb["nh"], DFT_N2, wd)
    spec = _spectrum(_dft_in(view(sides), tb["f1"]), scale.reshape(HY_ORDER, 1, wd), tb)
    z = view(v)
    for o, xg in enumerate((x1, x2)):
        bq = _dft_mid(_dft_in(z, tb["f1"]), spec, o, tb)
        z = _dft_out(bq, z, view(xg), skip[o].astype(F32).reshape(1, wd), tb["f3"])
    return z.reshape(B, n, wd)


def _hyena_context(v, x1, x2, filt, skip):
    n = v.shape[1]
    spec = jnp.fft.rfft(filt, axis=0)
    z = v
    for o, xg in enumerate((x1, x2)):
        zf = jnp.fft.rfft(z, n=2 * n, axis=1)
        y = jnp.fft.irfft(zf * spec[:, o], n=2 * n, axis=1)[:, :n]
        z = xg * (y + z * skip[o].astype(F32))
    return z


def _merge_kernel(ya_ref, yb_ref, yc_ref, yd_ref, gate_ref, x_ref, g1_ref, sh_ref, sc_ref, ng_ref,
                  wb_ref, wo_ref, wr_ref, x1_ref, h2_ref, aff_ref):
    d = x_ref.shape[1]
    merged = None
    for i, y_ref in enumerate((ya_ref, yb_ref, yc_ref, yd_ref)):
        term = _sigmoid(gate_ref[:, i * d:(i + 1) * d]) * _dot(y_ref[...], wb_ref[i])
        merged = term if merged is None else merged + term
    x1 = x_ref[...] + g1_ref[...] * _dot(merged.astype(BF16), wo_ref[...])
    x1_ref[...] = x1
    h2 = _rms(x1) * ng_ref[...] * (1.0 + sc_ref[...]) + sh_ref[...]
    h2_ref[...] = h2.astype(h2_ref.dtype)
    h2_hi, h2_lo = _split_bf16(h2)
    logits = _dot(h2_hi, wr_ref[0]) + _dot(h2_lo, wr_ref[0]) + _dot(h2_hi, wr_ref[1])
    lane = lax.broadcasted_iota(jnp.int32, logits.shape, 1)
    logits = jnp.where(lane < N_EXPERTS, logits, MASK_NEG)
    e = jnp.exp(logits - jnp.max(logits, axis=1, keepdims=True))
    aff_ref[...] = e / jnp.sum(e, axis=1, keepdims=True)


def _merge(ys, gates, xc, g1, shift, scale, gain, wb, wo, wr, n_ctx_tiles):
    B, T, D = xc.shape
    w = BRANCH_W
    tile = lambda width: pl.BlockSpec((None, ROW_TILE, width), lambda b, i: (b, i, 0))
    mod_spec = pl.BlockSpec((None, None, 1, D), lambda b, i: (b, jnp.where(i < n_ctx_tiles, 1, 0), 0, 0))
    const = lambda shape: pl.BlockSpec(shape, lambda b, i: (0,) * len(shape))
    return pl.pallas_call(
        _merge_kernel,
        grid=(B, T // ROW_TILE),
        in_specs=[tile(w), tile(w), tile(w), tile(w), tile(N_BRANCHES * D), tile(D), mod_spec, mod_spec, mod_spec,
                  const((1, D)), const(wb.shape), const(wo.shape), const(wr.shape)],
        out_specs=[tile(D), tile(D), tile(LANES)],
        out_shape=[jax.ShapeDtypeStruct((B, T, D), F32), jax.ShapeDtypeStruct((B, T, D), BF16),
                   jax.ShapeDtypeStruct((B, T, LANES), F32)],
        compiler_params=pltpu.CompilerParams(dimension_semantics=("parallel", "parallel"),
                                             vmem_limit_bytes=VMEM_LIMIT),
        name="merge_out_router",
    )(*ys, gates, xc, g1, shift, scale, gain.reshape(1, D), wb, wo, wr)


def _expert_kernel(x_ref, s_ref, wg_ref, wu_ref, wd_ref, o_ref):
    x = x_ref[...]
    hid = _silu(_dot(x, wg_ref[...])) * _dot(x, wu_ref[...])
    o_ref[...] = _dot(hid.astype(BF16), wd_ref[...]) * s_ref[...]


def _experts(xg, score, wg, wu, wd):
    E, R, D = xg.shape
    F = wg.shape[2]
    tm = min(R, 512)
    return pl.pallas_call(
        _expert_kernel,
        grid=(E, R // tm),
        in_specs=[pl.BlockSpec((None, tm, D), lambda e, i: (e, i, 0)),
                  pl.BlockSpec((None, tm, 1), lambda e, i: (e, i, 0)),
                  pl.BlockSpec((None, D, F), lambda e, i: (e, 0, 0)),
                  pl.BlockSpec((None, D, F), lambda e, i: (e, 0, 0)),
                  pl.BlockSpec((None, F, D), lambda e, i: (e, 0, 0))],
        out_specs=pl.BlockSpec((None, tm, D), lambda e, i: (e, i, 0)),
        out_shape=jax.ShapeDtypeStruct((E, R, D), F32),
        compiler_params=pltpu.CompilerParams(dimension_semantics=("parallel", "arbitrary"),
                                             vmem_limit_bytes=VMEM_LIMIT),
        name="expert_swiglu",
    )(xg, score, wg, wu, wd)


def _expert_choice(h2, aff, seg_start, seg_len, wg, wu, wd):
    B, T, D = h2.shape
    cap = EC_CAPACITY * seg_len // N_EXPERTS
    a = aff[:, seg_start:seg_start + seg_len, :N_EXPERTS]
    score, idx = lax.top_k(jnp.swapaxes(a, 1, 2), cap)
    rows = idx + seg_start + (jnp.arange(B, dtype=idx.dtype) * T)[:, None, None]
    rows = jnp.swapaxes(rows, 0, 1).reshape(N_EXPERTS, B * cap)
    score = jnp.swapaxes(score, 0, 1).reshape(N_EXPERTS, B * cap, 1)
    xg = jnp.take(h2.reshape(B * T, D), rows, axis=0)
    ye = _experts(xg, score, wg, wu, wd)
    return rows.reshape(-1), ye.reshape(-1, D)


def _residual_kernel(x_ref, m_ref, g2_ref, gain_ref, o_ref, *, final):
    x = x_ref[...] + g2_ref[...] * m_ref[...]
    o_ref[...] = _rms(x) * gain_ref[...] if final else x


def _residual(x1, moe, g2, gain, final, n_ctx_tiles):
    B, T, D = x1.shape
    tile = pl.BlockSpec((None, ROW_TILE, D), lambda b, i: (b, i, 0))
    mod_spec = pl.BlockSpec((None, None, 1, D), lambda b, i: (b, jnp.where(i < n_ctx_tiles, 1, 0), 0, 0))
    return pl.pallas_call(
        functools.partial(_residual_kernel, final=final),
        grid=(B, T // ROW_TILE),
        in_specs=[tile, tile, mod_spec, pl.BlockSpec((1, D), lambda b, i: (0, 0))],
        out_specs=tile,
        out_shape=jax.ShapeDtypeStruct((B, T, D), F32),
        compiler_params=pltpu.CompilerParams(dimension_semantics=("parallel", "parallel")),
        name="moe_residual_final_norm" if final else "moe_residual",
    )(x1, moe, g2, gain.reshape(1, D))


def _to_col_major(t):
    b, n = t.shape[:2]
    rows = n // GRID_W
    return t.reshape((b, rows, GRID_W) + t.shape[2:]).swapaxes(1, 2).reshape(t.shape)


def _to_row_major(t):
    b, n = t.shape[:2]
    rows = n // GRID_W
    return t.reshape((b, GRID_W, rows) + t.shape[2:]).swapaxes(1, 2).reshape(t.shape)


def _pad_cols(w, width):
    return jnp.pad(w, ((0, 0), (0, width - w.shape[1])))


def kernel(x, c, ctx, c_ctx, w_ada, b_ada, norm1_g, norm2_g, w_in, ssd_conv_w, ssd_conv_b, ssd_dt_bias, ssd_a_log, ssd_d, ssd_norm_g, hy_conv_w, hy_conv_b, hy_w1, hy_b1, hy_w2, hy_b2, hy_w3, hy_freq, hy_skip, ml_conv_w, ml_conv_b, ml_i_bias, ml_f_bias, ml_norm_g, hg_lb_logits, hg_norm_g, w_branch, w_out, w_router, w_gate, w_up, w_down, final_g):
    B, n, D = x.shape
    n_ctx = ctx.shape[1]
    T = n_ctx + n
    assert n_ctx % ROW_TILE == 0 and n % ROW_TILE == 0 and n % GRID_W == 0
    assert n_ctx % SCAN_CHUNK == 0 and n % SCAN_CHUNK == 0
    n_ctx_tiles = n_ctx // ROW_TILE
    n_ctx_chunks = n_ctx // SCAN_CHUNK
    w = BRANCH_W

    p_lb = jax.nn.softmax(hg_lb_logits.astype(F32), axis=0)
    lower_bounds = jnp.maximum(jnp.cumsum(p_lb, axis=0) - p_lb[0], 0.0)
    xc = jnp.concatenate([ctx, x], axis=1)
    depth = w_in.shape[0]
    for l in range(depth):
        last_layer = l == depth - 1
        mod = jax.nn.silu(c) @ w_ada[l] + b_ada[l]
        mod_c = jax.nn.silu(c_ctx) @ w_ada[l] + b_ada[l]
        mods = jnp.stack([mod, jnp.broadcast_to(mod_c, mod.shape)], axis=1).reshape(B, 2, 6, 1, D)
        sh1, sc1, g1, sh2, sc2, g2 = (mods[:, :, i] for i in range(6))

        wl = w_in[l].astype(BF16)
        o_ssd, o_hy, o_ml, o_hg, o_gate = 0, SSD_COLS, SSD_COLS + HY_COLS, SSD_COLS + HY_COLS + ML_COLS, \
            SSD_COLS + HY_COLS + ML_COLS + HG_COLS
        cols = lambda a, b_: wl[:, a:b_]
        z, xs, bc, dt, qk, v, o, gts = _proj(xc, sh1, sc1, norm1_g[l], [
            cols(o_ssd, o_ssd + w), cols(o_ssd + w, o_ssd + 2 * w), cols(o_ssd + 2 * w, o_ssd + w + SSD_CONV_CH),
            _pad_cols(cols(o_ssd + w + SSD_CONV_CH, o_hy), LANES),
            cols(o_ml, o_ml + 2 * w), cols(o_ml + 2 * w, o_ml + 3 * w), cols(o_ml + 3 * w, o_ml + 4 * w),
            _pad_cols(cols(o_ml + 4 * w, o_hg), LANES)], n_ctx_tiles)
        p_hy, gate_pre = _proj(xc, sh1, sc1, norm1_g[l], [cols(o_hy, o_ml), cols(o_gate, o_gate + N_BRANCHES * D)],
                               n_ctx_tiles)
        xc_cm = jnp.concatenate([xc[:, :n_ctx], _to_col_major(xc[:, n_ctx:])], axis=1)
        hq, hi, hff, hfb, hgt = _proj(xc_cm, sh1, sc1, norm1_g[l],
                                      [cols(o_hg + i * w, o_hg + (i + 1) * w) for i in range(5)], n_ctx_tiles)

        ya = _ssd_branch(z, xs, bc, dt, ssd_conv_w[l], ssd_conv_b[l], ssd_dt_bias[l], ssd_a_log[l], ssd_d[l],
                         ssd_norm_g[l], n_ctx_chunks)
        yc = _mlstm_branch(qk, v, o, gts, ml_conv_w[l], ml_conv_b[l], ml_i_bias[l], ml_f_bias[l], ml_norm_g[l],
                           n_ctx_chunks)
        yd_cm = _hgrn_branch(hq, hi, hff, hfb, hgt, lower_bounds[l], hg_norm_g[l], n_ctx_chunks)
        yd = jnp.concatenate([yd_cm[:, :n_ctx], _to_row_major(yd_cm[:, n_ctx:])], axis=1)
        hv, hx1, hx2 = _hyena_conv(p_hy, hy_conv_w[l], hy_conv_b[l], n_ctx_tiles)
        hy_params = tuple(p_[l].astype(F32) for p_ in (hy_w1, hy_b1, hy_w2, hy_b2, hy_w3, hy_freq))
        lat = lambda t: t[:, n_ctx:]
        sides, side_scale = _hyena_filter_sides(n, *hy_params)
        yb_lat = _hyena_latent(lat(hv), lat(hx1), lat(hx2), sides, side_scale, hy_skip[l])
        if last_layer:
            yb_ctx = jnp.zeros((B, n_ctx, w), F32)
        else:
            head = lambda t: t[:, :n_ctx]
            yb_ctx = _hyena_context(head(hv), head(hx1), head(hx2), _hyena_filter_time(n_ctx, *hy_params), hy_skip[l])
        yb = jnp.concatenate([yb_ctx, yb_lat], axis=1).astype(BF16)

        wr = jnp.stack(_split_bf16(_pad_cols(w_router[l].astype(F32), LANES)))
        x1, h2, aff = _merge((ya, yb, yc, yd), gate_pre, xc, g1, sh2, sc2, norm2_g[l],
                             w_branch[l].astype(BF16), w_out[l].astype(BF16), wr, n_ctx_tiles)
        wg, wu, wd = w_gate[l].astype(BF16), w_up[l].astype(BF16), w_down[l].astype(BF16)
        rows, vals = _expert_choice(h2, aff, n_ctx, n, wg, wu, wd)
        moe = jnp.zeros((B * T, D), F32).at[rows].add(vals, mode="promise_in_bounds")
        if not last_layer:
            rows_c, vals_c = _expert_choice(h2, aff, 0, n_ctx, wg, wu, wd)
            moe = moe.at[rows_c].add(vals_c, mode="promise_in_bounds")
        xc = _residual(x1, moe.reshape(B, T, D), g2, final_g, last_layer, n_ctx_tiles)
    return xc[:, n_ctx:]
```
